```python
import math, functools
import jax, jax.numpy as jnp
from jax import lax
import numpy as np

D_MODEL = 1024
BATCH = 32
SEQ = 2048
DEPTH = 4

GRID_W = 64
CTX_LEN = 256
N_MIXERS = 3
CONF_KERNEL = 31
CONF_INNER = D_MODEL
SC_KERNEL = 3
S5_GROUP = 16
S5_GROUPS = D_MODEL // S5_GROUP
S5_STATE = 64
S5_DT_MIN = 1e-3
S5_DT_MAX = 1e-1
N_GROUPS = 4
EXPERTS_PER_GROUP = 8
N_EXPERTS = N_GROUPS * EXPERTS_PER_GROUP
TOP_K_IN_GROUP = 2
D_EXPERT = D_MODEL // 2
RMS_EPS = 1e-6
LN_EPS = 1e-5

kernel_name = 'hybrid_conv_s5_hmoe_diffusion_trunk'

F32 = jnp.float32


def _rmsnorm(x, g):
    xf = x.astype(F32)
    y = xf * lax.rsqrt(jnp.mean(xf * xf, axis=-1, keepdims=True) + RMS_EPS)
    return (y * g.astype(F32)).astype(x.dtype)


def _layernorm(x, g, b):
    xf = x.astype(F32)
    mu = jnp.mean(xf, axis=-1, keepdims=True)
    var = jnp.mean(jnp.square(xf - mu), axis=-1, keepdims=True)
    y = (xf - mu) * lax.rsqrt(var + LN_EPS)
    return (y * g.astype(F32) + b.astype(F32)).astype(x.dtype)


def _dwconv_grid(x, w):
    kh, kw, ch = w.shape
    return lax.conv_general_dilated(
        x, w.astype(x.dtype)[:, :, None, :], (1, 1),
        ((kh // 2, kh // 2), (kw // 2, kw // 2)),
        dimension_numbers=('NHWC', 'HWIO', 'NHWC'), feature_group_count=ch)


def _conv_latent(z, w):
    b, s, ch = z.shape
    rows = s // GRID_W
    y = _dwconv_grid(z.reshape(b, rows, GRID_W, ch), w)
    return y.reshape(b, s, ch)


def _conv_context(z, w1d):
    b, l, ch = z.shape
    y = _dwconv_grid(z.reshape(b, l, 1, ch), w1d[:, None, :])
    return y.reshape(b, l, ch)


def _conformer_module(h, conv_fn, w_in, dw_b, ln_g, ln_b, w_out):
    val, gate = jnp.split(h @ w_in, 2, axis=-1)
    z = val * jax.nn.sigmoid(gate)
    z = conv_fn(z) + dw_b
    z = _layernorm(z, ln_g, ln_b)
    return jax.nn.silu(z) @ w_out


def _short_conv_mixer(h, conv_fn, w_in, w_out):
    gb, gc, v = jnp.split(h @ w_in, 3, axis=-1)
    return (gb * conv_fn(gc * v)) @ w_out


def _s5_discretize(a_re, a_im, log_dt, b_re, b_im):
    a_re = jnp.minimum(a_re.astype(F32), -1e-4)
    a_im = a_im.astype(F32)
    dt = jnp.exp(log_dt.astype(F32))[:, None]
    mag = jnp.exp(dt * a_re)
    ang = dt * a_im
    abar_re = mag * jnp.cos(ang)
    abar_im = mag * jnp.sin(ang)
    den = a_re * a_re + a_im * a_im
    n_re = abar_re - 1.0
    n_im = abar_im
    k_re = (n_re * a_re + n_im * a_im) / den
    k_im = (n_im * a_re - n_re * a_im) / den
    b_re = b_re.astype(F32)
    b_im = b_im.astype(F32)
    bb_re = k_re[..., None] * b_re - k_im[..., None] * b_im
    bb_im = k_re[..., None] * b_im + k_im[..., None] * b_re
    return abar_re, abar_im, bb_re, bb_im


def _complex_linear_scan(abar_re, abar_im, bu_re, bu_im, reverse):
    length = bu_re.shape[1]
    a_re = jnp.broadcast_to(abar_re, (1, length) + abar_re.shape)
    a_im = jnp.broadcast_to(abar_im, (1, length) + abar_im.shape)

    def combine(e1, e2):
        a1r, a1i, b1r, b1i = e1
        a2r, a2i, b2r, b2i = e2
        return (a2r * a1r - a2i * a1i,
                a2r * a1i + a2i * a1r,
                a2r * b1r - a2i * b1i + b2r,
                a2r * b1i + a2i * b1r + b2i)

    _, _, hr, hi = lax.associative_scan(combine, (a_re, a_im, bu_re, bu_im),
                                        reverse=reverse, axis=1)
    return hr, hi


def _s5_states(u_g, disc, h0, reverse):
    abar_re, abar_im, bb_re, bb_im = disc
    bu_re = jnp.einsum('blgc,gpc->blgp', u_g, bb_re)
    bu_im = jnp.einsum('blgc,gpc->blgp', u_g, bb_im)
    if h0 is not None:
        h0r, h0i = h0
        pos = -1 if reverse else 0
        bu_re = bu_re.at[:, pos].add(abar_re * h0r - abar_im * h0i)
        bu_im = bu_im.at[:, pos].add(abar_re * h0i + abar_im * h0r)
    return _complex_linear_scan(abar_re, abar_im, bu_re, bu_im, reverse)


def _s5_readout(hr, hi, c_re, c_im):
    return (jnp.einsum('blgp,gcp->blgc', hr, c_re.astype(F32))
            - jnp.einsum('blgp,gcp->blgc', hi, c_im.astype(F32)))


def _s5_head(y, h, d, w_glu):
    b, l, dm = h.shape
    y = y.reshape(b, l, dm) + d.astype(F32) * h.astype(F32)
    y = jax.nn.gelu(y).astype(h.dtype)
    val, gate = jnp.split(y @ w_glu, 2, axis=-1)
    return val * jax.nn.sigmoid(gate)


def _s5_mixer(h_lat, h_ctx, a_re, a_im, log_dt, b_re, b_im, c_re, c_im, d, w_glu, need_ctx_out):
    def groups(h):
        b, l, _ = h.shape
        return h.astype(F32).reshape(b, l, S5_GROUPS, S5_GROUP)
    u_lat = groups(h_lat)
    u_ctx = groups(h_ctx)
    y_lat = 0.0
    y_ctx = 0.0
    for direction, reverse in enumerate((False, True)):
        disc = _s5_discretize(a_re[direction], a_im[direction], log_dt[direction],
                              b_re[direction], b_im[direction])
        cr, ci = _s5_states(u_ctx, disc, None, reverse)
        fin = (cr[:, 0], ci[:, 0]) if reverse else (cr[:, -1], ci[:, -1])
        if need_ctx_out:
            y_ctx = y_ctx + _s5_readout(cr, ci, c_re[direction], c_im[direction])
        lr, li = _s5_states(u_lat, disc, fin, reverse)
        y_lat = y_lat + _s5_readout(lr, li, c_re[direction], c_im[direction])
    out_lat = _s5_head(y_lat, h_lat, d, w_glu)
    out_ctx = _s5_head(y_ctx, h_ctx, d, w_glu) if need_ctx_out else None
    return out_lat, out_ctx


def _hier_moe(h, wg, bg, we, be, w13, w2):
    pg = jax.nn.softmax((h @ wg + bg).astype(F32), axis=-1)
    g_w, g_idx = lax.top_k(pg, 1)
    le = (h @ we + be).astype(F32).reshape(-1, N_GROUPS, EXPERTS_PER_GROUP)
    le = jnp.take_along_axis(le, g_idx[:, :, None], axis=1)[:, 0]
    pe = jax.nn.softmax(le, axis=-1)
    top_p, top_i = lax.top_k(pe, TOP_K_IN_GROUP)
    w = g_w * top_p / jnp.sum(top_p, axis=-1, keepdims=True)
    ids = g_idx * EXPERTS_PER_GROUP + top_i
    combine = jnp.sum(jax.nn.one_hot(ids, N_EXPERTS, dtype=F32) * w[..., None], axis=1)
    combine = combine.astype(h.dtype)
    y = jnp.zeros_like(h)
    for e in range(N_EXPERTS):
        a, g = jnp.split(h @ w13[e], 2, axis=-1)
        y = y + combine[:, e:e + 1] * ((jax.nn.silu(a) * g) @ w2[e])
    return y


def setup_inputs(seed: int = 0) -> dict:
    key = jax.random.key(seed)
    it = iter(list(jax.random.split(key, 40)))
    def nrm(shape, scale):
        return scale * jax.random.normal(next(it), shape, F32)
    n_conf = len([i for i in range(DEPTH) if i % N_MIXERS == 0])
    n_sc = len([i for i in range(DEPTH) if i % N_MIXERS == 1])
    n_s5 = len([i for i in range(DEPTH) if i % N_MIXERS == 2])
    d = D_MODEL
    G, P, CG = S5_GROUPS, S5_STATE, S5_GROUP
    n_idx = jnp.arange(P, dtype=F32)
    return {
        'x': nrm((BATCH, SEQ, d), 1.0),
        'c': nrm((BATCH, d), 1.0),
        'ctx': nrm((BATCH, CTX_LEN, d), 1.0),
        'c_ctx': nrm((d,), 1.0),
        'ada_w': nrm((DEPTH, d, 6 * d), 0.5 * d ** -0.5),
        'ada_b': nrm((DEPTH, 6 * d), 0.02),
        'norm1_g': 1.0 + nrm((DEPTH, d), 0.02),
        'norm2_g': 1.0 + nrm((DEPTH, d), 0.02),
        'conf_w_in': nrm((n_conf, d, 2 * CONF_INNER), d ** -0.5),
        'conf_dw': nrm((n_conf, CONF_KERNEL, CONF_INNER), CONF_KERNEL ** -0.5),
        'conf_dw_b': nrm((n_conf, CONF_INNER), 0.02),
        'conf_ln_g': 1.0 + nrm((n_conf, CONF_INNER), 0.02),
        'conf_ln_b': nrm((n_conf, CONF_INNER), 0.02),
        'conf_w_out': nrm((n_conf, CONF_INNER, d), CONF_INNER ** -0.5),
        'sc_w_in': nrm((n_sc, d, 3 * d), d ** -0.5),
        'sc_conv': nrm((n_sc, SC_KERNEL, d), SC_KERNEL ** -0.5),
        'sc_w_out': nrm((n_sc, d, d), d ** -0.5),
        's5_a_re': -0.5 + nrm((n_s5, 2, G, P), 0.01),
        's5_a_im': math.pi * n_idx + nrm((n_s5, 2, G, P), 0.01),
        's5_log_dt': jax.random.uniform(next(it), (n_s5, 2, G), F32,
                                        math.log(S5_DT_MIN), math.log(S5_DT_MAX)),
        's5_b_re': nrm((n_s5, 2, G, P, CG), (2 * CG) ** -0.5),
        's5_b_im': nrm((n_s5, 2, G, P, CG), (2 * CG) ** -0.5),
        's5_c_re': nrm((n_s5, 2, G, CG, P), P ** -0.5),
        's5_c_im': nrm((n_s5, 2, G, CG, P), P ** -0.5),
        's5_d': nrm((n_s5, d), 1.0),
        's5_w_glu': nrm((n_s5, d, 2 * d), d ** -0.5),
        'moe_wg': nrm((DEPTH, d, N_GROUPS), d ** -0.5),
        'moe_bg': nrm((DEPTH, N_GROUPS), 0.01),
        'moe_we': nrm((DEPTH, d, N_EXPERTS), d ** -0.5),
        'moe_be': nrm((DEPTH, N_EXPERTS), 0.01),
        'moe_w13': nrm((DEPTH, N_EXPERTS, d, 2 * D_EXPERT), d ** -0.5),
        'moe_w2': nrm((DEPTH, N_EXPERTS, D_EXPERT, d), D_EXPERT ** -0.5),
        'final_g': 1.0 + nrm((d,), 0.02),
    }


def reference(x, c, ctx, c_ctx, ada_w, ada_b, norm1_g, norm2_g,
              conf_w_in, conf_dw, conf_dw_b, conf_ln_g, conf_ln_b, conf_w_out,
              sc_w_in, sc_conv, sc_w_out,
              s5_a_re, s5_a_im, s5_log_dt, s5_b_re, s5_b_im, s5_c_re, s5_c_im, s5_d, s5_w_glu,
              moe_wg, moe_bg, moe_we, moe_be, moe_w13, moe_w2, final_g):
    b, s, dm = x.shape
    lc = ctx.shape[1]
    silu_c = jax.nn.silu(c)
    silu_cc = jax.nn.silu(c_ctx)
    for i in range(DEPTH):
        kind = i % N_MIXERS
        j = i // N_MIXERS
        update_ctx = i < DEPTH - 1
        need_ctx_in = update_ctx or kind == 2
        sh1, sc1, g1, sh2, sc2, g2 = jnp.split((silu_c @ ada_w[i] + ada_b[i])[:, None, :], 6, axis=-1)
        csh1, csc1, cg1, csh2, csc2, cg2 = jnp.split(silu_cc @ ada_w[i] + ada_b[i], 6, axis=-1)

        hx = _rmsnorm(x, norm1_g[i]) * (1.0 + sc1) + sh1
        hc = _rmsnorm(ctx, norm1_g[i]) * (1.0 + csc1) + csh1 if need_ctx_in else None
        if kind == 0:
            args = (conf_w_in[j], conf_dw_b[j], conf_ln_g[j], conf_ln_b[j], conf_w_out[j])
            mx = _conformer_module(hx, functools.partial(_conv_latent, w=conf_dw[j][:, None, :]), *args)
            mc = (_conformer_module(hc, functools.partial(_conv_context, w1d=conf_dw[j]), *args)
                  if update_ctx else None)
        elif kind == 1:
            mx = _short_conv_mixer(hx, functools.partial(_conv_latent, w=sc_conv[j][None, :, :]),
                                   sc_w_in[j], sc_w_out[j])
            mc = (_short_conv_mixer(hc, functools.partial(_conv_context, w1d=sc_conv[j]),
                                    sc_w_in[j], sc_w_out[j]) if update_ctx else None)
        else:
            mx, mc = _s5_mixer(hx, hc, s5_a_re[j], s5_a_im[j], s5_log_dt[j], s5_b_re[j], s5_b_im[j],
                               s5_c_re[j], s5_c_im[j], s5_d[j], s5_w_glu[j], update_ctx)
        x = x + g1 * mx
        if update_ctx:
            ctx = ctx + cg1 * mc

        hx2 = _rmsnorm(x, norm2_g[i]) * (1.0 + sc2) + sh2
        moe_args = (moe_wg[i], moe_bg[i], moe_we[i], moe_be[i], moe_w13[i], moe_w2[i])
        if update_ctx:
            hc2 = _rmsnorm(ctx, norm2_g[i]) * (1.0 + csc2) + csh2
            tokens = jnp.concatenate([hx2.reshape(-1, dm), hc2.reshape(-1, dm)], axis=0)
            y = _hier_moe(tokens, *moe_args)
            n_lat = b * s
            x = x + g2 * y[:n_lat].reshape(b, s, dm)
            ctx = ctx + cg2 * y[n_lat:].reshape(b, lc, dm)
        else:
            x = x + g2 * _hier_moe(hx2.reshape(-1, dm), *moe_args).reshape(b, s, dm)
    return _rmsnorm(x, final_g)
```

```python
import functools

import jax
import jax.numpy as jnp
from jax import lax
from jax.experimental import pallas as pl
from jax.experimental.pallas import tpu as pltpu

F32 = jnp.float32
BF16 = jnp.bfloat16
I32 = jnp.int32
HIGHEST = lax.Precision.HIGHEST

GRID_W = 64
N_MIXERS = 3
TOP_K = 2
RMS_EPS = 1e-6
LN_EPS = 1e-5
S5_DT_FLOOR = -1e-4

V7X_VMEM_BYTES = 64 * 1024 * 1024
V7X_LANES = 128
V7X_SUBLANES = 8
VMEM_LIMIT_BYTES = V7X_VMEM_BYTES - 8 * 1024 * 1024

S5_CHUNK = 16
MOE_TILE = 128
MOE_BATCH = V7X_SUBLANES
NEG_BIG = -1e30


def _cparams(n_axes):
    return pltpu.CompilerParams(dimension_semantics=("arbitrary",) * n_axes,
                                vmem_limit_bytes=VMEM_LIMIT_BYTES)


def _row_tile(seq_len, want):
    t = min(seq_len, want)
    while seq_len % t or t % V7X_SUBLANES:
        t -= 1
    return t


def _modnorm(x, g, sc, sh):
    y = x * lax.rsqrt(jnp.mean(x * x, axis=-1, keepdims=True) + RMS_EPS)
    return (y * g) * (1.0 + sc) + sh


def _silu(v):
    return v * jax.nn.sigmoid(v)


def _mod_spec(n_mod, tm, seq_len, d, j):
    if n_mod == 1:
        return pl.BlockSpec((1, 1, d), lambda t: (0, 0, j))
    return pl.BlockSpec((1, 1, d), lambda t: ((t * tm) // seq_len, 0, j))


def _row_spec(tm, d):
    return pl.BlockSpec((tm, d), lambda t: (t, 0))


def _full_spec(shape):
    nd = len(shape)
    return pl.BlockSpec(shape, lambda *_: (0,) * nd)


def _ada_kernel(c_ref, w_ref, b_ref, o_ref):
    o_ref[0] = jnp.dot(_silu(c_ref[...]), w_ref[0], preferred_element_type=F32,
                       precision=HIGHEST) + b_ref[0]


def _ada_table(cin, ada_w, ada_b):
    depth, d, d6 = ada_w.shape
    r = cin.shape[0]
    tn = d6 // 6
    return pl.pallas_call(
        _ada_kernel,
        grid=(depth, d6 // tn),
        in_specs=[pl.BlockSpec((r, d), lambda i, j: (0, 0)),
                  pl.BlockSpec((1, d, tn), lambda i, j: (i, 0, j)),
                  pl.BlockSpec((1, 1, tn), lambda i, j: (i, 0, j))],
        out_specs=pl.BlockSpec((1, r, tn), lambda i, j: (i, 0, j)),
        out_shape=jax.ShapeDtypeStruct((depth, r, d6), F32),
        compiler_params=_cparams(2),
        name="ada_table",
    )(cin, ada_w, ada_b.reshape(depth, 1, d6))


def _conf_in_kernel(x_ref, g_ref, sc_ref, sh_ref, w_ref, z_ref):
    h = _modnorm(x_ref[...], g_ref[...], sc_ref[0], sh_ref[0])
    y = jnp.dot(h.astype(BF16), w_ref[...], preferred_element_type=F32)
    ci = y.shape[1] // 2
    z_ref[...] = (y[:, :ci] * jax.nn.sigmoid(y[:, ci:])).astype(z_ref.dtype)


def _conf_in(x, mods, seq_len, norm_g, w_in):
    t, d = x.shape
    tm = _row_tile(seq_len, 512)
    ci = w_in.shape[1] // 2
    return pl.pallas_call(
        _conf_in_kernel,
        grid=(t // tm,),
        in_specs=[_row_spec(tm, d), _full_spec((1, d)),
                  _mod_spec(mods.shape[0], tm, seq_len, d, 1),
                  _mod_spec(mods.shape[0], tm, seq_len, d, 0),
                  _full_spec(w_in.shape)],
        out_specs=_row_spec(tm, ci),
        out_shape=jax.ShapeDtypeStruct((t, ci), BF16),
        compiler_params=_cparams(1),
        name="conf_in",
    )(x, norm_g.reshape(1, d), mods, mods, w_in)


def _dwconv_kernel(z_ref, w_ref, b_ref, o_ref, pad_scr, *, stride, chunk):
    seq_len, cb = z_ref.shape
    taps = w_ref.shape[0]
    pad = (taps // 2) * stride
    pad_scr[pl.ds(0, pad), :] = jnp.zeros((pad, cb), F32)
    pad_scr[pl.ds(pad + seq_len, pad), :] = jnp.zeros((pad, cb), F32)
    pad_scr[pl.ds(pad, seq_len), :] = z_ref[...].astype(F32)
    w = w_ref[...]
    bias = b_ref[...]

    def one_chunk(r0):
        acc = jnp.broadcast_to(bias, (chunk, cb))
        for k in range(taps):
            acc = acc + w[k:k + 1, :] * pad_scr[pl.ds(r0 + k * stride, chunk), :]
        o_ref[pl.ds(r0, chunk), :] = acc.astype(o_ref.dtype)

    n_chunk = seq_len // chunk
    if stride % V7X_SUBLANES:
        for i in range(n_chunk):
            one_chunk(i * chunk)
    else:
        def body(i, carry):
            one_chunk(pl.multiple_of(i * chunk, chunk))
            return carry
        lax.fori_loop(0, n_chunk, body, 0)


def _dwconv(z, seq_len, stride, w, b):
    t, c = z.shape
    taps = w.shape[0]
    cb = min(c, 2 * V7X_LANES)
    chunk = _row_tile(seq_len, 64)
    pad = (taps // 2) * stride
    return pl.pallas_call(
        functools.partial(_dwconv_kernel, stride=stride, chunk=chunk),
        grid=(t // seq_len, c // cb),
        in_specs=[pl.BlockSpec((seq_len, cb), lambda s, j: (s, j)),
                  pl.BlockSpec((taps, cb), lambda s, j: (0, j)),
                  pl.BlockSpec((1, cb), lambda s, j: (0, j))],
        out_specs=pl.BlockSpec((seq_len, cb), lambda s, j: (s, j)),
        out_shape=jax.ShapeDtypeStruct((t, c), BF16),
        scratch_shapes=[pltpu.VMEM((seq_len + 2 * pad, cb), F32)],
        compiler_params=_cparams(2),
        name="dwconv",
    )(z, w, b.reshape(1, c))


def _conf_out_kernel(z_ref, lg_ref, lb_ref, w_ref, x_ref, g1_ref, o_ref):
    z = z_ref[...].astype(F32)
    mu = jnp.mean(z, axis=-1, keepdims=True)
    zc = z - mu
    var = jnp.mean(zc * zc, axis=-1, keepdims=True)
    y = zc * lax.rsqrt(var + LN_EPS) * lg_ref[...] + lb_ref[...]
    m = jnp.dot(_silu(y).astype(BF16), w_ref[...], preferred_element_type=F32)
    o_ref[...] = x_ref[...] + g1_ref[0] * m


def _conf_out(z, x, mods, seq_len, ln_g, ln_b, w_out):
    t, d = x.shape
    ci = z.shape[1]
    tm = _row_tile(seq_len, 512)
    return pl.pallas_call(
        _conf_out_kernel,
        grid=(t // tm,),
        in_specs=[_row_spec(tm, ci), _full_spec((1, ci)), _full_spec((1, ci)),
                  _full_spec(w_out.shape), _row_spec(tm, d),
                  _mod_spec(mods.shape[0], tm, seq_len, d, 2)],
        out_specs=_row_spec(tm, d),
        out_shape=jax.ShapeDtypeStruct((t, d), F32),
        compiler_params=_cparams(1),
        name="conf_out",
    )(z, ln_g.reshape(1, ci), ln_b.reshape(1, ci), w_out, x, mods)


def _conformer(x, mods, seq_len, stride, norm_g, w_in, dw, dw_b, ln_g, ln_b, w_out):
    z = _conf_in(x, mods, seq_len, norm_g, w_in)
    z = _dwconv(z, seq_len, stride, dw, dw_b)
    return _conf_out(z, x, mods, seq_len, ln_g, ln_b, w_out)


def _sc_kernel(x_ref, g_ref, sc_ref, sh_ref, win_ref, cw_ref, wout_ref, g1_ref, o_ref, *, period):
    x = x_ref[...]
    tm, d = x.shape
    h = _modnorm(x, g_ref[...], sc_ref[0], sh_ref[0])
    y = jnp.dot(h.astype(BF16), win_ref[...], preferred_element_type=F32)
    gb, gc, v = y[:, :d], y[:, d:2 * d], y[:, 2 * d:]
    u = gc * v
    pos = lax.broadcasted_iota(I32, (tm, 1), 0) % period
    u_prev = jnp.where(pos == 0, 0.0, pltpu.roll(u, 1, 0))
    u_next = jnp.where(pos == period - 1, 0.0, pltpu.roll(u, tm - 1, 0))
    cw = cw_ref[...]
    conv = cw[0:1, :] * u_prev + cw[1:2, :] * u + cw[2:3, :] * u_next
    m = jnp.dot((gb * conv).astype(BF16), wout_ref[...], preferred_element_type=F32)
    o_ref[...] = x + g1_ref[0] * m


def _short_conv(x, mods, seq_len, period, norm_g, w_in, conv_w, w_out):
    t, d = x.shape
    tm = _row_tile(seq_len, 512)
    assert tm % period == 0 and conv_w.shape[0] == 3
    return pl.pallas_call(
        functools.partial(_sc_kernel, period=period),
        grid=(t // tm,),
        in_specs=[_row_spec(tm, d), _full_spec((1, d)),
                  _mod_spec(mods.shape[0], tm, seq_len, d, 1),
                  _mod_spec(mods.shape[0], tm, seq_len, d, 0),
                  _full_spec(w_in.shape), _full_spec(conv_w.shape), _full_spec(w_out.shape),
                  _mod_spec(mods.shape[0], tm, seq_len, d, 2)],
        out_specs=_row_spec(tm, d),
        out_shape=jax.ShapeDtypeStruct((t, d), F32),
        compiler_params=_cparams(1),
        name="short_conv",
    )(x, norm_g.reshape(1, d), mods, mods, w_in, conv_w, w_out, mods)


def _s5_prep_kernel(are_ref, aim_ref, ldt_ref, bre_ref, bim_ref, cre_ref, cim_ref,
                    a16r_ref, a16i_ref, winr_ref, wini_ref, woutr_ref, wouti_ref, k_ref, *, reverse):
    a_re = jnp.minimum(are_ref[0], S5_DT_FLOOR)
    a_im = aim_ref[0]
    dt = jnp.exp(ldt_ref[0])
    b_re, b_im = bre_ref[0], bim_ref[0]
    c_re, c_im = cre_ref[0], cim_ref[0]
    cg = b_re.shape[1]
    tc = S5_CHUNK

    def power(n):
        mag = jnp.exp((n * dt) * a_re)
        ang = (n * dt) * a_im
        return mag * jnp.cos(ang), mag * jnp.sin(ang)

    abar_re, abar_im = power(1)
    den = a_re * a_re + a_im * a_im
    n_re = abar_re - 1.0
    n_im = abar_im
    k_re = (n_re * a_re + n_im * a_im) / den
    k_im = (n_im * a_re - n_re * a_im) / den
    bb_re = k_re * b_re - k_im * b_im
    bb_im = k_re * b_im + k_im * b_re

    e16r, e16i = power(tc)
    a16r_ref[0] = e16r
    a16i_ref[0] = e16i
    for t in range(tc):
        er, ei = power(t if reverse else tc - 1 - t)
        winr_ref[0, :, t * cg:(t + 1) * cg, :] = er * bb_re - ei * bb_im
        wini_ref[0, :, t * cg:(t + 1) * cg, :] = er * bb_im + ei * bb_re
        er, ei = power(tc - t if reverse else t + 1)
        woutr_ref[0, :, t * cg:(t + 1) * cg, :] = c_re * er - c_im * ei
        wouti_ref[0, :, t * cg:(t + 1) * cg, :] = -(c_re * ei + c_im * er)
        er, ei = power(t)
        m_re = c_re * er - c_im * ei
        m_im = c_re * ei + c_im * er
        k_ref[0, :, t * cg:(t + 1) * cg, :] = (
            jnp.einsum('gap,gbp->gab', m_re, bb_re, preferred_element_type=F32, precision=HIGHEST)
            - jnp.einsum('gap,gbp->gab', m_im, bb_im, preferred_element_type=F32, precision=HIGHEST))


def _s5_prep(a_re, a_im, log_dt, b_re, b_im, c_re, c_im, reverse):
    g, p = a_re.shape
    cg = b_re.shape[2]
    gb = min(g, 8)
    rows = S5_CHUNK * cg
    a4 = lambda a: a.reshape(1, g, 1, p)
    bt = lambda b: jnp.swapaxes(b, 1, 2).reshape(1, g, cg, p)
    spec_a = pl.BlockSpec((1, gb, 1, p), lambda i: (0, i, 0, 0))
    spec_b = pl.BlockSpec((1, gb, cg, p), lambda i: (0, i, 0, 0))
    spec_w = pl.BlockSpec((1, gb, rows, p), lambda i: (0, i, 0, 0))
    outs = pl.pallas_call(
        functools.partial(_s5_prep_kernel, reverse=reverse),
        grid=(g // gb,),
        in_specs=[spec_a, spec_a, pl.BlockSpec((1, gb, 1, 1), lambda i: (0, i, 0, 0)),
                  spec_b, spec_b, spec_b, spec_b],
        out_specs=[spec_a, spec_a, spec_w, spec_w, spec_w, spec_w,
                   pl.BlockSpec((1, gb, rows, cg), lambda i: (0, i, 0, 0))],
        out_shape=[jax.ShapeDtypeStruct((1, g, 1, p), F32)] * 2
        + [jax.ShapeDtypeStruct((1, g, rows, p), F32)] * 4
        + [jax.ShapeDtypeStruct((1, g, rows, cg), F32)],
        compiler_params=_cparams(1),
        name="s5_prep",
    )(a4(a_re), a4(a_im), log_dt.reshape(1, g, 1, 1), bt(b_re), bt(b_im),
      c_re.reshape(1, g, cg, p), c_im.reshape(1, g, cg, p))
    a16r, a16i, winr, wini, woutr, wouti, kk = [o[0] for o in outs]
    return a16r, a16i, winr, wini, woutr, wouti, kk


def _pair_blockdiag(w):
    g, r, p = w.shape
    w = w.reshape(g // 2, 2, r, p)
    z = jnp.zeros_like(w[:, 0])
    top = jnp.concatenate([w[:, 0], z], axis=-1)
    bot = jnp.concatenate([z, w[:, 1]], axis=-1)
    return jnp.concatenate([top, bot], axis=1)


def _toeplitz(kk, cg, reverse):
    g = kk.shape[0]
    tc = S5_CHUNK
    k5 = kk.reshape(g, tc, cg, cg)
    t_in = jnp.arange(tc)[None, :]
    t_out = jnp.arange(tc)[:, None]
    lag = (t_in - t_out) if reverse else (t_out - t_in)
    blocks = jnp.where((lag >= 0)[None, :, :, None, None], k5[:, jnp.clip(lag, 0, tc - 1)], 0.0)
    return blocks.transpose(0, 1, 3, 2, 4).reshape(g, tc * cg, tc * cg)


def _s5_core_kernel(v_ref, a16r_ref, a16i_ref, winr_ref, wini_ref, woutr_ref, wouti_ref, tz_ref,
                    y_ref, sre, sim, hre, him, *, n_ctx_chunks, n_seq):
    n_col = v_ref.shape[2]
    n_chunks = n_col // n_seq
    r2 = 2 * v_ref.shape[1]
    v = v_ref[...].reshape(r2, n_col)
    for direction in range(2):
        sre[...] = lax.dot_general(v, winr_ref[direction, 0], (((0,), (0,)), ((), ())),
                                   preferred_element_type=F32)
        sim[...] = lax.dot_general(v, wini_ref[direction, 0], (((0,), (0,)), ((), ())),
                                   preferred_element_type=F32)
        ar = a16r_ref[direction, 0]
        ai = a16i_ref[direction, 0]

        def step(k, carry):
            h_r, h_i = carry
            rows = pl.ds(pl.multiple_of(k * n_seq, n_seq), n_seq)
            hre[rows, :] = h_r
            him[rows, :] = h_i
            return (ar * h_r - ai * h_i + sre[rows, :], ar * h_i + ai * h_r + sim[rows, :])

        zero = jnp.zeros((n_seq, sre.shape[1]), F32)
        if direction == 0:
            lax.fori_loop(0, n_chunks, step, (zero, zero))
        else:
            mid = lax.fori_loop(0, n_ctx_chunks,
                                lambda i, c: step(n_ctx_chunks - 1 - i, c), (zero, zero))
            lax.fori_loop(0, n_chunks - n_ctx_chunks,
                          lambda i, c: step(n_chunks - 1 - i, c), mid)

        carried = (lax.dot_general(woutr_ref[direction, 0], hre[...].astype(BF16),
                                   (((1,), (1,)), ((), ())), preferred_element_type=F32)
                   + lax.dot_general(wouti_ref[direction, 0], him[...].astype(BF16),
                                     (((1,), (1,)), ((), ())), preferred_element_type=F32))
        for half in range(2):
            local = jnp.dot(tz_ref[direction, half], v_ref[half], preferred_element_type=F32)
            part = carried[half * (r2 // 2):(half + 1) * (r2 // 2), :] + local
            if direction == 0:
                y_ref[half] = part
            else:
                y_ref[half] = y_ref[half] + part


def _s5_core(v, ops_fwd, ops_bwd, n_ctx_chunks, n_seq):
    g, r, n_col = v.shape
    cg = r // S5_CHUNK
    stacked = []
    for idx in range(6):
        stacked.append(jnp.stack([ops_fwd[idx], ops_bwd[idx]]))
    a16r, a16i, winr, wini, woutr, wouti = stacked
    p = a16r.shape[-1]
    pair_vec = lambda a: a.reshape(2, g // 2, 1, 2 * p)
    pair_mat = lambda w: jnp.stack([_pair_blockdiag(w[0]), _pair_blockdiag(w[1])]).astype(BF16)
    tz = jnp.stack([_toeplitz(ops_fwd[6], cg, False), _toeplitz(ops_bwd[6], cg, True)]).astype(BF16)
    vec_spec = pl.BlockSpec((2, 1, 1, 2 * p), lambda i: (0, i, 0, 0))
    mat_spec = pl.BlockSpec((2, 1, 2 * r, 2 * p), lambda i: (0, i, 0, 0))
    return pl.pallas_call(
        functools.partial(_s5_core_kernel, n_ctx_chunks=n_ctx_chunks, n_seq=n_seq),
        grid=(g // 2,),
        in_specs=[pl.BlockSpec((2, r, n_col), lambda i: (i, 0, 0)),
                  vec_spec, vec_spec, mat_spec, mat_spec, mat_spec, mat_spec,
                  pl.BlockSpec((2, 2, r, r), lambda i: (0, i, 0, 0))],
        out_specs=pl.BlockSpec((2, r, n_col), lambda i: (i, 0, 0)),
        out_shape=jax.ShapeDtypeStruct((g, r, n_col), F32),
        scratch_shapes=[pltpu.VMEM((n_col, 2 * p), F32)] * 4,
        compiler_params=_cparams(1),
        name="s5_core",
    )(v, pair_vec(a16r), pair_vec(a16i), pair_mat(winr), pair_mat(wini),
      pair_mat(woutr), pair_mat(wouti), tz)


def _modnorm_kernel(x_ref, g_ref, sc_ref, sh_ref, o_ref):
    o_ref[...] = _modnorm(x_ref[...], g_ref[...], sc_ref[0], sh_ref[0]).astype(o_ref.dtype)


def _modnorm_call(x, mods, seq_len, norm_g):
    t, d = x.shape
    tm = _row_tile(seq_len, 1024)
    return pl.pallas_call(
        _modnorm_kernel,
        grid=(t // tm,),
        in_specs=[_row_spec(tm, d), _full_spec((1, d)),
                  _mod_spec(mods.shape[0], tm, seq_len, d, 1),
                  _mod_spec(mods.shape[0], tm, seq_len, d, 0)],
        out_specs=_row_spec(tm, d),
        out_shape=jax.ShapeDtypeStruct((t, d), BF16),
        compiler_params=_cparams(1),
        name="modnorm",
    )(x, norm_g.reshape(1, d), mods, mods)


def _s5_head_kernel(x_ref, y_ref, g_ref, sc_ref, sh_ref, d_ref, w_ref, g1_ref, o_ref):
    x = x_ref[...]
    h = _modnorm(x, g_ref[...], sc_ref[0], sh_ref[0])
    a = jax.nn.gelu(y_ref[...] + d_ref[...] * h, approximate=True)
    z = jnp.dot(a.astype(BF16), w_ref[...], preferred_element_type=F32)
    dm = z.shape[1] // 2
    o_ref[...] = x + g1_ref[0] * (z[:, :dm] * jax.nn.sigmoid(z[:, dm:]))


def _s5_head(x, y, mods, seq_len, norm_g, d_skip, w_glu):
    t, d = x.shape
    tm = _row_tile(seq_len, 512)
    return pl.pallas_call(
        _s5_head_kernel,
        grid=(t // tm,),
        in_specs=[_row_spec(tm, d), _row_spec(tm, d), _full_spec((1, d)),
                  _mod_spec(mods.shape[0], tm, seq_len, d, 1),
                  _mod_spec(mods.shape[0], tm, seq_len, d, 0),
                  _full_spec((1, d)), _full_spec(w_glu.shape),
                  _mod_spec(mods.shape[0], tm, seq_len, d, 2)],
        out_specs=_row_spec(tm, d),
        out_shape=jax.ShapeDtypeStruct((t, d), F32),
        compiler_params=_cparams(1),
        name="s5_head",
    )(x, y, norm_g.reshape(1, d), mods, mods, d_skip.reshape(1, d), w_glu, mods)


def _s5_mixer(x, ctx, mods_x, mods_c, n_seq, norm_g, a_re, a_im, log_dt, b_re, b_im, c_re, c_im,
              d_skip, w_glu):
    d = x.shape[1]
    s_len, c_len = x.shape[0] // n_seq, ctx.shape[0] // n_seq
    g = a_re.shape[1]
    cg = d // g
    tc = S5_CHUNK
    ops = [_s5_prep(a_re[k], a_im[k], log_dt[k], b_re[k], b_im[k], c_re[k], c_im[k], bool(k))
           for k in range(2)]
    hx = _modnorm_call(x, mods_x, s_len, norm_g).reshape(n_seq, s_len, d)
    hc = _modnorm_call(ctx, mods_c, c_len, norm_g).reshape(n_seq, c_len, d)
    n_chunks = (s_len + c_len) // tc
    u = jnp.concatenate([hc, hx], axis=1).reshape(n_seq, n_chunks, tc, g, cg)
    v = u.transpose(3, 2, 4, 1, 0).reshape(g, tc * cg, n_chunks * n_seq)
    y = _s5_core(v, ops[0], ops[1], c_len // tc, n_seq)
    y = y.reshape(g, tc, cg, n_chunks, n_seq).transpose(4, 3, 1, 0, 2).reshape(n_seq, s_len + c_len, d)
    y_c = y[:, :c_len].reshape(n_seq * c_len, d)
    y_x = y[:, c_len:].reshape(n_seq * s_len, d)
    x = _s5_head(x, y_x, mods_x, s_len, norm_g, d_skip, w_glu)
    ctx = _s5_head(ctx, y_c, mods_c, c_len, norm_g, d_skip, w_glu)
    return x, ctx


def _router_kernel(x_ref, g_ref, sc_ref, sh_ref, wr_ref, br_ref, h_ref, r_ref, *, n_groups, epg):
    h = _modnorm(x_ref[...], g_ref[...], sc_ref[0], sh_ref[0])
    h_ref[...] = h
    logits = jnp.dot(h, wr_ref[...], preferred_element_type=F32, precision=HIGHEST) + br_ref[...]
    col = lax.broadcasted_iota(I32, logits.shape, 1)
    far = jnp.int32(1 << 20)

    def first_max(vals):
        m = jnp.max(vals, axis=-1, keepdims=True)
        return m, jnp.min(jnp.where(vals == m, col, far), axis=-1, keepdims=True)

    is_group = col < n_groups
    gl = jnp.where(is_group, logits, NEG_BIG)
    gmax, gidx = first_max(gl)
    gsum = jnp.sum(jnp.where(is_group, jnp.exp(gl - gmax), 0.0), axis=-1, keepdims=True)
    g_w = 1.0 / gsum
    lo = n_groups + gidx * epg
    le = jnp.where((col >= lo) & (col < lo + epg), logits, NEG_BIG)
    m1, i1 = first_max(le)
    m2, i2 = first_max(jnp.where(col == i1, NEG_BIG, le))
    ratio = jnp.exp(m2 - m1)
    w1 = g_w / (1.0 + ratio)
    w2 = g_w * ratio / (1.0 + ratio)
    e1 = (i1 - n_groups).astype(F32)
    e2 = (i2 - n_groups).astype(F32)
    r_ref[...] = jnp.where(col == 0, e1, jnp.where(col == 1, e2, jnp.where(
        col == 2, w1, jnp.where(col == 3, w2, 0.0))))


def _router(x, mods, seq_len, norm_g, wr, br, n_groups, epg):
    t, d = x.shape
    tm = _row_tile(seq_len, 512)
    return pl.pallas_call(
        functools.partial(_router_kernel, n_groups=n_groups, epg=epg),
        grid=(t // tm,),
        in_specs=[_row_spec(tm, d), _full_spec((1, d)),
                  _mod_spec(mods.shape[0], tm, seq_len, d, 4),
                  _mod_spec(mods.shape[0], tm, seq_len, d, 3),
                  _full_spec(wr.shape), _full_spec(br.shape)],
        out_specs=[_row_spec(tm, d), _row_spec(tm, V7X_LANES)],
        out_shape=[jax.ShapeDtypeStruct((t, d), F32), jax.ShapeDtypeStruct((t, V7X_LANES), F32)],
        compiler_params=_cparams(1),
        name="moe_router",
    )(x, norm_g.reshape(1, d), mods, mods, wr, br)


def _dispatch_lists(route, n_block, n_exp, cap):
    t = route.shape[0]
    n_sb = t // n_block
    ids = route[:, 0:TOP_K].astype(I32).reshape(n_sb, n_block * TOP_K)
    wts = route[:, TOP_K:2 * TOP_K].reshape(n_sb, n_block * TOP_K)
    tok = jnp.broadcast_to(jnp.arange(n_block, dtype=I32)[:, None], (n_block, TOP_K)).reshape(-1)
    order = jnp.argsort(ids, axis=1, stable=True)
    counts = jnp.sum((ids[:, :, None] == jnp.arange(n_exp, dtype=I32)).astype(I32), axis=1)
    padded = (counts + MOE_BATCH - 1) // MOE_BATCH * MOE_BATCH
    seg_end = jnp.cumsum(padded, axis=1)
    seg_off = seg_end - padded
    src_off = jnp.cumsum(counts, axis=1) - counts
    q = jnp.arange(cap, dtype=I32)[None, :]
    e_q = jnp.minimum(jnp.sum((q[:, :, None] >= seg_end[:, None, :]).astype(I32), axis=2), n_exp - 1)
    pick = lambda a: jnp.take_along_axis(a, e_q, axis=1)
    rank = jnp.minimum(q - pick(seg_off), pick(counts) - 1)
    src = jnp.clip(pick(src_off) + rank, 0, n_block * TOP_K - 1)
    live = q < seg_end[:, -1:]
    entry = jnp.take_along_axis(order, src, axis=1)
    tok_list = jnp.where(live, tok[entry], 0)
    w_list = jnp.where(live, jnp.take_along_axis(wts, entry, axis=1), 0.0)
    return (padded.reshape(-1), seg_off.reshape(-1),
            tok_list.reshape(n_sb, 1, cap), w_list.reshape(n_sb, 1, cap))


def _moe_kernel(cnt_ref, off_ref, tok_ref, wgt_ref, h_ref, w13_ref, w2_ref, o_ref, lhs_scr, ys_scr,
                *, n_exp):
    sb = pl.program_id(0)
    e = pl.program_id(1)
    tm = lhs_scr.shape[0]
    de = w2_ref.shape[1]

    @pl.when(e == 0)
    def _():
        o_ref[...] = jnp.zeros_like(o_ref)

    count = cnt_ref[sb * n_exp + e]
    seg = off_ref[sb * n_exp + e]

    def tile(j, carry):
        base = seg + j * tm

        def gather(r, c):
            lhs_scr[pl.ds(r, 1), :] = h_ref[pl.ds(tok_ref[0, 0, base + r], 1), :]
            return c
        lax.fori_loop(0, tm, gather, 0, unroll=MOE_BATCH)

        hid = jnp.dot(lhs_scr[...].astype(BF16), w13_ref[0], preferred_element_type=F32)
        act = _silu(hid[:, :de]) * hid[:, de:]
        ys_scr[...] = jnp.dot(act.astype(BF16), w2_ref[0], preferred_element_type=F32)

        def combine(b, c):
            r0 = b * MOE_BATCH
            toks = [tok_ref[0, 0, base + r0 + i] for i in range(MOE_BATCH)]
            wts = [wgt_ref[0, 0, base + r0 + i] for i in range(MOE_BATCH)]
            new = [o_ref[pl.ds(toks[i], 1), :] + wts[i] * ys_scr[pl.ds(r0 + i, 1), :]
                   for i in range(MOE_BATCH)]
            for i in range(MOE_BATCH):
                o_ref[pl.ds(toks[i], 1), :] = new[i]
            return c
        rows = jnp.minimum(count - j * tm, tm)
        lax.fori_loop(0, lax.div(rows, MOE_BATCH), combine, 0)
        return carry

    lax.fori_loop(0, lax.div(count + tm - 1, tm), tile, 0)


def _moe_experts(h, route, w13, w2, n_block):
    t, d = h.shape
    n_exp, _, de2 = w13.shape
    de = de2 // 2
    n_sb = t // n_block
    tm = min(MOE_TILE, n_block)
    cap = n_block * TOP_K + n_exp * MOE_BATCH + tm
    cap = (cap + V7X_LANES - 1) // V7X_LANES * V7X_LANES
    cnt, off, tok_list, w_list = _dispatch_lists(route, n_block, n_exp, cap)
    grid_spec = pltpu.PrefetchScalarGridSpec(
        num_scalar_prefetch=2,
        grid=(n_sb, n_exp),
        in_specs=[pl.BlockSpec((1, 1, cap), lambda s, e, *_: (s, 0, 0), memory_space=pltpu.SMEM),
                  pl.BlockSpec((1, 1, cap), lambda s, e, *_: (s, 0, 0), memory_space=pltpu.SMEM),
                  pl.BlockSpec((n_block, d), lambda s, e, *_: (s, 0)),
                  pl.BlockSpec((1, d, de2), lambda s, e, *_: (e, 0, 0)),
                  pl.BlockSpec((1, de, d), lambda s, e, *_: (e, 0, 0))],
        out_specs=pl.BlockSpec((n_block, d), lambda s, e, *_: (s, 0)),
        scratch_shapes=[pltpu.VMEM((tm, d), F32), pltpu.VMEM((tm, d), F32)])
    return pl.pallas_call(
        functools.partial(_moe_kernel, n_exp=n_exp),
        grid_spec=grid_spec,
        out_shape=jax.ShapeDtypeStruct((t, d), F32),
        compiler_params=_cparams(2),
        name="moe_experts",
    )(cnt, off, tok_list, w_list, h, w13, w2)


def _residual_kernel(x_ref, y_ref, g2_ref, fg_ref, o_ref, *, final_norm):
    x = x_ref[...] + g2_ref[0] * y_ref[...]
    if final_norm:
        x = x * lax.rsqrt(jnp.mean(x * x, axis=-1, keepdims=True) + RMS_EPS) * fg_ref[...]
    o_ref[...] = x


def _residual(x, y, mods, seq_len, final_g, final_norm):
    t, d = x.shape
    tm = _row_tile(seq_len, 1024)
    return pl.pallas_call(
        functools.partial(_residual_kernel, final_norm=final_norm),
        grid=(t // tm,),
        in_specs=[_row_spec(tm, d), _row_spec(tm, d),
                  _mod_spec(mods.shape[0], tm, seq_len, d, 5), _full_spec((1, d))],
        out_specs=_row_spec(tm, d),
        out_shape=jax.ShapeDtypeStruct((t, d), F32),
        compiler_params=_cparams(1),
        name="moe_residual",
    )(x, y, mods, final_g.reshape(1, d))


def _moe_block(t):
    n = min(t, 2048)
    while t % n:
        n //= 2
    return n


def _moe(x, mods, seq_len, norm_g, wr, br, w13, w2, n_groups, final_g, final_norm):
    n_exp = w13.shape[0]
    h, route = _router(x, mods, seq_len, norm_g, wr, br, n_groups, n_exp // n_groups)
    y = _moe_experts(h, route, w13, w2, _moe_block(x.shape[0]))
    return _residual(x, y, mods, seq_len, final_g, final_norm)


def kernel(x, c, ctx, c_ctx, ada_w, ada_b, norm1_g, norm2_g, conf_w_in, conf_dw, conf_dw_b, conf_ln_g, conf_ln_b, conf_w_out, sc_w_in, sc_conv, sc_w_out, s5_a_re, s5_a_im, s5_log_dt, s5_b_re, s5_b_im, s5_c_re, s5_c_im, s5_d, s5_w_glu, moe_wg, moe_bg, moe_we, moe_be, moe_w13, moe_w2, final_g):
    b, s, d = x.shape
    lc = ctx.shape[1]
    depth = ada_w.shape[0]
    n_groups = moe_wg.shape[-1]
    n_exp = moe_we.shape[-1]
    assert s % GRID_W == 0 and n_groups + n_exp <= V7X_LANES

    rows = (b + 1 + V7X_SUBLANES - 1) // V7X_SUBLANES * V7X_SUBLANES
    cin = jnp.zeros((rows, d), F32).at[:b].set(c).at[b].set(c_ctx)
    table = _ada_table(cin, ada_w, ada_b)

    xs = x.reshape(b * s, d)
    cs = ctx.reshape(b * lc, d)
    for i in range(depth):
        kind, j = i % N_MIXERS, i // N_MIXERS
        update_ctx = i < depth - 1
        mods_x = table[i, :b].reshape(b, 1, 6 * d)
        mods_c = table[i, b].reshape(1, 1, 6 * d)
        if kind == 0:
            args = (norm1_g[i], conf_w_in[j].astype(BF16), conf_dw[j], conf_dw_b[j],
                    conf_ln_g[j], conf_ln_b[j], conf_w_out[j].astype(BF16))
            xs = _conformer(xs, mods_x, s, GRID_W, *args)
            if update_ctx:
                cs = _conformer(cs, mods_c, lc, 1, *args)
        elif kind == 1:
            args = (norm1_g[i], sc_w_in[j].astype(BF16), sc_conv[j], sc_w_out[j].astype(BF16))
            xs = _short_conv(xs, mods_x, s, GRID_W, *args)
            if update_ctx:
                cs = _short_conv(cs, mods_c, lc, lc, *args)
        else:
            xs, cs_new = _s5_mixer(xs, cs, mods_x, mods_c, b, norm1_g[i], s5_a_re[j], s5_a_im[j],
                                   s5_log_dt[j], s5_b_re[j], s5_b_im[j], s5_c_re[j], s5_c_im[j],
                                   s5_d[j], s5_w_glu[j].astype(BF16))
            if update_ctx:
                cs = cs_new

        wr = jnp.zeros((d, V7X_LANES), F32).at[:, :n_groups].set(moe_wg[i])
        wr = wr.at[:, n_groups:n_groups + n_exp].set(moe_we[i])
        br = jnp.zeros((1, V7X_LANES), F32).at[0, :n_groups].set(moe_bg[i])
        br = br.at[0, n_groups:n_groups + n_exp].set(moe_be[i])
        moe_args = (norm2_g[i], wr, br, moe_w13[i].astype(BF16), moe_w2[i].astype(BF16), n_groups,
                    final_g)
        xs = _moe(xs, mods_x, s, *moe_args, final_norm=(i == depth - 1))
        if update_ctx:
            cs = _moe(cs, mods_c, lc, *moe_args, final_norm=False)
    return xs.reshape(b, s, d)
```

```python
import functools

import jax
import jax.numpy as jnp
from jax import lax
from jax.experimental import pallas as pl
from jax.experimental.pallas import tpu as pltpu

F32 = jnp.float32
BF16 = jnp.bfloat16
I32 = jnp.int32
HIGHEST = lax.Precision.HIGHEST

GRID_W = 64
N_MIXERS = 3
TOP_K = 2
RMS_EPS = 1e-6
LN_EPS = 1e-5
S5_DT_FLOOR = -1e-4

V7X_VMEM_BYTES = 64 * 1024 * 1024
V7X_LANES = 128
V7X_SUBLANES = 8
VMEM_LIMIT_BYTES = V7X_VMEM_BYTES - 8 * 1024 * 1024

S5_CHUNK = 16
MOE_BATCH = V7X_SUBLANES
NEG_BIG = -1e30


def _cparams(n_axes):
    return pltpu.CompilerParams(dimension_semantics=("arbitrary",) * n_axes,
                                vmem_limit_bytes=VMEM_LIMIT_BYTES)


def _row_tile(seq_len, want):
    t = min(seq_len, want)
    while seq_len % t or t % V7X_SUBLANES:
        t -= 1
    return t


def _modnorm(x, g, sc, sh):
    y = x * lax.rsqrt(jnp.mean(x * x, axis=-1, keepdims=True) + RMS_EPS)
    return (y * g) * (1.0 + sc) + sh


def _silu(v):
    return v * jax.nn.sigmoid(v)


def _mod_spec(n_mod, tm, seq_len, d, j):
    if n_mod == 1:
        return pl.BlockSpec((1, 1, d), lambda t: (0, 0, j))
    return pl.BlockSpec((1, 1, d), lambda t: ((t * tm) // seq_len, 0, j))


def _row_spec(tm, d):
    return pl.BlockSpec((tm, d), lambda t: (t, 0))


def _full_spec(shape):
    nd = len(shape)
    return pl.BlockSpec(shape, lambda *_: (0,) * nd)


def _ada_kernel(c_ref, w_ref, b_ref, o_ref):
    o_ref[0] = jnp.dot(_silu(c_ref[...]), w_ref[0], preferred_element_type=F32,
                       precision=HIGHEST) + b_ref[0]


def _ada_table(cin, ada_w, ada_b):
    depth, d, d6 = ada_w.shape
    r = cin.shape[0]
    tn = d6 // 6
    return pl.pallas_call(
        _ada_kernel,
        grid=(depth, d6 // tn),
        in_specs=[pl.BlockSpec((r, d), lambda i, j: (0, 0)),
                  pl.BlockSpec((1, d, tn), lambda i, j: (i, 0, j)),
                  pl.BlockSpec((1, 1, tn), lambda i, j: (i, 0, j))],
        out_specs=pl.BlockSpec((1, r, tn), lambda i, j: (i, 0, j)),
        out_shape=jax.ShapeDtypeStruct((depth, r, d6), F32),
        compiler_params=_cparams(2),
        name="ada_table",
    )(cin, ada_w, ada_b.reshape(depth, 1, d6))


def _conf_in_kernel(x_ref, g_ref, sc_ref, sh_ref, w_ref, z_ref):
    h = _modnorm(x_ref[...], g_ref[...], sc_ref[0], sh_ref[0])
    y = jnp.dot(h.astype(BF16), w_ref[...], preferred_element_type=F32)
    ci = y.shape[1] // 2
    z_ref[...] = (y[:, :ci] * jax.nn.sigmoid(y[:, ci:])).astype(z_ref.dtype)


def _conf_in(x, mods, seq_len, norm_g, w_in):
    t, d = x.shape
    tm = _row_tile(seq_len, 512)
    ci = w_in.shape[1] // 2
    return pl.pallas_call(
        _conf_in_kernel,
        grid=(t // tm,),
        in_specs=[_row_spec(tm, d), _full_spec((1, d)),
                  _mod_spec(mods.shape[0], tm, seq_len, d, 1),
                  _mod_spec(mods.shape[0], tm, seq_len, d, 0),
                  _full_spec(w_in.shape)],
        out_specs=_row_spec(tm, ci),
        out_shape=jax.ShapeDtypeStruct((t, ci), BF16),
        compiler_params=_cparams(1),
        name="conf_in",
    )(x, norm_g.reshape(1, d), mods, mods, w_in)


def _dwconv_kernel(z_ref, w_ref, b_ref, o_ref, src_scr, *, stride, chunk):
    seq_len, cb = z_ref.shape
    taps = w_ref.shape[0]
    half = taps // 2
    aligned = stride % chunk == 0
    pad = 0 if aligned else half * stride
    if pad:
        src_scr[pl.ds(0, pad), :] = jnp.zeros((pad, cb), F32)
        src_scr[pl.ds(pad + seq_len, pad), :] = jnp.zeros((pad, cb), F32)
    src_scr[pl.ds(pad, seq_len), :] = z_ref[...].astype(F32)
    w = w_ref[...]
    bias = b_ref[...]
    for r0 in range(0, seq_len, chunk):
        acc = jnp.broadcast_to(bias, (chunk, cb))
        for k in range(taps):
            lo = r0 + (k - half) * stride
            if aligned and (lo < 0 or lo + chunk > seq_len):
                continue
            acc = acc + w[k:k + 1, :] * src_scr[pl.ds(lo + pad, chunk), :]
        o_ref[pl.ds(r0, chunk), :] = acc.astype(o_ref.dtype)


def _dwconv(z, seq_len, stride, w, b):
    t, c = z.shape
    taps = w.shape[0]
    cb = min(c, 2 * V7X_LANES)
    chunk = _row_tile(seq_len, 64)
    pad = 0 if stride % chunk == 0 else (taps // 2) * stride
    return pl.pallas_call(
        functools.partial(_dwconv_kernel, stride=stride, chunk=chunk),
        grid=(t // seq_len, c // cb),
        in_specs=[pl.BlockSpec((seq_len, cb), lambda s, j: (s, j)),
                  pl.BlockSpec((taps, cb), lambda s, j: (0, j)),
                  pl.BlockSpec((1, cb), lambda s, j: (0, j))],
        out_specs=pl.BlockSpec((seq_len, cb), lambda s, j: (s, j)),
        out_shape=jax.ShapeDtypeStruct((t, c), BF16),
        scratch_shapes=[pltpu.VMEM((seq_len + 2 * pad, cb), F32)],
        compiler_params=_cparams(2),
        name="dwconv",
    )(z, w, b.reshape(1, c))


def _conf_out_kernel(z_ref, lg_ref, lb_ref, w_ref, x_ref, g1_ref, o_ref):
    z = z_ref[...].astype(F32)
    mu = jnp.mean(z, axis=-1, keepdims=True)
    zc = z - mu
    var = jnp.mean(zc * zc, axis=-1, keepdims=True)
    y = zc * lax.rsqrt(var + LN_EPS) * lg_ref[...] + lb_ref[...]
    m = jnp.dot(_silu(y).astype(BF16), w_ref[...], preferred_element_type=F32)
    o_ref[...] = x_ref[...] + g1_ref[0] * m


def _conf_out(z, x, mods, seq_len, ln_g, ln_b, w_out):
    t, d = x.shape
    ci = z.shape[1]
    tm = _row_tile(seq_len, 512)
    return pl.pallas_call(
        _conf_out_kernel,
        grid=(t // tm,),
        in_specs=[_row_spec(tm, ci), _full_spec((1, ci)), _full_spec((1, ci)),
                  _full_spec(w_out.shape), _row_spec(tm, d),
                  _mod_spec(mods.shape[0], tm, seq_len, d, 2)],
        out_specs=_row_spec(tm, d),
        out_shape=jax.ShapeDtypeStruct((t, d), F32),
        compiler_params=_cparams(1),
        name="conf_out",
    )(z, ln_g.reshape(1, ci), ln_b.reshape(1, ci), w_out, x, mods)


def _conformer(x, mods, seq_len, stride, norm_g, w_in, dw, dw_b, ln_g, ln_b, w_out):
    z = _conf_in(x, mods, seq_len, norm_g, w_in)
    z = _dwconv(z, seq_len, stride, dw, dw_b)
    return _conf_out(z, x, mods, seq_len, ln_g, ln_b, w_out)


def _sc_kernel(x_ref, g_ref, sc_ref, sh_ref, win_ref, cw_ref, wout_ref, g1_ref, o_ref, *, period):
    x = x_ref[...]
    tm, d = x.shape
    h = _modnorm(x, g_ref[...], sc_ref[0], sh_ref[0])
    y = jnp.dot(h.astype(BF16), win_ref[...], preferred_element_type=F32)
    gb, gc, v = y[:, :d], y[:, d:2 * d], y[:, 2 * d:]
    u = gc * v
    pos = lax.broadcasted_iota(I32, (tm, 1), 0) % period
    u_prev = jnp.where(pos == 0, 0.0, pltpu.roll(u, 1, 0))
    u_next = jnp.where(pos == period - 1, 0.0, pltpu.roll(u, tm - 1, 0))
    cw = cw_ref[...]
    conv = cw[0:1, :] * u_prev + cw[1:2, :] * u + cw[2:3, :] * u_next
    m = jnp.dot((gb * conv).astype(BF16), wout_ref[...], preferred_element_type=F32)
    o_ref[...] = x + g1_ref[0] * m


def _short_conv(x, mods, seq_len, period, norm_g, w_in, conv_w, w_out):
    t, d = x.shape
    tm = _row_tile(seq_len, 512)
    assert tm % period == 0 and conv_w.shape[0] == 3
    return pl.pallas_call(
        functools.partial(_sc_kernel, period=period),
        grid=(t // tm,),
        in_specs=[_row_spec(tm, d), _full_spec((1, d)),
                  _mod_spec(mods.shape[0], tm, seq_len, d, 1),
                  _mod_spec(mods.shape[0], tm, seq_len, d, 0),
                  _full_spec(w_in.shape), _full_spec(conv_w.shape), _full_spec(w_out.shape),
                  _mod_spec(mods.shape[0], tm, seq_len, d, 2)],
        out_specs=_row_spec(tm, d),
        out_shape=jax.ShapeDtypeStruct((t, d), F32),
        compiler_params=_cparams(1),
        name="short_conv",
    )(x, norm_g.reshape(1, d), mods, mods, w_in, conv_w, w_out, mods)


def _s5_prep_kernel(are_ref, aim_ref, ldt_ref, bre_ref, bim_ref, cre_ref, cim_ref,
                    a16r_ref, a16i_ref, winr_ref, wini_ref, woutr_ref, wouti_ref, k_ref, *, reverse):
    a_re = jnp.minimum(are_ref[0], S5_DT_FLOOR)
    a_im = aim_ref[0]
    dt = jnp.exp(ldt_ref[0])
    b_re, b_im = bre_ref[0], bim_ref[0]
    c_re, c_im = cre_ref[0], cim_ref[0]
    cg = b_re.shape[1]
    tc = S5_CHUNK

    def power(n):
        mag = jnp.exp((n * dt) * a_re)
        ang = (n * dt) * a_im
        return mag * jnp.cos(ang), mag * jnp.sin(ang)

    abar_re, abar_im = power(1)
    den = a_re * a_re + a_im * a_im
    n_re = abar_re - 1.0
    n_im = abar_im
    k_re = (n_re * a_re + n_im * a_im) / den
    k_im = (n_im * a_re - n_re * a_im) / den
    bb_re = k_re * b_re - k_im * b_im
    bb_im = k_re * b_im + k_im * b_re

    e16r, e16i = power(tc)
    a16r_ref[0] = e16r
    a16i_ref[0] = e16i
    for t in range(tc):
        er, ei = power(t if reverse else tc - 1 - t)
        winr_ref[0, :, t * cg:(t + 1) * cg, :] = er * bb_re - ei * bb_im
        wini_ref[0, :, t * cg:(t + 1) * cg, :] = er * bb_im + ei * bb_re
        er, ei = power(tc - t if reverse else t + 1)
        woutr_ref[0, :, t * cg:(t + 1) * cg, :] = c_re * er - c_im * ei
        wouti_ref[0, :, t * cg:(t + 1) * cg, :] = -(c_re * ei + c_im * er)
        er, ei = power(t)
        m_re = c_re * er - c_im * ei
        m_im = c_re * ei + c_im * er
        k_ref[0, :, t * cg:(t + 1) * cg, :] = (
            jnp.einsum('gap,gbp->gab', m_re, bb_re, preferred_element_type=F32, precision=HIGHEST)
            - jnp.einsum('gap,gbp->gab', m_im, bb_im, preferred_element_type=F32, precision=HIGHEST))


def _s5_prep(a_re, a_im, log_dt, b_re, b_im, c_re, c_im, reverse):
    g, p = a_re.shape
    cg = b_re.shape[2]
    gb = min(g, 8)
    rows = S5_CHUNK * cg
    a4 = lambda a: a.reshape(1, g, 1, p)
    bt = lambda b: jnp.swapaxes(b, 1, 2).reshape(1, g, cg, p)
    spec_a = pl.BlockSpec((1, gb, 1, p), lambda i: (0, i, 0, 0))
    spec_b = pl.BlockSpec((1, gb, cg, p), lambda i: (0, i, 0, 0))
    spec_w = pl.BlockSpec((1, gb, rows, p), lambda i: (0, i, 0, 0))
    outs = pl.pallas_call(
        functools.partial(_s5_prep_kernel, reverse=reverse),
        grid=(g // gb,),
        in_specs=[spec_a, spec_a, pl.BlockSpec((1, gb, 1, 1), lambda i: (0, i, 0, 0)),
                  spec_b, spec_b, spec_b, spec_b],
        out_specs=[spec_a, spec_a, spec_w, spec_w, spec_w, spec_w,
                   pl.BlockSpec((1, gb, rows, cg), lambda i: (0, i, 0, 0))],
        out_shape=[jax.ShapeDtypeStruct((1, g, 1, p), F32)] * 2
        + [jax.ShapeDtypeStruct((1, g, rows, p), F32)] * 4
        + [jax.ShapeDtypeStruct((1, g, rows, cg), F32)],
        compiler_params=_cparams(1),
        name="s5_prep",
    )(a4(a_re), a4(a_im), log_dt.reshape(1, g, 1, 1), bt(b_re), bt(b_im),
      c_re.reshape(1, g, cg, p), c_im.reshape(1, g, cg, p))
    a16r, a16i, winr, wini, woutr, wouti, kk = [o[0] for o in outs]
    return a16r, a16i, winr, wini, woutr, wouti, kk


def _pair_blockdiag(w):
    g, r, p = w.shape
    w = w.reshape(g // 2, 2, r, p)
    z = jnp.zeros_like(w[:, 0])
    top = jnp.concatenate([w[:, 0], z], axis=-1)
    bot = jnp.concatenate([z, w[:, 1]], axis=-1)
    return jnp.concatenate([top, bot], axis=1)


def _toeplitz(kk, cg, reverse):
    g = kk.shape[0]
    tc = S5_CHUNK
    k5 = kk.reshape(g, tc, cg, cg)
    t_in = jnp.arange(tc)[None, :]
    t_out = jnp.arange(tc)[:, None]
    lag = (t_in - t_out) if reverse else (t_out - t_in)
    blocks = jnp.where((lag >= 0)[None, :, :, None, None], k5[:, jnp.clip(lag, 0, tc - 1)], 0.0)
    return blocks.transpose(0, 1, 3, 2, 4).reshape(g, tc * cg, tc * cg)


def _s5_core_kernel(vc_ref, vx_ref, a16r_ref, a16i_ref, winr_ref, wini_ref, woutr_ref, wouti_ref,
                    tz_ref, yc_ref, yx_ref, sre, sim, hre, him, *, n_seq):
    r = vc_ref.shape[1]
    n_cc = vc_ref.shape[2]
    n_col = n_cc + vx_ref.shape[2]
    n_chunks = n_col // n_seq
    n_ctx_chunks = n_cc // n_seq
    v = jnp.concatenate([jnp.concatenate([vc_ref[half], vx_ref[half]], axis=-1)
                         for half in range(2)], axis=0)
    for direction in range(2):
        sre[...] = lax.dot_general(v, winr_ref[direction, 0], (((0,), (0,)), ((), ())),
                                   preferred_element_type=F32)
        sim[...] = lax.dot_general(v, wini_ref[direction, 0], (((0,), (0,)), ((), ())),
                                   preferred_element_type=F32)
        ar = a16r_ref[direction, 0]
        ai = a16i_ref[direction, 0]

        def step(k, carry):
            h_r, h_i = carry
            rows = pl.ds(pl.multiple_of(k * n_seq, n_seq), n_seq)
            hre[rows, :] = h_r
            him[rows, :] = h_i
            return (ar * h_r - ai * h_i + sre[rows, :], ar * h_i + ai * h_r + sim[rows, :])

        zero = jnp.zeros((n_seq, sre.shape[1]), F32)
        if direction == 0:
            lax.fori_loop(0, n_chunks, step, (zero, zero))
        else:
            mid = lax.fori_loop(0, n_ctx_chunks,
                                lambda i, c: step(n_ctx_chunks - 1 - i, c), (zero, zero))
            lax.fori_loop(0, n_chunks - n_ctx_chunks,
                          lambda i, c: step(n_chunks - 1 - i, c), mid)

        h_r = hre[...].astype(BF16)
        h_i = him[...].astype(BF16)
        for half in range(2):
            rows = pl.ds(half * r, r)
            part = (lax.dot_general(woutr_ref[direction, 0, rows, :], h_r,
                                    (((1,), (1,)), ((), ())), preferred_element_type=F32)
                    + lax.dot_general(wouti_ref[direction, 0, rows, :], h_i,
                                      (((1,), (1,)), ((), ())), preferred_element_type=F32)
                    + jnp.dot(tz_ref[direction, half], v[half * r:(half + 1) * r, :],
                              preferred_element_type=F32))
            if direction == 0:
                yc_ref[half] = part[:, :n_cc]
                yx_ref[half] = part[:, n_cc:]
            else:
                yc_ref[half] = yc_ref[half] + part[:, :n_cc]
                yx_ref[half] = yx_ref[half] + part[:, n_cc:]


def _s5_core(vc, vx, ops_fwd, ops_bwd, n_seq):
    g, r, n_cc = vc.shape
    n_cx = vx.shape[2]
    cg = r // S5_CHUNK
    stacked = []
    for idx in range(6):
        stacked.append(jnp.stack([ops_fwd[idx], ops_bwd[idx]]))
    a16r, a16i, winr, wini, woutr, wouti = stacked
    p = a16r.shape[-1]
    pair_vec = lambda a: a.reshape(2, g // 2, 1, 2 * p)
    pair_mat = lambda w: jnp.stack([_pair_blockdiag(w[0]), _pair_blockdiag(w[1])]).astype(BF16)
    tz = jnp.stack([_toeplitz(ops_fwd[6], cg, False), _toeplitz(ops_bwd[6], cg, True)]).astype(BF16)
    vec_spec = pl.BlockSpec((2, 1, 1, 2 * p), lambda i: (0, i, 0, 0))
    mat_spec = pl.BlockSpec((2, 1, 2 * r, 2 * p), lambda i: (0, i, 0, 0))
    col_spec = lambda n: pl.BlockSpec((2, r, n), lambda i: (i, 0, 0))
    return pl.pallas_call(
        functools.partial(_s5_core_kernel, n_seq=n_seq),
        grid=(g // 2,),
        in_specs=[col_spec(n_cc), col_spec(n_cx),
                  vec_spec, vec_spec, mat_spec, mat_spec, mat_spec, mat_spec,
                  pl.BlockSpec((2, 2, r, r), lambda i: (0, i, 0, 0))],
        out_specs=[col_spec(n_cc), col_spec(n_cx)],
        out_shape=[jax.ShapeDtypeStruct((g, r, n_cc), F32), jax.ShapeDtypeStruct((g, r, n_cx), F32)],
        scratch_shapes=[pltpu.VMEM((n_cc + n_cx, 2 * p), F32)] * 4,
        compiler_params=_cparams(1),
        name="s5_core",
    )(vc, vx, pair_vec(a16r), pair_vec(a16i), pair_mat(winr), pair_mat(wini),
      pair_mat(woutr), pair_mat(wouti), tz)


def _modnorm_kernel(x_ref, g_ref, sc_ref, sh_ref, o_ref):
    o_ref[...] = _modnorm(x_ref[...], g_ref[...], sc_ref[0], sh_ref[0]).astype(o_ref.dtype)


def _modnorm_call(x, mods, seq_len, norm_g):
    t, d = x.shape
    tm = _row_tile(seq_len, 1024)
    return pl.pallas_call(
        _modnorm_kernel,
        grid=(t // tm,),
        in_specs=[_row_spec(tm, d), _full_spec((1, d)),
                  _mod_spec(mods.shape[0], tm, seq_len, d, 1),
                  _mod_spec(mods.shape[0], tm, seq_len, d, 0)],
        out_specs=_row_spec(tm, d),
        out_shape=jax.ShapeDtypeStruct((t, d), BF16),
        compiler_params=_cparams(1),
        name="modnorm",
    )(x, norm_g.reshape(1, d), mods, mods)


def _s5_head_kernel(x_ref, y_ref, g_ref, sc_ref, sh_ref, d_ref, w_ref, g1_ref, o_ref):
    x = x_ref[...]
    h = _modnorm(x, g_ref[...], sc_ref[0], sh_ref[0])
    a = jax.nn.gelu(y_ref[...] + d_ref[...] * h, approximate=True)
    z = jnp.dot(a.astype(BF16), w_ref[...], preferred_element_type=F32)
    dm = z.shape[1] // 2
    o_ref[...] = x + g1_ref[0] * (z[:, :dm] * jax.nn.sigmoid(z[:, dm:]))


def _s5_head(x, y, mods, seq_len, norm_g, d_skip, w_glu):
    t, d = x.shape
    tm = _row_tile(seq_len, 512)
    return pl.pallas_call(
        _s5_head_kernel,
        grid=(t // tm,),
        in_specs=[_row_spec(tm, d), _row_spec(tm, d), _full_spec((1, d)),
                  _mod_spec(mods.shape[0], tm, seq_len, d, 1),
                  _mod_spec(mods.shape[0], tm, seq_len, d, 0),
                  _full_spec((1, d)), _full_spec(w_glu.shape),
                  _mod_spec(mods.shape[0], tm, seq_len, d, 2)],
        out_specs=_row_spec(tm, d),
        out_shape=jax.ShapeDtypeStruct((t, d), F32),
        compiler_params=_cparams(1),
        name="s5_head",
    )(x, y, norm_g.reshape(1, d), mods, mods, d_skip.reshape(1, d), w_glu, mods)


def _s5_mixer(x, ctx, mods_x, mods_c, n_seq, norm_g, a_re, a_im, log_dt, b_re, b_im, c_re, c_im,
              d_skip, w_glu):
    d = x.shape[1]
    s_len, c_len = x.shape[0] // n_seq, ctx.shape[0] // n_seq
    g = a_re.shape[1]
    cg = d // g
    tc = S5_CHUNK
    ops = [_s5_prep(a_re[k], a_im[k], log_dt[k], b_re[k], b_im[k], c_re[k], c_im[k], bool(k))
           for k in range(2)]

    def to_cols(h, length):
        u = h.reshape(n_seq, length // tc, tc, g, cg)
        return u.transpose(3, 2, 4, 1, 0).reshape(g, tc * cg, (length // tc) * n_seq)

    def to_rows(y, length):
        y = y.reshape(g, tc, cg, length // tc, n_seq)
        return y.transpose(4, 3, 1, 0, 2).reshape(n_seq * length, d)

    vc = to_cols(_modnorm_call(ctx, mods_c, c_len, norm_g), c_len)
    vx = to_cols(_modnorm_call(x, mods_x, s_len, norm_g), s_len)
    y_c, y_x = _s5_core(vc, vx, ops[0], ops[1], n_seq)
    x = _s5_head(x, to_rows(y_x, s_len), mods_x, s_len, norm_g, d_skip, w_glu)
    ctx = _s5_head(ctx, to_rows(y_c, c_len), mods_c, c_len, norm_g, d_skip, w_glu)
    return x, ctx


def _router_kernel(x_ref, g_ref, sc_ref, sh_ref, wr_ref, br_ref, h_ref, r_ref, *, n_groups, epg):
    h = _modnorm(x_ref[...], g_ref[...], sc_ref[0], sh_ref[0])
    for c in range(h_ref.shape[1]):
        h_ref[:, c, :] = h[:, c * V7X_LANES:(c + 1) * V7X_LANES]
    logits = jnp.dot(h, wr_ref[...], preferred_element_type=F32, precision=HIGHEST) + br_ref[...]
    col = lax.broadcasted_iota(I32, logits.shape, 1)
    far = jnp.int32(1 << 20)

    def first_max(vals):
        m = jnp.max(vals, axis=-1, keepdims=True)
        return m, jnp.min(jnp.where(vals == m, col, far), axis=-1, keepdims=True)

    is_group = col < n_groups
    gl = jnp.where(is_group, logits, NEG_BIG)
    gmax, gidx = first_max(gl)
    gsum = jnp.sum(jnp.where(is_group, jnp.exp(gl - gmax), 0.0), axis=-1, keepdims=True)
    g_w = 1.0 / gsum
    lo = n_groups + gidx * epg
    le = jnp.where((col >= lo) & (col < lo + epg), logits, NEG_BIG)
    m1, i1 = first_max(le)
    m2, i2 = first_max(jnp.where(col == i1, NEG_BIG, le))
    ratio = jnp.exp(m2 - m1)
    w1 = g_w / (1.0 + ratio)
    w2 = g_w * ratio / (1.0 + ratio)
    e1 = (i1 - n_groups).astype(F32)
    e2 = (i2 - n_groups).astype(F32)
    r_ref[...] = jnp.where(col == 0, e1, jnp.where(col == 1, e2, jnp.where(
        col == 2, w1, jnp.where(col == 3, w2, 0.0))))


def _router(x, mods, seq_len, norm_g, wr, br, n_groups, epg):
    t, d = x.shape
    tm = _row_tile(seq_len, 512)
    p = d // V7X_LANES
    return pl.pallas_call(
        functools.partial(_router_kernel, n_groups=n_groups, epg=epg),
        grid=(t // tm,),
        in_specs=[_row_spec(tm, d), _full_spec((1, d)),
                  _mod_spec(mods.shape[0], tm, seq_len, d, 4),
                  _mod_spec(mods.shape[0], tm, seq_len, d, 3),
                  _full_spec(wr.shape), _full_spec(br.shape)],
        out_specs=[pl.BlockSpec((tm, p, V7X_LANES), lambda i: (i, 0, 0)), _row_spec(tm, V7X_LANES)],
        out_shape=[jax.ShapeDtypeStruct((t, p, V7X_LANES), F32),
                   jax.ShapeDtypeStruct((t, V7X_LANES), F32)],
        compiler_params=_cparams(1),
        name="moe_router",
    )(x, norm_g.reshape(1, d), mods, mods, wr, br)


def _dispatch_lists(route, n_block, n_exp, cap):
    t = route.shape[0]
    n_sb = t // n_block
    n_assign = n_block * TOP_K
    ids = route[:, 0:TOP_K].astype(I32).reshape(n_sb, n_assign)
    wts = route[:, TOP_K:2 * TOP_K].reshape(n_sb, n_assign)
    tok = jnp.broadcast_to(jnp.arange(n_block, dtype=I32)[None, :, None],
                           (n_sb, n_block, TOP_K)).reshape(n_sb, n_assign)
    _, tok_s, w_s = lax.sort((ids, tok, wts), dimension=1, is_stable=True, num_keys=1)
    counts = jnp.sum((ids[:, :, None] == jnp.arange(n_exp, dtype=I32)).astype(I32), axis=1)
    offs = jnp.cumsum(counts, axis=1) - counts
    pad = ((0, 0), (0, cap - n_assign))
    return (counts.reshape(-1), offs.reshape(-1),
            jnp.pad(tok_s, pad).reshape(n_sb, 1, cap), jnp.pad(w_s, pad).reshape(n_sb, 1, cap))


def _moe_kernel(cnt_ref, off_ref, tok_ref, wgt_ref, h_ref, w13_ref, w2_ref, o_ref, lhs_scr, ys_scr,
                *, n_exp, tm):
    sb = pl.program_id(0)
    e = pl.program_id(1)
    p = h_ref.shape[1]
    de = w2_ref.shape[1]
    stride = tm + 1

    @pl.when(e == 0)
    def _():
        o_ref[...] = jnp.zeros_like(o_ref)

    count = cnt_ref[sb * n_exp + e]
    seg = off_ref[sb * n_exp + e]

    def tile(j, carry):
        base = seg + j * tm
        for mi in range(tm):
            lhs_scr[pl.ds(mi, p, stride=stride), :] = h_ref[tok_ref[0, 0, base + mi]]
        lhs = jnp.concatenate([lhs_scr[pl.ds(c * stride, tm), :] for c in range(p)], axis=-1)
        hid = jnp.dot(lhs.astype(BF16), w13_ref[0], preferred_element_type=F32)
        act = _silu(hid[:, :de]) * hid[:, de:]
        ys = jnp.dot(act.astype(BF16), w2_ref[0], preferred_element_type=F32)
        for c in range(p):
            ys_scr[pl.ds(c * stride, tm), :] = ys[:, c * V7X_LANES:(c + 1) * V7X_LANES]

        rows = jnp.minimum(count - j * tm, tm)

        def updated(r):
            tok = tok_ref[0, 0, base + r]
            return tok, o_ref[tok] + wgt_ref[0, 0, base + r] * ys_scr[pl.ds(r, p, stride=stride), :]

        for b0 in range(0, tm, MOE_BATCH):
            @pl.when(rows >= b0 + MOE_BATCH)
            def _():
                new = [updated(b0 + i) for i in range(MOE_BATCH)]
                for tok, slab in new:
                    o_ref[tok] = slab

            @pl.when((rows > b0) & (rows < b0 + MOE_BATCH))
            def _():
                for i in range(MOE_BATCH - 1):
                    @pl.when(b0 + i < rows)
                    def _():
                        tok, slab = updated(b0 + i)
                        o_ref[tok] = slab
        return carry

    lax.fori_loop(0, lax.div(count + tm - 1, tm), tile, 0)


def _moe_tile(n_block, n_exp):
    want = (n_block * TOP_K * 5) // (n_exp * 4)
    return max(2 * V7X_SUBLANES, -(-want // (2 * V7X_SUBLANES)) * 2 * V7X_SUBLANES)


def _moe_experts(h, route, w13, w2, n_block):
    t, p, _ = h.shape
    n_exp, d, de2 = w13.shape
    de = de2 // 2
    n_sb = t // n_block
    tm = _moe_tile(n_block, n_exp)
    cap = -(-(n_block * TOP_K + tm) // V7X_LANES) * V7X_LANES
    cnt, off, tok_list, w_list = _dispatch_lists(route, n_block, n_exp, cap)
    slab_spec = pl.BlockSpec((n_block, p, V7X_LANES), lambda s, e, *_: (s, 0, 0))
    list_spec = pl.BlockSpec((1, 1, cap), lambda s, e, *_: (s, 0, 0), memory_space=pltpu.SMEM)
    grid_spec = pltpu.PrefetchScalarGridSpec(
        num_scalar_prefetch=2,
        grid=(n_sb, n_exp),
        in_specs=[list_spec, list_spec, slab_spec,
                  pl.BlockSpec((1, d, de2), lambda s, e, *_: (e, 0, 0)),
                  pl.BlockSpec((1, de, d), lambda s, e, *_: (e, 0, 0))],
        out_specs=slab_spec,
        scratch_shapes=[pltpu.VMEM((p * (tm + 1), V7X_LANES), F32)] * 2)
    return pl.pallas_call(
        functools.partial(_moe_kernel, n_exp=n_exp, tm=tm),
        grid_spec=grid_spec,
        out_shape=jax.ShapeDtypeStruct((t, p, V7X_LANES), F32),
        compiler_params=_cparams(2),
        name="moe_experts",
    )(cnt, off, tok_list, w_list, h, w13, w2)


def _residual_kernel(x_ref, y_ref, g2_ref, fg_ref, o_ref, *, final_norm):
    y = jnp.concatenate([y_ref[:, c, :] for c in range(y_ref.shape[1])], axis=-1)
    x = x_ref[...] + g2_ref[0] * y
    if final_norm:
        x = x * lax.rsqrt(jnp.mean(x * x, axis=-1, keepdims=True) + RMS_EPS) * fg_ref[...]
    o_ref[...] = x


def _residual(x, y, mods, seq_len, final_g, final_norm):
    t, d = x.shape
    tm = _row_tile(seq_len, 1024)
    return pl.pallas_call(
        functools.partial(_residual_kernel, final_norm=final_norm),
        grid=(t // tm,),
        in_specs=[_row_spec(tm, d), pl.BlockSpec((tm,) + y.shape[1:], lambda i: (i, 0, 0)),
                  _mod_spec(mods.shape[0], tm, seq_len, d, 5), _full_spec((1, d))],
        out_specs=_row_spec(tm, d),
        out_shape=jax.ShapeDtypeStruct((t, d), F32),
        compiler_params=_cparams(1),
        name="moe_residual",
    )(x, y, mods, final_g.reshape(1, d))


def _moe_block(t):
    n = min(t, 2048)
    while t % n:
        n //= 2
    return n


def _moe(x, mods, seq_len, norm_g, wr, br, w13, w2, n_groups, final_g, final_norm):
    n_exp = w13.shape[0]
    h, route = _router(x, mods, seq_len, norm_g, wr, br, n_groups, n_exp // n_groups)
    y = _moe_experts(h, route, w13, w2, _moe_block(x.shape[0]))
    return _residual(x, y, mods, seq_len, final_g, final_norm)


def kernel(x, c, ctx, c_ctx, ada_w, ada_b, norm1_g, norm2_g, conf_w_in, conf_dw, conf_dw_b, conf_ln_g, conf_ln_b, conf_w_out, sc_w_in, sc_conv, sc_w_out, s5_a_re, s5_a_im, s5_log_dt, s5_b_re, s5_b_im, s5_c_re, s5_c_im, s5_d, s5_w_glu, moe_wg, moe_bg, moe_we, moe_be, moe_w13, moe_w2, final_g):
    b, s, d = x.shape
    lc = ctx.shape[1]
    depth = ada_w.shape[0]
    n_groups = moe_wg.shape[-1]
    n_exp = moe_we.shape[-1]
    assert s % GRID_W == 0 and n_groups + n_exp <= V7X_LANES

    rows = (b + 1 + V7X_SUBLANES - 1) // V7X_SUBLANES * V7X_SUBLANES
    cin = jnp.zeros((rows, d), F32).at[:b].set(c).at[b].set(c_ctx)
    table = _ada_table(cin, ada_w, ada_b)

    xs = x.reshape(b * s, d)
    cs = ctx.reshape(b * lc, d)
    for i in range(depth):
        kind, j = i % N_MIXERS, i // N_MIXERS
        update_ctx = i < depth - 1
        mods_x = table[i, :b].reshape(b, 1, 6 * d)
        mods_c = table[i, b].reshape(1, 1, 6 * d)
        if kind == 0:
            args = (norm1_g[i], conf_w_in[j].astype(BF16), conf_dw[j], conf_dw_b[j],
                    conf_ln_g[j], conf_ln_b[j], conf_w_out[j].astype(BF16))
            xs = _conformer(xs, mods_x, s, GRID_W, *args)
            if update_ctx:
                cs = _conformer(cs, mods_c, lc, 1, *args)
        elif kind == 1:
            args = (norm1_g[i], sc_w_in[j].astype(BF16), sc_conv[j], sc_w_out[j].astype(BF16))
            xs = _short_conv(xs, mods_x, s, GRID_W, *args)
            if update_ctx:
                cs = _short_conv(cs, mods_c, lc, lc, *args)
        else:
            xs, cs_new = _s5_mixer(xs, cs, mods_x, mods_c, b, norm1_g[i], s5_a_re[j], s5_a_im[j],
                                   s5_log_dt[j], s5_b_re[j], s5_b_im[j], s5_c_re[j], s5_c_im[j],
                                   s5_d[j], s5_w_glu[j].astype(BF16))
            if update_ctx:
                cs = cs_new

        wr = jnp.zeros((d, V7X_LANES), F32).at[:, :n_groups].set(moe_wg[i])
        wr = wr.at[:, n_groups:n_groups + n_exp].set(moe_we[i])
        br = jnp.zeros((1, V7X_LANES), F32).at[0, :n_groups].set(moe_bg[i])
        br = br.at[0, n_groups:n_groups + n_exp].set(moe_be[i])
        moe_args = (norm2_g[i], wr, br, moe_w13[i].astype(BF16), moe_w2[i].astype(BF16), n_groups,
                    final_g)
        xs = _moe(xs, mods_x, s, *moe_args, final_norm=(i == depth - 1))
        if update_ctx:
            cs = _moe(cs, mods_c, lc, *moe_args, final_norm=False)
    return xs.reshape(b, s, d)
```

```python
import functools

import jax
import jax.numpy as jnp
from jax import lax
from jax.experimental import pallas as pl
from jax.experimental.pallas import tpu as pltpu

F32 = jnp.float32
BF16 = jnp.bfloat16
I32 = jnp.int32
HIGHEST = lax.Precision.HIGHEST

GRID_W = 64
N_MIXERS = 3
TOP_K = 2
RMS_EPS = 1e-6
LN_EPS = 1e-5
S5_DT_FLOOR = -1e-4

V7X_VMEM_BYTES = 64 * 1024 * 1024
V7X_LANES = 128
V7X_SUBLANES = 8
VMEM_LIMIT_BYTES = V7X_VMEM_BYTES - 8 * 1024 * 1024

S5_CHUNK = 16
NEG_BIG = -1e30


def _cparams(n_axes):
    return pltpu.CompilerParams(dimension_semantics=("arbitrary",) * n_axes,
                                vmem_limit_bytes=VMEM_LIMIT_BYTES)


def _row_tile(seq_len, want):
    t = min(seq_len, want)
    while seq_len % t or t % V7X_SUBLANES:
        t -= 1
    return t


def _modnorm(x, g, sc, sh):
    y = x * lax.rsqrt(jnp.mean(x * x, axis=-1, keepdims=True) + RMS_EPS)
    return (y * g) * (1.0 + sc) + sh


def _silu(v):
    return v * jax.nn.sigmoid(v)


def _mod_spec(n_mod, tm, seq_len, d, j):
    if n_mod == 1:
        return pl.BlockSpec((1, 1, d), lambda t: (0, 0, j))
    return pl.BlockSpec((1, 1, d), lambda t: ((t * tm) // seq_len, 0, j))


def _row_spec(tm, d):
    return pl.BlockSpec((tm, d), lambda t: (t, 0))


def _full_spec(shape):
    nd = len(shape)
    return pl.BlockSpec(shape, lambda *_: (0,) * nd)


def _ada_kernel(c_ref, w_ref, b_ref, o_ref):
    o_ref[0] = jnp.dot(_silu(c_ref[...]), w_ref[0], preferred_element_type=F32,
                       precision=HIGHEST) + b_ref[0]


def _ada_table(cin, ada_w, ada_b):
    depth, d, d6 = ada_w.shape
    r = cin.shape[0]
    tn = d6 // 6
    return pl.pallas_call(
        _ada_kernel,
        grid=(depth, d6 // tn),
        in_specs=[pl.BlockSpec((r, d), lambda i, j: (0, 0)),
                  pl.BlockSpec((1, d, tn), lambda i, j: (i, 0, j)),
                  pl.BlockSpec((1, 1, tn), lambda i, j: (i, 0, j))],
        out_specs=pl.BlockSpec((1, r, tn), lambda i, j: (i, 0, j)),
        out_shape=jax.ShapeDtypeStruct((depth, r, d6), F32),
        compiler_params=_cparams(2),
        name="ada_table",
    )(cin, ada_w, ada_b.reshape(depth, 1, d6))


def _conf_in_kernel(x_ref, g_ref, sc_ref, sh_ref, w_ref, z_ref):
    h = _modnorm(x_ref[...], g_ref[...], sc_ref[0], sh_ref[0])
    y = jnp.dot(h.astype(BF16), w_ref[...], preferred_element_type=F32)
    ci = y.shape[1] // 2
    z_ref[...] = (y[:, :ci] * jax.nn.sigmoid(y[:, ci:])).astype(z_ref.dtype)


def _conf_in(x, mods, seq_len, norm_g, w_in):
    t, d = x.shape
    tm = _row_tile(seq_len, 512)
    ci = w_in.shape[1] // 2
    return pl.pallas_call(
        _conf_in_kernel,
        grid=(t // tm,),
        in_specs=[_row_spec(tm, d), _full_spec((1, d)),
                  _mod_spec(mods.shape[0], tm, seq_len, d, 1),
                  _mod_spec(mods.shape[0], tm, seq_len, d, 0),
                  _full_spec(w_in.shape)],
        out_specs=_row_spec(tm, ci),
        out_shape=jax.ShapeDtypeStruct((t, ci), BF16),
        compiler_params=_cparams(1),
        name="conf_in",
    )(x, norm_g.reshape(1, d), mods, mods, w_in)


def _dwconv_kernel(z_ref, w_ref, b_ref, o_ref, src_scr, *, stride, chunk):
    seq_len, cb = z_ref.shape
    taps = w_ref.shape[0]
    half = taps // 2
    aligned = stride % chunk == 0
    pad = 0 if aligned else half * stride
    if pad:
        src_scr[pl.ds(0, pad), :] = jnp.zeros((pad, cb), F32)
        src_scr[pl.ds(pad + seq_len, pad), :] = jnp.zeros((pad, cb), F32)
    src_scr[pl.ds(pad, seq_len), :] = z_ref[...].astype(F32)
    w = w_ref[...]
    bias = b_ref[...]
    for r0 in range(0, seq_len, chunk):
        acc = jnp.broadcast_to(bias, (chunk, cb))
        for k in range(taps):
            lo = r0 + (k - half) * stride
            if aligned and (lo < 0 or lo + chunk > seq_len):
                continue
            acc = acc + w[k:k + 1, :] * src_scr[pl.ds(lo + pad, chunk), :]
        o_ref[pl.ds(r0, chunk), :] = acc.astype(o_ref.dtype)


def _dwconv(z, seq_len, stride, w, b):
    t, c = z.shape
    taps = w.shape[0]
    cb = min(c, 2 * V7X_LANES)
    chunk = _row_tile(seq_len, 64)
    pad = 0 if stride % chunk == 0 else (taps // 2) * stride
    return pl.pallas_call(
        functools.partial(_dwconv_kernel, stride=stride, chunk=chunk),
        grid=(t // seq_len, c // cb),
        in_specs=[pl.BlockSpec((seq_len, cb), lambda s, j: (s, j)),
                  pl.BlockSpec((taps, cb), lambda s, j: (0, j)),
                  pl.BlockSpec((1, cb), lambda s, j: (0, j))],
        out_specs=pl.BlockSpec((seq_len, cb), lambda s, j: (s, j)),
        out_shape=jax.ShapeDtypeStruct((t, c), BF16),
        scratch_shapes=[pltpu.VMEM((seq_len + 2 * pad, cb), F32)],
        compiler_params=_cparams(2),
        name="dwconv",
    )(z, w, b.reshape(1, c))


def _conf_out_kernel(z_ref, lg_ref, lb_ref, w_ref, x_ref, g1_ref, o_ref):
    z = z_ref[...].astype(F32)
    mu = jnp.mean(z, axis=-1, keepdims=True)
    zc = z - mu
    var = jnp.mean(zc * zc, axis=-1, keepdims=True)
    y = zc * lax.rsqrt(var + LN_EPS) * lg_ref[...] + lb_ref[...]
    m = jnp.dot(_silu(y).astype(BF16), w_ref[...], preferred_element_type=F32)
    o_ref[...] = x_ref[...] + g1_ref[0] * m


def _conf_out(z, x, mods, seq_len, ln_g, ln_b, w_out):
    t, d = x.shape
    ci = z.shape[1]
    tm = _row_tile(seq_len, 512)
    return pl.pallas_call(
        _conf_out_kernel,
        grid=(t // tm,),
        in_specs=[_row_spec(tm, ci), _full_spec((1, ci)), _full_spec((1, ci)),
                  _full_spec(w_out.shape), _row_spec(tm, d),
                  _mod_spec(mods.shape[0], tm, seq_len, d, 2)],
        out_specs=_row_spec(tm, d),
        out_shape=jax.ShapeDtypeStruct((t, d), F32),
        compiler_params=_cparams(1),
        name="conf_out",
    )(z, ln_g.reshape(1, ci), ln_b.reshape(1, ci), w_out, x, mods)


def _conformer(x, mods, seq_len, stride, norm_g, w_in, dw, dw_b, ln_g, ln_b, w_out):
    z = _conf_in(x, mods, seq_len, norm_g, w_in)
    z = _dwconv(z, seq_len, stride, dw, dw_b)
    return _conf_out(z, x, mods, seq_len, ln_g, ln_b, w_out)


def _sc_kernel(x_ref, g_ref, sc_ref, sh_ref, win_ref, cw_ref, wout_ref, g1_ref, o_ref, *, period):
    x = x_ref[...]
    tm, d = x.shape
    h = _modnorm(x, g_ref[...], sc_ref[0], sh_ref[0])
    y = jnp.dot(h.astype(BF16), win_ref[...], preferred_element_type=F32)
    gb, gc, v = y[:, :d], y[:, d:2 * d], y[:, 2 * d:]
    u = gc * v
    pos = lax.broadcasted_iota(I32, (tm, 1), 0) % period
    u_prev = jnp.where(pos == 0, 0.0, pltpu.roll(u, 1, 0))
    u_next = jnp.where(pos == period - 1, 0.0, pltpu.roll(u, tm - 1, 0))
    cw = cw_ref[...]
    conv = cw[0:1, :] * u_prev + cw[1:2, :] * u + cw[2:3, :] * u_next
    m = jnp.dot((gb * conv).astype(BF16), wout_ref[...], preferred_element_type=F32)
    o_ref[...] = x + g1_ref[0] * m


def _short_conv(x, mods, seq_len, period, norm_g, w_in, conv_w, w_out):
    t, d = x.shape
    tm = _row_tile(seq_len, 512)
    assert tm % period == 0 and conv_w.shape[0] == 3
    return pl.pallas_call(
        functools.partial(_sc_kernel, period=period),
        grid=(t // tm,),
        in_specs=[_row_spec(tm, d), _full_spec((1, d)),
                  _mod_spec(mods.shape[0], tm, seq_len, d, 1),
                  _mod_spec(mods.shape[0], tm, seq_len, d, 0),
                  _full_spec(w_in.shape), _full_spec(conv_w.shape), _full_spec(w_out.shape),
                  _mod_spec(mods.shape[0], tm, seq_len, d, 2)],
        out_specs=_row_spec(tm, d),
        out_shape=jax.ShapeDtypeStruct((t, d), F32),
        compiler_params=_cparams(1),
        name="short_conv",
    )(x, norm_g.reshape(1, d), mods, mods, w_in, conv_w, w_out, mods)


def _s5_prep_kernel(are_ref, aim_ref, ldt_ref, bre_ref, bim_ref, cre_ref, cim_ref,
                    a16r_ref, a16i_ref, winr_ref, wini_ref, woutr_ref, wouti_ref, k_ref, *, reverse):
    a_re = jnp.minimum(are_ref[0], S5_DT_FLOOR)
    a_im = aim_ref[0]
    dt = jnp.exp(ldt_ref[0])
    b_re, b_im = bre_ref[0], bim_ref[0]
    c_re, c_im = cre_ref[0], cim_ref[0]
    cg = b_re.shape[1]
    tc = S5_CHUNK

    def power(n):
        mag = jnp.exp((n * dt) * a_re)
        ang = (n * dt) * a_im
        return mag * jnp.cos(ang), mag * jnp.sin(ang)

    abar_re, abar_im = power(1)
    den = a_re * a_re + a_im * a_im
    n_re = abar_re - 1.0
    n_im = abar_im
    k_re = (n_re * a_re + n_im * a_im) / den
    k_im = (n_im * a_re - n_re * a_im) / den
    bb_re = k_re * b_re - k_im * b_im
    bb_im = k_re * b_im + k_im * b_re

    e16r, e16i = power(tc)
    a16r_ref[0] = e16r
    a16i_ref[0] = e16i
    for t in range(tc):
        er, ei = power(t if reverse else tc - 1 - t)
        winr_ref[0, :, t * cg:(t + 1) * cg, :] = er * bb_re - ei * bb_im
        wini_ref[0, :, t * cg:(t + 1) * cg, :] = er * bb_im + ei * bb_re
        er, ei = power(tc - t if reverse else t + 1)
        woutr_ref[0, :, t * cg:(t + 1) * cg, :] = c_re * er - c_im * ei
        wouti_ref[0, :, t * cg:(t + 1) * cg, :] = -(c_re * ei + c_im * er)
        er, ei = power(t)
        m_re = c_re * er - c_im * ei
        m_im = c_re * ei + c_im * er
        k_ref[0, :, t * cg:(t + 1) * cg, :] = (
            jnp.einsum('gap,gbp->gab', m_re, bb_re, preferred_element_type=F32, precision=HIGHEST)
            - jnp.einsum('gap,gbp->gab', m_im, bb_im, preferred_element_type=F32, precision=HIGHEST))


def _s5_prep(a_re, a_im, log_dt, b_re, b_im, c_re, c_im, reverse):
    g, p = a_re.shape
    cg = b_re.shape[2]
    gb = min(g, 8)
    rows = S5_CHUNK * cg
    a4 = lambda a: a.reshape(1, g, 1, p)
    bt = lambda b: jnp.swapaxes(b, 1, 2).reshape(1, g, cg, p)
    spec_a = pl.BlockSpec((1, gb, 1, p), lambda i: (0, i, 0, 0))
    spec_b = pl.BlockSpec((1, gb, cg, p), lambda i: (0, i, 0, 0))
    spec_w = pl.BlockSpec((1, gb, rows, p), lambda i: (0, i, 0, 0))
    outs = pl.pallas_call(
        functools.partial(_s5_prep_kernel, reverse=reverse),
        grid=(g // gb,),
        in_specs=[spec_a, spec_a, pl.BlockSpec((1, gb, 1, 1), lambda i: (0, i, 0, 0)),
                  spec_b, spec_b, spec_b, spec_b],
        out_specs=[spec_a, spec_a, spec_w, spec_w, spec_w, spec_w,
                   pl.BlockSpec((1, gb, rows, cg), lambda i: (0, i, 0, 0))],
        out_shape=[jax.ShapeDtypeStruct((1, g, 1, p), F32)] * 2
        + [jax.ShapeDtypeStruct((1, g, rows, p), F32)] * 4
        + [jax.ShapeDtypeStruct((1, g, rows, cg), F32)],
        compiler_params=_cparams(1),
        name="s5_prep",
    )(a4(a_re), a4(a_im), log_dt.reshape(1, g, 1, 1), bt(b_re), bt(b_im),
      c_re.reshape(1, g, cg, p), c_im.reshape(1, g, cg, p))
    a16r, a16i, winr, wini, woutr, wouti, kk = [o[0] for o in outs]
    return a16r, a16i, winr, wini, woutr, wouti, kk


def _pair_blockdiag(w):
    g, r, p = w.shape
    w = w.reshape(g // 2, 2, r, p)
    z = jnp.zeros_like(w[:, 0])
    top = jnp.concatenate([w[:, 0], z], axis=-1)
    bot = jnp.concatenate([z, w[:, 1]], axis=-1)
    return jnp.concatenate([top, bot], axis=1)


def _toeplitz(kk, cg, reverse):
    g = kk.shape[0]
    tc = S5_CHUNK
    k5 = kk.reshape(g, tc, cg, cg)
    t_in = jnp.arange(tc)[None, :]
    t_out = jnp.arange(tc)[:, None]
    lag = (t_in - t_out) if reverse else (t_out - t_in)
    blocks = jnp.where((lag >= 0)[None, :, :, None, None], k5[:, jnp.clip(lag, 0, tc - 1)], 0.0)
    return blocks.transpose(0, 1, 3, 2, 4).reshape(g, tc * cg, tc * cg)


def _s5_core_kernel(vc_ref, vx_ref, a16r_ref, a16i_ref, winr_ref, wini_ref, woutr_ref, wouti_ref,
                    tz_ref, yc_ref, yx_ref, sre, sim, hre, him, *, n_seq):
    r = vc_ref.shape[1]
    n_cc = vc_ref.shape[2]
    n_col = n_cc + vx_ref.shape[2]
    n_chunks = n_col // n_seq
    n_ctx_chunks = n_cc // n_seq
    v = jnp.concatenate([jnp.concatenate([vc_ref[half], vx_ref[half]], axis=-1)
                         for half in range(2)], axis=0)
    for direction in range(2):
        sre[...] = lax.dot_general(v, winr_ref[direction, 0], (((0,), (0,)), ((), ())),
                                   preferred_element_type=F32)
        sim[...] = lax.dot_general(v, wini_ref[direction, 0], (((0,), (0,)), ((), ())),
                                   preferred_element_type=F32)
        ar = a16r_ref[direction, 0]
        ai = a16i_ref[direction, 0]

        def step(k, carry):
            h_r, h_i = carry
            rows = pl.ds(pl.multiple_of(k * n_seq, n_seq), n_seq)
            hre[rows, :] = h_r
            him[rows, :] = h_i
            return (ar * h_r - ai * h_i + sre[rows, :], ar * h_i + ai * h_r + sim[rows, :])

        zero = jnp.zeros((n_seq, sre.shape[1]), F32)
        if direction == 0:
            lax.fori_loop(0, n_chunks, step, (zero, zero))
        else:
            mid = lax.fori_loop(0, n_ctx_chunks,
                                lambda i, c: step(n_ctx_chunks - 1 - i, c), (zero, zero))
            lax.fori_loop(0, n_chunks - n_ctx_chunks,
                          lambda i, c: step(n_chunks - 1 - i, c), mid)

        h_r = hre[...].astype(BF16)
        h_i = him[...].astype(BF16)
        for half in range(2):
            rows = pl.ds(half * r, r)
            part = (lax.dot_general(woutr_ref[direction, 0, rows, :], h_r,
                                    (((1,), (1,)), ((), ())), preferred_element_type=F32)
                    + lax.dot_general(wouti_ref[direction, 0, rows, :], h_i,
                                      (((1,), (1,)), ((), ())), preferred_element_type=F32)
                    + jnp.dot(tz_ref[direction, half], v[half * r:(half + 1) * r, :],
                              preferred_element_type=F32))
            if direction == 0:
                yc_ref[half] = part[:, :n_cc]
                yx_ref[half] = part[:, n_cc:]
            else:
                yc_ref[half] = yc_ref[half] + part[:, :n_cc]
                yx_ref[half] = yx_ref[half] + part[:, n_cc:]


def _s5_core(vc, vx, ops_fwd, ops_bwd, n_seq):
    g, r, n_cc = vc.shape
    n_cx = vx.shape[2]
    cg = r // S5_CHUNK
    stacked = []
    for idx in range(6):
        stacked.append(jnp.stack([ops_fwd[idx], ops_bwd[idx]]))
    a16r, a16i, winr, wini, woutr, wouti = stacked
    p = a16r.shape[-1]
    pair_vec = lambda a: a.reshape(2, g // 2, 1, 2 * p)
    pair_mat = lambda w: jnp.stack([_pair_blockdiag(w[0]), _pair_blockdiag(w[1])]).astype(BF16)
    tz = jnp.stack([_toeplitz(ops_fwd[6], cg, False), _toeplitz(ops_bwd[6], cg, True)]).astype(BF16)
    vec_spec = pl.BlockSpec((2, 1, 1, 2 * p), lambda i: (0, i, 0, 0))
    mat_spec = pl.BlockSpec((2, 1, 2 * r, 2 * p), lambda i: (0, i, 0, 0))
    col_spec = lambda n: pl.BlockSpec((2, r, n), lambda i: (i, 0, 0))
    return pl.pallas_call(
        functools.partial(_s5_core_kernel, n_seq=n_seq),
        grid=(g // 2,),
        in_specs=[col_spec(n_cc), col_spec(n_cx),
                  vec_spec, vec_spec, mat_spec, mat_spec, mat_spec, mat_spec,
                  pl.BlockSpec((2, 2, r, r), lambda i: (0, i, 0, 0))],
        out_specs=[col_spec(n_cc), col_spec(n_cx)],
        out_shape=[jax.ShapeDtypeStruct((g, r, n_cc), F32), jax.ShapeDtypeStruct((g, r, n_cx), F32)],
        scratch_shapes=[pltpu.VMEM((n_cc + n_cx, 2 * p), F32)] * 4,
        compiler_params=_cparams(1),
        name="s5_core",
    )(vc, vx, pair_vec(a16r), pair_vec(a16i), pair_mat(winr), pair_mat(wini),
      pair_mat(woutr), pair_mat(wouti), tz)


def _s5_cols_kernel(x_ref, g_ref, sc_ref, sh_ref, v_ref, row_scr):
    nb, r, d = x_ref.shape
    g = v_ref.shape[0]
    cg = d // g
    tc = v_ref.shape[1] // cg
    nk = r // tc
    x = x_ref[...]
    y = x * lax.rsqrt(jnp.mean(x * x, axis=-1, keepdims=True) + RMS_EPS)
    h = ((y * g_ref[...]) * (1.0 + sc_ref[...]) + sh_ref[...]).reshape(nb * r, d)
    n_lane_blocks = row_scr.shape[0]
    gl = V7X_LANES // cg
    for c in range(n_lane_blocks):
        row_scr[c] = h[:, c * V7X_LANES:(c + 1) * V7X_LANES]
    for t in range(tc):
        for c in range(n_lane_blocks):
            z = jnp.concatenate([row_scr[c, pl.ds(t + tc * k, nb, stride=r), :] for k in range(nk)],
                                axis=0)
            v_ref[c * gl:(c + 1) * gl, t * cg:(t + 1) * cg, :] = (
                z.T.reshape(gl, cg, nk * nb).astype(v_ref.dtype))


def _s5_block_rows(n_seq, seq_len):
    return min(seq_len, S5_CHUNK * max(1, V7X_LANES // n_seq))


def _s5_cols(x, mods, n_seq, seq_len, norm_g, n_groups):
    d = x.shape[1]
    r = _s5_block_rows(n_seq, seq_len)
    cols = (r // S5_CHUNK) * n_seq
    rows = S5_CHUNK * (d // n_groups)
    n_mod = mods.shape[0]
    mod_spec = lambda j: pl.BlockSpec((n_mod, 1, d), lambda i: (0, 0, j))
    return pl.pallas_call(
        _s5_cols_kernel,
        grid=(seq_len // r,),
        in_specs=[pl.BlockSpec((n_seq, r, d), lambda i: (0, i, 0)), _full_spec((1, d)),
                  mod_spec(1), mod_spec(0)],
        out_specs=pl.BlockSpec((n_groups, rows, cols), lambda i: (0, 0, i)),
        out_shape=jax.ShapeDtypeStruct((n_groups, rows, (seq_len // S5_CHUNK) * n_seq), BF16),
        scratch_shapes=[pltpu.VMEM((d // V7X_LANES, n_seq * r, V7X_LANES), F32)],
        compiler_params=_cparams(1),
        name="s5_cols",
    )(x.reshape(n_seq, seq_len, d), norm_g.reshape(1, d), mods, mods)


def _s5_rows_kernel(y_ref, o_ref, row_scr):
    nb, r, d = o_ref.shape
    g = y_ref.shape[0]
    cg = d // g
    tc = y_ref.shape[1] // cg
    nk = r // tc
    n_lane_blocks = row_scr.shape[0]
    gl = V7X_LANES // cg
    for t in range(tc):
        for c in range(n_lane_blocks):
            z = y_ref[c * gl:(c + 1) * gl, t * cg:(t + 1) * cg, :].reshape(V7X_LANES, nk * nb).T
            for k in range(nk):
                row_scr[c, pl.ds(t + tc * k, nb, stride=r), :] = z[k * nb:(k + 1) * nb, :]
    o_ref[...] = jnp.concatenate([row_scr[c] for c in range(n_lane_blocks)],
                                 axis=-1).reshape(nb, r, d)


def _s5_rows(y, n_seq, seq_len):
    n_groups, rows, _ = y.shape
    d = n_groups * (rows // S5_CHUNK)
    r = _s5_block_rows(n_seq, seq_len)
    cols = (r // S5_CHUNK) * n_seq
    out = pl.pallas_call(
        _s5_rows_kernel,
        grid=(seq_len // r,),
        in_specs=[pl.BlockSpec((n_groups, rows, cols), lambda i: (0, 0, i))],
        out_specs=pl.BlockSpec((n_seq, r, d), lambda i: (0, i, 0)),
        out_shape=jax.ShapeDtypeStruct((n_seq, seq_len, d), F32),
        scratch_shapes=[pltpu.VMEM((d // V7X_LANES, n_seq * r, V7X_LANES), F32)],
        compiler_params=_cparams(1),
        name="s5_rows",
    )(y)
    return out.reshape(n_seq * seq_len, d)


def _s5_head_kernel(x_ref, y_ref, g_ref, sc_ref, sh_ref, d_ref, w_ref, g1_ref, o_ref):
    x = x_ref[...]
    h = _modnorm(x, g_ref[...], sc_ref[0], sh_ref[0])
    a = jax.nn.gelu(y_ref[...] + d_ref[...] * h, approximate=True)
    z = jnp.dot(a.astype(BF16), w_ref[...], preferred_element_type=F32)
    dm = z.shape[1] // 2
    o_ref[...] = x + g1_ref[0] * (z[:, :dm] * jax.nn.sigmoid(z[:, dm:]))


def _s5_head(x, y, mods, seq_len, norm_g, d_skip, w_glu):
    t, d = x.shape
    tm = _row_tile(seq_len, 512)
    return pl.pallas_call(
        _s5_head_kernel,
        grid=(t // tm,),
        in_specs=[_row_spec(tm, d), _row_spec(tm, d), _full_spec((1, d)),
                  _mod_spec(mods.shape[0], tm, seq_len, d, 1),
                  _mod_spec(mods.shape[0], tm, seq_len, d, 0),
                  _full_spec((1, d)), _full_spec(w_glu.shape),
                  _mod_spec(mods.shape[0], tm, seq_len, d, 2)],
        out_specs=_row_spec(tm, d),
        out_shape=jax.ShapeDtypeStruct((t, d), F32),
        compiler_params=_cparams(1),
        name="s5_head",
    )(x, y, norm_g.reshape(1, d), mods, mods, d_skip.reshape(1, d), w_glu, mods)


def _s5_mixer(x, ctx, mods_x, mods_c, n_seq, norm_g, a_re, a_im, log_dt, b_re, b_im, c_re, c_im,
              d_skip, w_glu):
    s_len, c_len = x.shape[0] // n_seq, ctx.shape[0] // n_seq
    g = a_re.shape[1]
    ops = [_s5_prep(a_re[k], a_im[k], log_dt[k], b_re[k], b_im[k], c_re[k], c_im[k], bool(k))
           for k in range(2)]
    vc = _s5_cols(ctx, mods_c, n_seq, c_len, norm_g, g)
    vx = _s5_cols(x, mods_x, n_seq, s_len, norm_g, g)
    y_c, y_x = _s5_core(vc, vx, ops[0], ops[1], n_seq)
    x = _s5_head(x, _s5_rows(y_x, n_seq, s_len), mods_x, s_len, norm_g, d_skip, w_glu)
    ctx = _s5_head(ctx, _s5_rows(y_c, n_seq, c_len), mods_c, c_len, norm_g, d_skip, w_glu)
    return x, ctx


def _router_kernel(x_ref, g_ref, sc_ref, sh_ref, wr_ref, br_ref, h_ref, r_ref, *, n_groups, epg):
    h = _modnorm(x_ref[...], g_ref[...], sc_ref[0], sh_ref[0])
    for c in range(h_ref.shape[1]):
        h_ref[:, c, :] = h[:, c * V7X_LANES:(c + 1) * V7X_LANES]
    logits = lax.dot_general(wr_ref[...], h, (((1,), (1,)), ((), ())), preferred_element_type=F32,
                             precision=HIGHEST) + br_ref[...]
    row = lax.broadcasted_iota(I32, logits.shape, 0)
    far = jnp.int32(1 << 20)

    def first_max(vals):
        m = jnp.max(vals, axis=0, keepdims=True)
        return m, jnp.min(jnp.where(vals == m, row, far), axis=0, keepdims=True)

    is_group = row < n_groups
    gl = jnp.where(is_group, logits, NEG_BIG)
    gmax, gidx = first_max(gl)
    gsum = jnp.sum(jnp.where(is_group, jnp.exp(gl - gmax), 0.0), axis=0, keepdims=True)
    g_w = 1.0 / gsum
    lo = n_groups + gidx * epg
    le = jnp.where((row >= lo) & (row < lo + epg), logits, NEG_BIG)
    m1, i1 = first_max(le)
    m2, i2 = first_max(jnp.where(row == i1, NEG_BIG, le))
    ratio = jnp.exp(m2 - m1)
    w1 = g_w / (1.0 + ratio)
    w2 = g_w * ratio / (1.0 + ratio)
    e1 = (i1 - n_groups).astype(F32)
    e2 = (i2 - n_groups).astype(F32)
    out_row = lax.broadcasted_iota(I32, r_ref.shape, 0)
    r_ref[...] = jnp.where(out_row == 0, e1, jnp.where(out_row == 1, e2, jnp.where(
        out_row == 2, w1, jnp.where(out_row == 3, w2, 0.0))))


def _router(x, mods, seq_len, norm_g, wr, br, n_groups, epg):
    t, d = x.shape
    tm = _row_tile(seq_len, 512)
    p = d // V7X_LANES
    return pl.pallas_call(
        functools.partial(_router_kernel, n_groups=n_groups, epg=epg),
        grid=(t // tm,),
        in_specs=[_row_spec(tm, d), _full_spec((1, d)),
                  _mod_spec(mods.shape[0], tm, seq_len, d, 4),
                  _mod_spec(mods.shape[0], tm, seq_len, d, 3),
                  _full_spec(wr.shape), _full_spec(br.shape)],
        out_specs=[pl.BlockSpec((tm, p, V7X_LANES), lambda i: (i, 0, 0)),
                   pl.BlockSpec((V7X_SUBLANES, tm), lambda i: (0, i))],
        out_shape=[jax.ShapeDtypeStruct((t, p, V7X_LANES), F32),
                   jax.ShapeDtypeStruct((V7X_SUBLANES, t), F32)],
        compiler_params=_cparams(1),
        name="moe_router",
    )(x, norm_g.reshape(1, d), mods, mods, wr, br)


def _dispatch_lists(route, n_block, n_exp, cap):
    t = route.shape[1]
    n_sb = t // n_block
    n_assign = n_block * TOP_K
    ids = route[0:TOP_K].T.astype(I32).reshape(n_sb, n_assign)
    wts = route[TOP_K:2 * TOP_K].T.reshape(n_sb, n_assign)
    asg = jnp.broadcast_to(jnp.arange(n_assign, dtype=I32)[None, :], (n_sb, n_assign))
    _, asg_s, w_s = lax.sort((ids, asg, wts), dimension=1, is_stable=True, num_keys=1)
    counts = jnp.sum((ids[:, :, None] == jnp.arange(n_exp, dtype=I32)).astype(I32), axis=1)
    offs = jnp.cumsum(counts, axis=1) - counts
    pad = ((0, 0), (0, cap - n_assign))
    return (counts.reshape(-1), offs.reshape(-1),
            jnp.pad(asg_s, pad, constant_values=n_assign).reshape(n_sb, 1, cap),
            jnp.pad(w_s, pad).reshape(n_sb, 1, cap))


def _moe_kernel(cnt_ref, off_ref, asg_ref, wgt_ref, h_ref, w13_ref, w2_ref, o_ref, lhs_scr, ys_scr,
                *, n_exp, tm):
    sb = pl.program_id(0)
    e = pl.program_id(1)
    n_tok, p, _ = h_ref.shape
    de = w2_ref.shape[1]
    stride = tm + 1
    count = cnt_ref[sb * n_exp + e]
    seg = off_ref[sb * n_exp + e]
    n_out = o_ref.shape[1]

    @pl.when(e == 0)
    def _():
        spare = n_out - n_tok * TOP_K
        o_ref[0, pl.ds(n_tok * TOP_K, spare)] = jnp.zeros((spare,) + o_ref.shape[2:], F32)

    def tile(j, carry):
        base = seg + j * tm
        for mi in range(tm):
            tok = jnp.minimum(lax.div(asg_ref[0, 0, base + mi], TOP_K), n_tok - 1)
            lhs_scr[pl.ds(mi, p, stride=stride), :] = h_ref[tok]
        lhs = jnp.concatenate([lhs_scr[pl.ds(c * stride, tm), :] for c in range(p)], axis=-1)
        hid = jnp.dot(lhs.astype(BF16), w13_ref[0], preferred_element_type=F32)
        act = _silu(hid[:, :de]) * hid[:, de:]
        ys = jnp.dot(act.astype(BF16), w2_ref[0], preferred_element_type=F32)
        for c in range(p):
            ys_scr[pl.ds(c * stride, tm), :] = ys[:, c * V7X_LANES:(c + 1) * V7X_LANES]
        for mi in range(tm):
            o_ref[0, asg_ref[0, 0, base + mi]] = (wgt_ref[0, 0, base + mi]
                                                  * ys_scr[pl.ds(mi, p, stride=stride), :])
        return carry

    lax.fori_loop(0, lax.div(count + tm - 1, tm), tile, 0)


def _moe_tile(n_block, n_exp):
    want = (n_block * TOP_K * 5) // (n_exp * 4)
    return max(2 * V7X_SUBLANES, -(-want // (2 * V7X_SUBLANES)) * 2 * V7X_SUBLANES)


def _moe_experts(h, route, w13, w2, n_block):
    t, p, _ = h.shape
    n_exp, d, de2 = w13.shape
    de = de2 // 2
    n_sb = t // n_block
    n_assign = n_block * TOP_K
    tm = _moe_tile(n_block, n_exp)
    cap = -(-(n_assign + tm) // V7X_LANES) * V7X_LANES
    cnt, off, asg_list, w_list = _dispatch_lists(route, n_block, n_exp, cap)
    list_spec = pl.BlockSpec((1, 1, cap), lambda s, e, *_: (s, 0, 0), memory_space=pltpu.SMEM)
    out_slabs = n_assign + V7X_SUBLANES
    grid_spec = pltpu.PrefetchScalarGridSpec(
        num_scalar_prefetch=2,
        grid=(n_sb, n_exp),
        in_specs=[list_spec, list_spec,
                  pl.BlockSpec((n_block, p, V7X_LANES), lambda s, e, *_: (s, 0, 0),
                               pipeline_mode=pl.Buffered(1)),
                  pl.BlockSpec((1, d, de2), lambda s, e, *_: (e, 0, 0)),
                  pl.BlockSpec((1, de, d), lambda s, e, *_: (e, 0, 0))],
        out_specs=pl.BlockSpec((1, out_slabs, p, V7X_LANES), lambda s, e, *_: (s, 0, 0, 0)),
        scratch_shapes=[pltpu.VMEM((p * (tm + 1), V7X_LANES), F32)] * 2)
    return pl.pallas_call(
        functools.partial(_moe_kernel, n_exp=n_exp, tm=tm),
        grid_spec=grid_spec,
        out_shape=jax.ShapeDtypeStruct((n_sb, out_slabs, p, V7X_LANES), F32),
        compiler_params=_cparams(2),
        name="moe_experts",
    )(cnt, off, asg_list, w_list, h, w13, w2)


def _residual_kernel(x_ref, y_ref, g2_ref, fg_ref, o_ref, *, final_norm):
    tm = x_ref.shape[0]
    y = jnp.concatenate(
        [sum(y_ref[0, pl.ds(k, tm, stride=TOP_K), c, :] for k in range(TOP_K))
         for c in range(y_ref.shape[2])], axis=-1)
    x = x_ref[...] + g2_ref[0] * y
    if final_norm:
        x = x * lax.rsqrt(jnp.mean(x * x, axis=-1, keepdims=True) + RMS_EPS) * fg_ref[...]
    o_ref[...] = x


def _residual(x, y, mods, seq_len, n_block, final_g, final_norm):
    t, d = x.shape
    tm = _row_tile(min(seq_len, n_block), 1024)
    per_sb = n_block // tm
    return pl.pallas_call(
        functools.partial(_residual_kernel, final_norm=final_norm),
        grid=(t // tm,),
        in_specs=[_row_spec(tm, d),
                  pl.BlockSpec((1, tm * TOP_K) + y.shape[2:], lambda i: (i // per_sb, i % per_sb, 0, 0)),
                  _mod_spec(mods.shape[0], tm, seq_len, d, 5), _full_spec((1, d))],
        out_specs=_row_spec(tm, d),
        out_shape=jax.ShapeDtypeStruct((t, d), F32),
        compiler_params=_cparams(1),
        name="moe_residual",
    )(x, y, mods, final_g.reshape(1, d))


def _moe_block(t):
    n = min(t, 2048)
    while t % n:
        n //= 2
    return n


def _moe(x, mods, seq_len, norm_g, wr, br, w13, w2, n_groups, final_g, final_norm):
    n_exp = w13.shape[0]
    n_block = _moe_block(x.shape[0])
    h, route = _router(x, mods, seq_len, norm_g, wr, br, n_groups, n_exp // n_groups)
    y = _moe_experts(h, route, w13, w2, n_block)
    return _residual(x, y, mods, seq_len, n_block, final_g, final_norm)


def kernel(x, c, ctx, c_ctx, ada_w, ada_b, norm1_g, norm2_g, conf_w_in, conf_dw, conf_dw_b, conf_ln_g, conf_ln_b, conf_w_out, sc_w_in, sc_conv, sc_w_out, s5_a_re, s5_a_im, s5_log_dt, s5_b_re, s5_b_im, s5_c_re, s5_c_im, s5_d, s5_w_glu, moe_wg, moe_bg, moe_we, moe_be, moe_w13, moe_w2, final_g):
    b, s, d = x.shape
    lc = ctx.shape[1]
    depth = ada_w.shape[0]
    n_groups = moe_wg.shape[-1]
    n_exp = moe_we.shape[-1]
    assert s % GRID_W == 0

    rows = (b + 1 + V7X_SUBLANES - 1) // V7X_SUBLANES * V7X_SUBLANES
    cin = jnp.zeros((rows, d), F32).at[:b].set(c).at[b].set(c_ctx)
    table = _ada_table(cin, ada_w, ada_b)

    xs = x.reshape(b * s, d)
    cs = ctx.reshape(b * lc, d)
    for i in range(depth):
        kind, j = i % N_MIXERS, i // N_MIXERS
        update_ctx = i < depth - 1
        mods_x = table[i, :b].reshape(b, 1, 6 * d)
        mods_c = table[i, b].reshape(1, 1, 6 * d)
        if kind == 0:
            args = (norm1_g[i], conf_w_in[j].astype(BF16), conf_dw[j], conf_dw_b[j],
                    conf_ln_g[j], conf_ln_b[j], conf_w_out[j].astype(BF16))
            xs = _conformer(xs, mods_x, s, GRID_W, *args)
            if update_ctx:
                cs = _conformer(cs, mods_c, lc, 1, *args)
        elif kind == 1:
            args = (norm1_g[i], sc_w_in[j].astype(BF16), sc_conv[j], sc_w_out[j].astype(BF16))
            xs = _short_conv(xs, mods_x, s, GRID_W, *args)
            if update_ctx:
                cs = _short_conv(cs, mods_c, lc, lc, *args)
        else:
            xs, cs_new = _s5_mixer(xs, cs, mods_x, mods_c, b, norm1_g[i], s5_a_re[j], s5_a_im[j],
                                   s5_log_dt[j], s5_b_re[j], s5_b_im[j], s5_c_re[j], s5_c_im[j],
                                   s5_d[j], s5_w_glu[j].astype(BF16))
            if update_ctx:
                cs = cs_new

        n_logit = -(-(n_groups + n_exp) // V7X_SUBLANES) * V7X_SUBLANES
        wr = jnp.zeros((n_logit, d), F32).at[:n_groups].set(moe_wg[i].T)
        wr = wr.at[n_groups:n_groups + n_exp].set(moe_we[i].T)
        br = jnp.zeros((n_logit, 1), F32).at[:n_groups, 0].set(moe_bg[i])
        br = br.at[n_groups:n_groups + n_exp, 0].set(moe_be[i])
        moe_args = (norm2_g[i], wr, br, moe_w13[i].astype(BF16), moe_w2[i].astype(BF16), n_groups,
                    final_g)
        xs = _moe(xs, mods_x, s, *moe_args, final_norm=(i == depth - 1))
        if update_ctx:
            cs = _moe(cs, mods_c, lc, *moe_args, final_norm=False)
    return xs.reshape(b, s, d)
```

```python
import functools

import jax
import jax.numpy as jnp
from jax import lax
from jax.experimental import pallas as pl
from jax.experimental.pallas import tpu as pltpu

F32 = jnp.float32
BF16 = jnp.bfloat16
I32 = jnp.int32
HIGHEST = lax.Precision.HIGHEST

GRID_W = 64
N_MIXERS = 3
TOP_K = 2
TOP_K_SHIFT = TOP_K.bit_length() - 1
assert TOP_K == 1 << TOP_K_SHIFT
RMS_EPS = 1e-6
LN_EPS = 1e-5
S5_DT_FLOOR = -1e-4

V7X_VMEM_BYTES = 64 * 1024 * 1024
V7X_LANES = 128
V7X_SUBLANES = 8
VMEM_LIMIT_BYTES = V7X_VMEM_BYTES - 8 * 1024 * 1024

S5_CHUNK = 16
NEG_BIG = -1e30


def _cparams(n_axes):
    return pltpu.CompilerParams(dimension_semantics=("arbitrary",) * n_axes,
                                vmem_limit_bytes=VMEM_LIMIT_BYTES)


def _row_tile(seq_len, want):
    t = min(seq_len, want)
    while seq_len % t or t % V7X_SUBLANES:
        t -= 1
    return t


def _modnorm(x, g, sc, sh):
    y = x * lax.rsqrt(jnp.mean(x * x, axis=-1, keepdims=True) + RMS_EPS)
    return (y * g) * (1.0 + sc) + sh


def _silu(v):
    return v * jax.nn.sigmoid(v)


def _mod_spec(n_mod, tm, seq_len, d, j):
    if n_mod == 1:
        return pl.BlockSpec((1, 1, d), lambda t: (0, 0, j))
    return pl.BlockSpec((1, 1, d), lambda t: ((t * tm) // seq_len, 0, j))


def _row_spec(tm, d):
    return pl.BlockSpec((tm, d), lambda t: (t, 0))


def _full_spec(shape):
    nd = len(shape)
    return pl.BlockSpec(shape, lambda *_: (0,) * nd)


def _ada_kernel(c_ref, w_ref, b_ref, o_ref):
    o_ref[0] = jnp.dot(_silu(c_ref[...]), w_ref[0], preferred_element_type=F32,
                       precision=HIGHEST) + b_ref[0]


def _ada_table(cin, ada_w, ada_b):
    depth, d, d6 = ada_w.shape
    r = cin.shape[0]
    tn = d6 // 6
    return pl.pallas_call(
        _ada_kernel,
        grid=(depth, d6 // tn),
        in_specs=[pl.BlockSpec((r, d), lambda i, j: (0, 0)),
                  pl.BlockSpec((1, d, tn), lambda i, j: (i, 0, j)),
                  pl.BlockSpec((1, 1, tn), lambda i, j: (i, 0, j))],
        out_specs=pl.BlockSpec((1, r, tn), lambda i, j: (i, 0, j)),
        out_shape=jax.ShapeDtypeStruct((depth, r, d6), F32),
        compiler_params=_cparams(2),
        name="ada_table",
    )(cin, ada_w, ada_b.reshape(depth, 1, d6))


def _conf_in_kernel(x_ref, g_ref, sc_ref, sh_ref, w_ref, z_ref):
    h = _modnorm(x_ref[...], g_ref[...], sc_ref[0], sh_ref[0])
    y = jnp.dot(h.astype(BF16), w_ref[...], preferred_element_type=F32)
    ci = y.shape[1] // 2
    z_ref[...] = (y[:, :ci] * jax.nn.sigmoid(y[:, ci:])).astype(z_ref.dtype)


def _conf_in(x, mods, seq_len, norm_g, w_in):
    t, d = x.shape
    tm = _row_tile(seq_len, 512)
    ci = w_in.shape[1] // 2
    return pl.pallas_call(
        _conf_in_kernel,
        grid=(t // tm,),
        in_specs=[_row_spec(tm, d), _full_spec((1, d)),
                  _mod_spec(mods.shape[0], tm, seq_len, d, 1),
                  _mod_spec(mods.shape[0], tm, seq_len, d, 0),
                  _full_spec(w_in.shape)],
        out_specs=_row_spec(tm, ci),
        out_shape=jax.ShapeDtypeStruct((t, ci), BF16),
        compiler_params=_cparams(1),
        name="conf_in",
    )(x, norm_g.reshape(1, d), mods, mods, w_in)


def _dwconv_kernel(z_ref, w_ref, b_ref, o_ref, src_scr, *, stride, chunk):
    seq_len, cb = z_ref.shape
    taps = w_ref.shape[0]
    half = taps // 2
    aligned = stride % chunk == 0
    pad = 0 if aligned else half * stride
    if pad:
        src_scr[pl.ds(0, pad), :] = jnp.zeros((pad, cb), F32)
        src_scr[pl.ds(pad + seq_len, pad), :] = jnp.zeros((pad, cb), F32)
    src_scr[pl.ds(pad, seq_len), :] = z_ref[...].astype(F32)
    w = w_ref[...]
    bias = b_ref[...]
    for r0 in range(0, seq_len, chunk):
        acc = jnp.broadcast_to(bias, (chunk, cb))
        for k in range(taps):
            lo = r0 + (k - half) * stride
            if aligned and (lo < 0 or lo + chunk > seq_len):
                continue
            acc = acc + w[k:k + 1, :] * src_scr[pl.ds(lo + pad, chunk), :]
        o_ref[pl.ds(r0, chunk), :] = acc.astype(o_ref.dtype)


def _dwconv(z, seq_len, stride, w, b):
    t, c = z.shape
    taps = w.shape[0]
    cb = min(c, 2 * V7X_LANES)
    chunk = _row_tile(seq_len, 64)
    pad = 0 if stride % chunk == 0 else (taps // 2) * stride
    return pl.pallas_call(
        functools.partial(_dwconv_kernel, stride=stride, chunk=chunk),
        grid=(t // seq_len, c // cb),
        in_specs=[pl.BlockSpec((seq_len, cb), lambda s, j: (s, j)),
                  pl.BlockSpec((taps, cb), lambda s, j: (0, j)),
                  pl.BlockSpec((1, cb), lambda s, j: (0, j))],
        out_specs=pl.BlockSpec((seq_len, cb), lambda s, j: (s, j)),
        out_shape=jax.ShapeDtypeStruct((t, c), BF16),
        scratch_shapes=[pltpu.VMEM((seq_len + 2 * pad, cb), F32)],
        compiler_params=_cparams(2),
        name="dwconv",
    )(z, w, b.reshape(1, c))


def _conf_out_kernel(z_ref, lg_ref, lb_ref, w_ref, x_ref, g1_ref, o_ref):
    z = z_ref[...].astype(F32)
    mu = jnp.mean(z, axis=-1, keepdims=True)
    zc = z - mu
    var = jnp.mean(zc * zc, axis=-1, keepdims=True)
    y = zc * lax.rsqrt(var + LN_EPS) * lg_ref[...] + lb_ref[...]
    m = jnp.dot(_silu(y).astype(BF16), w_ref[...], preferred_element_type=F32)
    o_ref[...] = x_ref[...] + g1_ref[0] * m


def _conf_out(z, x, mods, seq_len, ln_g, ln_b, w_out):
    t, d = x.shape
    ci = z.shape[1]
    tm = _row_tile(seq_len, 512)
    return pl.pallas_call(
        _conf_out_kernel,
        grid=(t // tm,),
        in_specs=[_row_spec(tm, ci), _full_spec((1, ci)), _full_spec((1, ci)),
                  _full_spec(w_out.shape), _row_spec(tm, d),
                  _mod_spec(mods.shape[0], tm, seq_len, d, 2)],
        out_specs=_row_spec(tm, d),
        out_shape=jax.ShapeDtypeStruct((t, d), F32),
        compiler_params=_cparams(1),
        name="conf_out",
    )(z, ln_g.reshape(1, ci), ln_b.reshape(1, ci), w_out, x, mods)


def _conformer(x, mods, seq_len, stride, norm_g, w_in, dw, dw_b, ln_g, ln_b, w_out):
    z = _conf_in(x, mods, seq_len, norm_g, w_in)
    z = _dwconv(z, seq_len, stride, dw, dw_b)
    return _conf_out(z, x, mods, seq_len, ln_g, ln_b, w_out)


def _sc_kernel(x_ref, g_ref, sc_ref, sh_ref, win_ref, cw_ref, wout_ref, g1_ref, o_ref, *, period):
    x = x_ref[...]
    tm, d = x.shape
    h = _modnorm(x, g_ref[...], sc_ref[0], sh_ref[0])
    y = jnp.dot(h.astype(BF16), win_ref[...], preferred_element_type=F32)
    gb, gc, v = y[:, :d], y[:, d:2 * d], y[:, 2 * d:]
    u = gc * v
    pos = lax.broadcasted_iota(I32, (tm, 1), 0) % period
    u_prev = jnp.where(pos == 0, 0.0, pltpu.roll(u, 1, 0))
    u_next = jnp.where(pos == period - 1, 0.0, pltpu.roll(u, tm - 1, 0))
    cw = cw_ref[...]
    conv = cw[0:1, :] * u_prev + cw[1:2, :] * u + cw[2:3, :] * u_next
    m = jnp.dot((gb * conv).astype(BF16), wout_ref[...], preferred_element_type=F32)
    o_ref[...] = x + g1_ref[0] * m


def _short_conv(x, mods, seq_len, period, norm_g, w_in, conv_w, w_out):
    t, d = x.shape
    tm = _row_tile(seq_len, 512)
    assert tm % period == 0 and conv_w.shape[0] == 3
    return pl.pallas_call(
        functools.partial(_sc_kernel, period=period),
        grid=(t // tm,),
        in_specs=[_row_spec(tm, d), _full_spec((1, d)),
                  _mod_spec(mods.shape[0], tm, seq_len, d, 1),
                  _mod_spec(mods.shape[0], tm, seq_len, d, 0),
                  _full_spec(w_in.shape), _full_spec(conv_w.shape), _full_spec(w_out.shape),
                  _mod_spec(mods.shape[0], tm, seq_len, d, 2)],
        out_specs=_row_spec(tm, d),
        out_shape=jax.ShapeDtypeStruct((t, d), F32),
        compiler_params=_cparams(1),
        name="short_conv",
    )(x, norm_g.reshape(1, d), mods, mods, w_in, conv_w, w_out, mods)


def _s5_prep_kernel(are_ref, aim_ref, ldt_ref, bre_ref, bim_ref, cre_ref, cim_ref,
                    a16r_ref, a16i_ref, winr_ref, wini_ref, woutr_ref, wouti_ref, k_ref, *, reverse):
    a_re = jnp.minimum(are_ref[0], S5_DT_FLOOR)
    a_im = aim_ref[0]
    dt = jnp.exp(ldt_ref[0])
    b_re, b_im = bre_ref[0], bim_ref[0]
    c_re, c_im = cre_ref[0], cim_ref[0]
    cg = b_re.shape[1]
    tc = S5_CHUNK

    def power(n):
        mag = jnp.exp((n * dt) * a_re)
        ang = (n * dt) * a_im
        return mag * jnp.cos(ang), mag * jnp.sin(ang)

    abar_re, abar_im = power(1)
    den = a_re * a_re + a_im * a_im
    n_re = abar_re - 1.0
    n_im = abar_im
    k_re = (n_re * a_re + n_im * a_im) / den
    k_im = (n_im * a_re - n_re * a_im) / den
    bb_re = k_re * b_re - k_im * b_im
    bb_im = k_re * b_im + k_im * b_re

    e16r, e16i = power(tc)
    a16r_ref[0] = e16r
    a16i_ref[0] = e16i
    for t in range(tc):
        er, ei = power(t if reverse else tc - 1 - t)
        winr_ref[0, :, t * cg:(t + 1) * cg, :] = er * bb_re - ei * bb_im
        wini_ref[0, :, t * cg:(t + 1) * cg, :] = er * bb_im + ei * bb_re
        er, ei = power(tc - t if reverse else t + 1)
        woutr_ref[0, :, t * cg:(t + 1) * cg, :] = c_re * er - c_im * ei
        wouti_ref[0, :, t * cg:(t + 1) * cg, :] = -(c_re * ei + c_im * er)
        er, ei = power(t)
        m_re = c_re * er - c_im * ei
        m_im = c_re * ei + c_im * er
        k_ref[0, :, t * cg:(t + 1) * cg, :] = (
            jnp.einsum('gap,gbp->gab', m_re, bb_re, preferred_element_type=F32, precision=HIGHEST)
            - jnp.einsum('gap,gbp->gab', m_im, bb_im, preferred_element_type=F32, precision=HIGHEST))


def _s5_prep(a_re, a_im, log_dt, b_re, b_im, c_re, c_im, reverse):
    g, p = a_re.shape
    cg = b_re.shape[2]
    gb = min(g, 8)
    rows = S5_CHUNK * cg
    a4 = lambda a: a.reshape(1, g, 1, p)
    bt = lambda b: jnp.swapaxes(b, 1, 2).reshape(1, g, cg, p)
    spec_a = pl.BlockSpec((1, gb, 1, p), lambda i: (0, i, 0, 0))
    spec_b = pl.BlockSpec((1, gb, cg, p), lambda i: (0, i, 0, 0))
    spec_w = pl.BlockSpec((1, gb, rows, p), lambda i: (0, i, 0, 0))
    outs = pl.pallas_call(
        functools.partial(_s5_prep_kernel, reverse=reverse),
        grid=(g // gb,),
        in_specs=[spec_a, spec_a, pl.BlockSpec((1, gb, 1, 1), lambda i: (0, i, 0, 0)),
                  spec_b, spec_b, spec_b, spec_b],
        out_specs=[spec_a, spec_a, spec_w, spec_w, spec_w, spec_w,
                   pl.BlockSpec((1, gb, rows, cg), lambda i: (0, i, 0, 0))],
        out_shape=[jax.ShapeDtypeStruct((1, g, 1, p), F32)] * 2
        + [jax.ShapeDtypeStruct((1, g, rows, p), F32)] * 4
        + [jax.ShapeDtypeStruct((1, g, rows, cg), F32)],
        compiler_params=_cparams(1),
        name="s5_prep",
    )(a4(a_re), a4(a_im), log_dt.reshape(1, g, 1, 1), bt(b_re), bt(b_im),
      c_re.reshape(1, g, cg, p), c_im.reshape(1, g, cg, p))
    a16r, a16i, winr, wini, woutr, wouti, kk = [o[0] for o in outs]
    return a16r, a16i, winr, wini, woutr, wouti, kk


def _pair_blockdiag(w):
    g, r, p = w.shape
    w = w.reshape(g // 2, 2, r, p)
    z = jnp.zeros_like(w[:, 0])
    top = jnp.concatenate([w[:, 0], z], axis=-1)
    bot = jnp.concatenate([z, w[:, 1]], axis=-1)
    return jnp.concatenate([top, bot], axis=1)


def _toeplitz(kk, cg, reverse):
    g = kk.shape[0]
    tc = S5_CHUNK
    k5 = kk.reshape(g, tc, cg, cg)
    t_in = jnp.arange(tc)[None, :]
    t_out = jnp.arange(tc)[:, None]
    lag = (t_in - t_out) if reverse else (t_out - t_in)
    blocks = jnp.where((lag >= 0)[None, :, :, None, None], k5[:, jnp.clip(lag, 0, tc - 1)], 0.0)
    return blocks.transpose(0, 1, 3, 2, 4).reshape(g, tc * cg, tc * cg)


def _s5_core_kernel(vc_ref, vx_ref, a16r_ref, a16i_ref, winr_ref, wini_ref, woutr_ref, wouti_ref,
                    tz_ref, yc_ref, yx_ref, sre, sim, hre, him, *, n_seq):
    r = vc_ref.shape[1]
    n_cc = vc_ref.shape[2]
    n_col = n_cc + vx_ref.shape[2]
    n_chunks = n_col // n_seq
    n_ctx_chunks = n_cc // n_seq
    v = jnp.concatenate([jnp.concatenate([vc_ref[half], vx_ref[half]], axis=-1)
                         for half in range(2)], axis=0)
    for direction in range(2):
        sre[...] = lax.dot_general(v, winr_ref[direction, 0], (((0,), (0,)), ((), ())),
                                   preferred_element_type=F32)
        sim[...] = lax.dot_general(v, wini_ref[direction, 0], (((0,), (0,)), ((), ())),
                                   preferred_element_type=F32)
        ar = a16r_ref[direction, 0]
        ai = a16i_ref[direction, 0]

        def step(k, carry):
            h_r, h_i = carry
            rows = pl.ds(pl.multiple_of(k * n_seq, n_seq), n_seq)
            hre[rows, :] = h_r
            him[rows, :] = h_i
            return (ar * h_r - ai * h_i + sre[rows, :], ar * h_i + ai * h_r + sim[rows, :])

        zero = jnp.zeros((n_seq, sre.shape[1]), F32)
        if direction == 0:
            lax.fori_loop(0, n_chunks, step, (zero, zero))
        else:
            mid = lax.fori_loop(0, n_ctx_chunks,
                                lambda i, c: step(n_ctx_chunks - 1 - i, c), (zero, zero))
            lax.fori_loop(0, n_chunks - n_ctx_chunks,
                          lambda i, c: step(n_chunks - 1 - i, c), mid)

        h_r = hre[...].astype(BF16)
        h_i = him[...].astype(BF16)
        for half in range(2):
            rows = pl.ds(half * r, r)
            part = (lax.dot_general(woutr_ref[direction, 0, rows, :], h_r,
                                    (((1,), (1,)), ((), ())), preferred_element_type=F32)
                    + lax.dot_general(wouti_ref[direction, 0, rows, :], h_i,
                                      (((1,), (1,)), ((), ())), preferred_element_type=F32)
                    + jnp.dot(tz_ref[direction, half], v[half * r:(half + 1) * r, :],
                              preferred_element_type=F32))
            if direction == 0:
                yc_ref[half] = part[:, :n_cc]
                yx_ref[half] = part[:, n_cc:]
            else:
                yc_ref[half] = yc_ref[half] + part[:, :n_cc]
                yx_ref[half] = yx_ref[half] + part[:, n_cc:]


def _s5_core(vc, vx, ops_fwd, ops_bwd, n_seq):
    g, r, n_cc = vc.shape
    n_cx = vx.shape[2]
    cg = r // S5_CHUNK
    stacked = []
    for idx in range(6):
        stacked.append(jnp.stack([ops_fwd[idx], ops_bwd[idx]]))
    a16r, a16i, winr, wini, woutr, wouti = stacked
    p = a16r.shape[-1]
    pair_vec = lambda a: a.reshape(2, g // 2, 1, 2 * p)
    pair_mat = lambda w: jnp.stack([_pair_blockdiag(w[0]), _pair_blockdiag(w[1])]).astype(BF16)
    tz = jnp.stack([_toeplitz(ops_fwd[6], cg, False), _toeplitz(ops_bwd[6], cg, True)]).astype(BF16)
    vec_spec = pl.BlockSpec((2, 1, 1, 2 * p), lambda i: (0, i, 0, 0))
    mat_spec = pl.BlockSpec((2, 1, 2 * r, 2 * p), lambda i: (0, i, 0, 0))
    col_spec = lambda n: pl.BlockSpec((2, r, n), lambda i: (i, 0, 0))
    return pl.pallas_call(
        functools.partial(_s5_core_kernel, n_seq=n_seq),
        grid=(g // 2,),
        in_specs=[col_spec(n_cc), col_spec(n_cx),
                  vec_spec, vec_spec, mat_spec, mat_spec, mat_spec, mat_spec,
                  pl.BlockSpec((2, 2, r, r), lambda i: (0, i, 0, 0))],
        out_specs=[col_spec(n_cc), col_spec(n_cx)],
        out_shape=[jax.ShapeDtypeStruct((g, r, n_cc), F32), jax.ShapeDtypeStruct((g, r, n_cx), F32)],
        scratch_shapes=[pltpu.VMEM((n_cc + n_cx, 2 * p), F32)] * 4,
        compiler_params=_cparams(1),
        name="s5_core",
    )(vc, vx, pair_vec(a16r), pair_vec(a16i), pair_mat(winr), pair_mat(wini),
      pair_mat(woutr), pair_mat(wouti), tz)


def _s5_cols_kernel(x_ref, g_ref, sc_ref, sh_ref, v_ref, row_scr):
    nb, r, d = x_ref.shape
    g = v_ref.shape[0]
    cg = d // g
    tc = v_ref.shape[1] // cg
    nk = r // tc
    x = x_ref[...]
    y = x * lax.rsqrt(jnp.mean(x * x, axis=-1, keepdims=True) + RMS_EPS)
    h = ((y * g_ref[...]) * (1.0 + sc_ref[...]) + sh_ref[...]).reshape(nb * r, d)
    n_lane_blocks = row_scr.shape[0]
    gl = V7X_LANES // cg
    for c in range(n_lane_blocks):
        row_scr[c] = h[:, c * V7X_LANES:(c + 1) * V7X_LANES]
    for t in range(tc):
        for c in range(n_lane_blocks):
            z = jnp.concatenate([row_scr[c, pl.ds(t + tc * k, nb, stride=r), :] for k in range(nk)],
                                axis=0)
            v_ref[c * gl:(c + 1) * gl, t * cg:(t + 1) * cg, :] = (
                z.T.reshape(gl, cg, nk * nb).astype(v_ref.dtype))


def _s5_block_rows(n_seq, seq_len):
    return min(seq_len, S5_CHUNK * max(1, V7X_LANES // n_seq))


def _s5_cols(x, mods, n_seq, seq_len, norm_g, n_groups):
    d = x.shape[1]
    r = _s5_block_rows(n_seq, seq_len)
    cols = (r // S5_CHUNK) * n_seq
    rows = S5_CHUNK * (d // n_groups)
    n_mod = mods.shape[0]
    mod_spec = lambda j: pl.BlockSpec((n_mod, 1, d), lambda i: (0, 0, j))
    return pl.pallas_call(
        _s5_cols_kernel,
        grid=(seq_len // r,),
        in_specs=[pl.BlockSpec((n_seq, r, d), lambda i: (0, i, 0)), _full_spec((1, d)),
                  mod_spec(1), mod_spec(0)],
        out_specs=pl.BlockSpec((n_groups, rows, cols), lambda i: (0, 0, i)),
        out_shape=jax.ShapeDtypeStruct((n_groups, rows, (seq_len // S5_CHUNK) * n_seq), BF16),
        scratch_shapes=[pltpu.VMEM((d // V7X_LANES, n_seq * r, V7X_LANES), F32)],
        compiler_params=_cparams(1),
        name="s5_cols",
    )(x.reshape(n_seq, seq_len, d), norm_g.reshape(1, d), mods, mods)


def _s5_rows_kernel(y_ref, o_ref, row_scr):
    nb, r, d = o_ref.shape
    g = y_ref.shape[0]
    cg = d // g
    tc = y_ref.shape[1] // cg
    nk = r // tc
    n_lane_blocks = row_scr.shape[0]
    gl = V7X_LANES // cg
    for t in range(tc):
        for c in range(n_lane_blocks):
            z = y_ref[c * gl:(c + 1) * gl, t * cg:(t + 1) * cg, :].reshape(V7X_LANES, nk * nb).T
            for k in range(nk):
                row_scr[c, pl.ds(t + tc * k, nb, stride=r), :] = z[k * nb:(k + 1) * nb, :]
    o_ref[...] = jnp.concatenate([row_scr[c] for c in range(n_lane_blocks)],
                                 axis=-1).reshape(nb, r, d)


def _s5_rows(y, n_seq, seq_len):
    n_groups, rows, _ = y.shape
    d = n_groups * (rows // S5_CHUNK)
    r = _s5_block_rows(n_seq, seq_len)
    cols = (r // S5_CHUNK) * n_seq
    out = pl.pallas_call(
        _s5_rows_kernel,
        grid=(seq_len // r,),
        in_specs=[pl.BlockSpec((n_groups, rows, cols), lambda i: (0, 0, i))],
        out_specs=pl.BlockSpec((n_seq, r, d), lambda i: (0, i, 0)),
        out_shape=jax.ShapeDtypeStruct((n_seq, seq_len, d), F32),
        scratch_shapes=[pltpu.VMEM((d // V7X_LANES, n_seq * r, V7X_LANES), F32)],
        compiler_params=_cparams(1),
        name="s5_rows",
    )(y)
    return out.reshape(n_seq * seq_len, d)


def _s5_head_kernel(x_ref, y_ref, g_ref, sc_ref, sh_ref, d_ref, w_ref, g1_ref, o_ref):
    x = x_ref[...]
    h = _modnorm(x, g_ref[...], sc_ref[0], sh_ref[0])
    a = jax.nn.gelu(y_ref[...] + d_ref[...] * h, approximate=True)
    z = jnp.dot(a.astype(BF16), w_ref[...], preferred_element_type=F32)
    dm = z.shape[1] // 2
    o_ref[...] = x + g1_ref[0] * (z[:, :dm] * jax.nn.sigmoid(z[:, dm:]))


def _s5_head(x, y, mods, seq_len, norm_g, d_skip, w_glu):
    t, d = x.shape
    tm = _row_tile(seq_len, 512)
    return pl.pallas_call(
        _s5_head_kernel,
        grid=(t // tm,),
        in_specs=[_row_spec(tm, d), _row_spec(tm, d), _full_spec((1, d)),
                  _mod_spec(mods.shape[0], tm, seq_len, d, 1),
                  _mod_spec(mods.shape[0], tm, seq_len, d, 0),
                  _full_spec((1, d)), _full_spec(w_glu.shape),
                  _mod_spec(mods.shape[0], tm, seq_len, d, 2)],
        out_specs=_row_spec(tm, d),
        out_shape=jax.ShapeDtypeStruct((t, d), F32),
        compiler_params=_cparams(1),
        name="s5_head",
    )(x, y, norm_g.reshape(1, d), mods, mods, d_skip.reshape(1, d), w_glu, mods)


def _s5_mixer(x, ctx, mods_x, mods_c, n_seq, norm_g, a_re, a_im, log_dt, b_re, b_im, c_re, c_im,
              d_skip, w_glu):
    s_len, c_len = x.shape[0] // n_seq, ctx.shape[0] // n_seq
    g = a_re.shape[1]
    ops = [_s5_prep(a_re[k], a_im[k], log_dt[k], b_re[k], b_im[k], c_re[k], c_im[k], bool(k))
           for k in range(2)]
    vc = _s5_cols(ctx, mods_c, n_seq, c_len, norm_g, g)
    vx = _s5_cols(x, mods_x, n_seq, s_len, norm_g, g)
    y_c, y_x = _s5_core(vc, vx, ops[0], ops[1], n_seq)
    x = _s5_head(x, _s5_rows(y_x, n_seq, s_len), mods_x, s_len, norm_g, d_skip, w_glu)
    ctx = _s5_head(ctx, _s5_rows(y_c, n_seq, c_len), mods_c, c_len, norm_g, d_skip, w_glu)
    return x, ctx


def _router_kernel(x_ref, g_ref, sc_ref, sh_ref, wr_ref, br_ref, h_ref, r_ref, *, n_groups, epg):
    h = _modnorm(x_ref[...], g_ref[...], sc_ref[0], sh_ref[0])
    for c in range(h_ref.shape[1]):
        h_ref[:, c, :] = h[:, c * V7X_LANES:(c + 1) * V7X_LANES]
    logits = lax.dot_general(wr_ref[...], h, (((1,), (1,)), ((), ())), preferred_element_type=F32,
                             precision=HIGHEST) + br_ref[...]
    row = lax.broadcasted_iota(I32, logits.shape, 0)
    far = jnp.int32(1 << 20)

    def first_max(vals):
        m = jnp.max(vals, axis=0, keepdims=True)
        return m, jnp.min(jnp.where(vals == m, row, far), axis=0, keepdims=True)

    is_group = row < n_groups
    gl = jnp.where(is_group, logits, NEG_BIG)
    gmax, gidx = first_max(gl)
    gsum = jnp.sum(jnp.where(is_group, jnp.exp(gl - gmax), 0.0), axis=0, keepdims=True)
    g_w = 1.0 / gsum
    lo = n_groups + gidx * epg
    le = jnp.where((row >= lo) & (row < lo + epg), logits, NEG_BIG)
    m1, i1 = first_max(le)
    m2, i2 = first_max(jnp.where(row == i1, NEG_BIG, le))
    ratio = jnp.exp(m2 - m1)
    w1 = g_w / (1.0 + ratio)
    w2 = g_w * ratio / (1.0 + ratio)
    e1 = (i1 - n_groups).astype(F32)
    e2 = (i2 - n_groups).astype(F32)
    out_row = lax.broadcasted_iota(I32, r_ref.shape, 0)
    r_ref[...] = jnp.where(out_row == 0, e1, jnp.where(out_row == 1, e2, jnp.where(
        out_row == 2, w1, jnp.where(out_row == 3, w2, 0.0))))


def _router(x, mods, seq_len, norm_g, wr, br, n_groups, epg):
    t, d = x.shape
    tm = _row_tile(seq_len, 512)
    p = d // V7X_LANES
    return pl.pallas_call(
        functools.partial(_router_kernel, n_groups=n_groups, epg=epg),
        grid=(t // tm,),
        in_specs=[_row_spec(tm, d), _full_spec((1, d)),
                  _mod_spec(mods.shape[0], tm, seq_len, d, 4),
                  _mod_spec(mods.shape[0], tm, seq_len, d, 3),
                  _full_spec(wr.shape), _full_spec(br.shape)],
        out_specs=[pl.BlockSpec((tm, p, V7X_LANES), lambda i: (i, 0, 0)),
                   pl.BlockSpec((V7X_SUBLANES, tm), lambda i: (0, i))],
        out_shape=[jax.ShapeDtypeStruct((t, p, V7X_LANES), F32),
                   jax.ShapeDtypeStruct((V7X_SUBLANES, t), F32)],
        compiler_params=_cparams(1),
        name="moe_router",
    )(x, norm_g.reshape(1, d), mods, mods, wr, br)


def _dispatch_lists(route, n_block, n_exp, cap):
    t = route.shape[1]
    n_sb = t // n_block
    n_assign = n_block * TOP_K
    ids = route[0:TOP_K].T.astype(I32).reshape(n_sb, n_assign)
    wts = route[TOP_K:2 * TOP_K].T.reshape(n_sb, n_assign)
    asg = jnp.broadcast_to(jnp.arange(n_assign, dtype=I32)[None, :], (n_sb, n_assign))
    _, asg_s, w_s = lax.sort((ids, asg, wts), dimension=1, is_stable=True, num_keys=1)
    counts = jnp.sum((ids[:, :, None] == jnp.arange(n_exp, dtype=I32)).astype(I32), axis=1)
    offs = jnp.cumsum(counts, axis=1) - counts
    pad = ((0, 0), (0, cap - n_assign))
    return (counts.reshape(-1), offs.reshape(-1),
            jnp.pad(asg_s, pad, constant_values=n_assign).reshape(n_sb, 1, cap),
            jnp.pad(w_s, pad).reshape(n_sb, 1, cap))


def _moe_kernel(cnt_ref, off_ref, asg_ref, wgt_ref, h_ref, w13_ref, w2_ref, o_ref,
                lhs_a, lhs_b, ys_a, ys_b, *, n_exp, tile):
    sb = pl.program_id(0)
    e = pl.program_id(1)
    n_tok, p, _ = h_ref.shape
    de = w2_ref.shape[1]
    count = cnt_ref[sb * n_exp + e]
    seg = off_ref[sb * n_exp + e]
    n_out = o_ref.shape[1]
    scratch = ((lhs_a, ys_a), (lhs_b, ys_b))

    @pl.when(e == 0)
    def _():
        spare = n_out - n_tok * TOP_K
        o_ref[0, pl.ds(n_tok * TOP_K, spare)] = jnp.zeros((spare,) + o_ref.shape[2:], F32)

    def run_tiles(base, sizes):
        starts = [base + sum(sizes[:k]) for k in range(len(sizes))]
        for (lhs_scr, _), start, size in zip(scratch, starts, sizes):
            stride = size + 1
            for mi in range(size):
                tok = jnp.minimum(lax.shift_right_logical(asg_ref[0, 0, start + mi], TOP_K_SHIFT),
                                  n_tok - 1)
                lhs_scr[pl.ds(mi, p, stride=stride), :] = h_ref[tok]
        for (lhs_scr, ys_scr), size in zip(scratch, sizes):
            stride = size + 1
            lhs = jnp.concatenate([lhs_scr[pl.ds(c * stride, size), :] for c in range(p)], axis=-1)
            hid = jnp.dot(lhs.astype(BF16), w13_ref[0], preferred_element_type=F32)
            act = _silu(hid[:, :de]) * hid[:, de:]
            ys = jnp.dot(act.astype(BF16), w2_ref[0], preferred_element_type=F32)
            for c in range(p):
                ys_scr[pl.ds(c * stride, size), :] = ys[:, c * V7X_LANES:(c + 1) * V7X_LANES]
        for (_, ys_scr), start, size in zip(scratch, starts, sizes):
            stride = size + 1
            for mi in range(size):
                o_ref[0, asg_ref[0, 0, start + mi]] = (wgt_ref[0, 0, start + mi]
                                                       * ys_scr[pl.ds(mi, p, stride=stride), :])

    half = tile // 2
    n_pairs = lax.div(count, 2 * tile)

    def pair(j, carry):
        run_tiles(seg + j * 2 * tile, (tile, tile))
        return carry
    lax.fori_loop(0, n_pairs, pair, 0)

    rem = count - n_pairs * 2 * tile
    rem_base = seg + n_pairs * 2 * tile
    for hi, sizes in ((half, (half,)), (tile, (tile,)), (tile + half, (tile, half)),
                      (2 * tile, (tile, tile))):
        @pl.when((rem > hi - half) & (rem <= hi))
        def _():
            run_tiles(rem_base, sizes)


def _moe_tile(n_block, n_exp):
    return max(4 * V7X_SUBLANES, n_block * TOP_K // n_exp)


def _moe_experts(h, route, w13, w2, n_block):
    t, p, _ = h.shape
    n_exp, d, de2 = w13.shape
    de = de2 // 2
    n_sb = t // n_block
    n_assign = n_block * TOP_K
    tile = _moe_tile(n_block, n_exp)
    cap = -(-(n_assign + 2 * tile) // V7X_LANES) * V7X_LANES
    cnt, off, asg_list, w_list = _dispatch_lists(route, n_block, n_exp, cap)
    list_spec = pl.BlockSpec((1, 1, cap), lambda s, e, *_: (s, 0, 0), memory_space=pltpu.SMEM)
    out_slabs = n_assign + V7X_SUBLANES
    grid_spec = pltpu.PrefetchScalarGridSpec(
        num_scalar_prefetch=2,
        grid=(n_sb, n_exp),
        in_specs=[list_spec, list_spec,
                  pl.BlockSpec((n_block, p, V7X_LANES), lambda s, e, *_: (s, 0, 0),
                               pipeline_mode=pl.Buffered(1)),
                  pl.BlockSpec((1, d, de2), lambda s, e, *_: (e, 0, 0)),
                  pl.BlockSpec((1, de, d), lambda s, e, *_: (e, 0, 0))],
        out_specs=pl.BlockSpec((1, out_slabs, p, V7X_LANES), lambda s, e, *_: (s, 0, 0, 0)),
        scratch_shapes=[pltpu.VMEM((p * (tile + 1), V7X_LANES), F32)] * 4)
    return pl.pallas_call(
        functools.partial(_moe_kernel, n_exp=n_exp, tile=tile),
        grid_spec=grid_spec,
        out_shape=jax.ShapeDtypeStruct((n_sb, out_slabs, p, V7X_LANES), F32),
        compiler_params=_cparams(2),
        name="moe_experts",
    )(cnt, off, asg_list, w_list, h, w13, w2)


def _residual_kernel(x_ref, y_ref, g2_ref, fg_ref, o_ref, *, final_norm):
    tm = x_ref.shape[0]
    y = jnp.concatenate(
        [sum(y_ref[0, pl.ds(k, tm, stride=TOP_K), c, :] for k in range(TOP_K))
         for c in range(y_ref.shape[2])], axis=-1)
    x = x_ref[...] + g2_ref[0] * y
    if final_norm:
        x = x * lax.rsqrt(jnp.mean(x * x, axis=-1, keepdims=True) + RMS_EPS) * fg_ref[...]
    o_ref[...] = x


def _residual(x, y, mods, seq_len, n_block, final_g, final_norm):
    t, d = x.shape
    tm = _row_tile(min(seq_len, n_block), 1024)
    per_sb = n_block // tm
    return pl.pallas_call(
        functools.partial(_residual_kernel, final_norm=final_norm),
        grid=(t // tm,),
        in_specs=[_row_spec(tm, d),
                  pl.BlockSpec((1, tm * TOP_K) + y.shape[2:], lambda i: (i // per_sb, i % per_sb, 0, 0)),
                  _mod_spec(mods.shape[0], tm, seq_len, d, 5), _full_spec((1, d))],
        out_specs=_row_spec(tm, d),
        out_shape=jax.ShapeDtypeStruct((t, d), F32),
        compiler_params=_cparams(1),
        name="moe_residual",
    )(x, y, mods, final_g.reshape(1, d))


def _moe_block(t):
    n = min(t, 2048)
    while t % n:
        n //= 2
    return n


def _moe(x, mods, seq_len, norm_g, wr, br, w13, w2, n_groups, final_g, final_norm):
    n_exp = w13.shape[0]
    n_block = _moe_block(x.shape[0])
    h, route = _router(x, mods, seq_len, norm_g, wr, br, n_groups, n_exp // n_groups)
    y = _moe_experts(h, route, w13, w2, n_block)
    return _residual(x, y, mods, seq_len, n_block, final_g, final_norm)


def kernel(x, c, ctx, c_ctx, ada_w, ada_b, norm1_g, norm2_g, conf_w_in, conf_dw, conf_dw_b, conf_ln_g, conf_ln_b, conf_w_out, sc_w_in, sc_conv, sc_w_out, s5_a_re, s5_a_im, s5_log_dt, s5_b_re, s5_b_im, s5_c_re, s5_c_im, s5_d, s5_w_glu, moe_wg, moe_bg, moe_we, moe_be, moe_w13, moe_w2, final_g):
    b, s, d = x.shape
    lc = ctx.shape[1]
    depth = ada_w.shape[0]
    n_groups = moe_wg.shape[-1]
    n_exp = moe_we.shape[-1]
    assert s % GRID_W == 0

    rows = (b + 1 + V7X_SUBLANES - 1) // V7X_SUBLANES * V7X_SUBLANES
    cin = jnp.zeros((rows, d), F32).at[:b].set(c).at[b].set(c_ctx)
    table = _ada_table(cin, ada_w, ada_b)

    xs = x.reshape(b * s, d)
    cs = ctx.reshape(b * lc, d)
    for i in range(depth):
        kind, j = i % N_MIXERS, i // N_MIXERS
        update_ctx = i < depth - 1
        mods_x = table[i, :b].reshape(b, 1, 6 * d)
        mods_c = table[i, b].reshape(1, 1, 6 * d)
        if kind == 0:
            args = (norm1_g[i], conf_w_in[j].astype(BF16), conf_dw[j], conf_dw_b[j],
                    conf_ln_g[j], conf_ln_b[j], conf_w_out[j].astype(BF16))
            xs = _conformer(xs, mods_x, s, GRID_W, *args)
            if update_ctx:
                cs = _conformer(cs, mods_c, lc, 1, *args)
        elif kind == 1:
            args = (norm1_g[i], sc_w_in[j].astype(BF16), sc_conv[j], sc_w_out[j].astype(BF16))
            xs = _short_conv(xs, mods_x, s, GRID_W, *args)
            if update_ctx:
                cs = _short_conv(cs, mods_c, lc, lc, *args)
        else:
            xs, cs_new = _s5_mixer(xs, cs, mods_x, mods_c, b, norm1_g[i], s5_a_re[j], s5_a_im[j],
                                   s5_log_dt[j], s5_b_re[j], s5_b_im[j], s5_c_re[j], s5_c_im[j],
                                   s5_d[j], s5_w_glu[j].astype(BF16))
            if update_ctx:
                cs = cs_new

        n_logit = -(-(n_groups + n_exp) // V7X_SUBLANES) * V7X_SUBLANES
        wr = jnp.zeros((n_logit, d), F32).at[:n_groups].set(moe_wg[i].T)
        wr = wr.at[n_groups:n_groups + n_exp].set(moe_we[i].T)
        br = jnp.zeros((n_logit, 1), F32).at[:n_groups, 0].set(moe_bg[i])
        br = br.at[n_groups:n_groups + n_exp, 0].set(moe_be[i])
        moe_args = (norm2_g[i], wr, br, moe_w13[i].astype(BF16), moe_w2[i].astype(BF16), n_groups,
                    final_g)
        xs = _moe(xs, mods_x, s, *moe_args, final_norm=(i == depth - 1))
        if update_ctx:
            cs = _moe(cs, mods_c, lc, *moe_args, final_norm=False)
    return xs.reshape(b, s, d)
```

```python
import functools

import jax
import jax.numpy as jnp
from jax import lax
from jax.experimental import pallas as pl
from jax.experimental.pallas import tpu as pltpu

F32 = jnp.float32
BF16 = jnp.bfloat16
I32 = jnp.int32
U32 = jnp.uint32
HIGHEST = lax.Precision.HIGHEST

GRID_W = 64
N_MIXERS = 3
TOP_K = 2
TOP_K_SHIFT = TOP_K.bit_length() - 1
assert TOP_K == 1 << TOP_K_SHIFT
RMS_EPS = 1e-6
LN_EPS = 1e-5
S5_DT_FLOOR = -1e-4

V7X_VMEM_BYTES = 64 * 1024 * 1024
V7X_LANES = 128
V7X_SUBLANES = 8
VMEM_LIMIT_BYTES = V7X_VMEM_BYTES - 8 * 1024 * 1024

S5_CHUNK = 16
NEG_BIG = -1e30


def _cparams(n_axes):
    return pltpu.CompilerParams(dimension_semantics=("arbitrary",) * n_axes,
                                vmem_limit_bytes=VMEM_LIMIT_BYTES)


def _row_tile(seq_len, want):
    t = min(seq_len, want)
    while seq_len % t or t % V7X_SUBLANES:
        t -= 1
    return t


def _modnorm(x, g, sc, sh):
    y = x * lax.rsqrt(jnp.mean(x * x, axis=-1, keepdims=True) + RMS_EPS)
    return (y * g) * (1.0 + sc) + sh


def _silu(v):
    return v * jax.nn.sigmoid(v)


def _mod_spec(n_mod, tm, seq_len, d, j):
    if n_mod == 1:
        return pl.BlockSpec((1, 1, d), lambda t: (0, 0, j))
    return pl.BlockSpec((1, 1, d), lambda t: ((t * tm) // seq_len, 0, j))


def _row_spec(tm, d):
    return pl.BlockSpec((tm, d), lambda t: (t, 0))


def _full_spec(shape):
    nd = len(shape)
    return pl.BlockSpec(shape, lambda *_: (0,) * nd)


def _ada_kernel(c_ref, w_ref, b_ref, o_ref):
    o_ref[0] = jnp.dot(_silu(c_ref[...]), w_ref[0], preferred_element_type=F32,
                       precision=HIGHEST) + b_ref[0]


def _ada_table(cin, ada_w, ada_b):
    depth, d, d6 = ada_w.shape
    r = cin.shape[0]
    tn = d6 // 6
    return pl.pallas_call(
        _ada_kernel,
        grid=(depth, d6 // tn),
        in_specs=[pl.BlockSpec((r, d), lambda i, j: (0, 0)),
                  pl.BlockSpec((1, d, tn), lambda i, j: (i, 0, j)),
                  pl.BlockSpec((1, 1, tn), lambda i, j: (i, 0, j))],
        out_specs=pl.BlockSpec((1, r, tn), lambda i, j: (i, 0, j)),
        out_shape=jax.ShapeDtypeStruct((depth, r, d6), F32),
        compiler_params=_cparams(2),
        name="ada_table",
    )(cin, ada_w, ada_b.reshape(depth, 1, d6))


def _conf_in_kernel(x_ref, g_ref, sc_ref, sh_ref, w_ref, z_ref):
    h = _modnorm(x_ref[...], g_ref[...], sc_ref[0], sh_ref[0])
    y = jnp.dot(h.astype(BF16), w_ref[...], preferred_element_type=F32)
    ci = y.shape[1] // 2
    z_ref[...] = (y[:, :ci] * jax.nn.sigmoid(y[:, ci:])).astype(z_ref.dtype)


def _conf_in(x, mods, seq_len, norm_g, w_in):
    t, d = x.shape
    tm = _row_tile(seq_len, 512)
    ci = w_in.shape[1] // 2
    return pl.pallas_call(
        _conf_in_kernel,
        grid=(t // tm,),
        in_specs=[_row_spec(tm, d), _full_spec((1, d)),
                  _mod_spec(mods.shape[0], tm, seq_len, d, 1),
                  _mod_spec(mods.shape[0], tm, seq_len, d, 0),
                  _full_spec(w_in.shape)],
        out_specs=_row_spec(tm, ci),
        out_shape=jax.ShapeDtypeStruct((t, ci), BF16),
        compiler_params=_cparams(1),
        name="conf_in",
    )(x, norm_g.reshape(1, d), mods, mods, w_in)


def _dwconv_kernel(z_ref, w_ref, b_ref, o_ref, src_scr, *, stride, chunk):
    seq_len, cb = z_ref.shape
    taps = w_ref.shape[0]
    half = taps // 2
    aligned = stride % chunk == 0
    pad = 0 if aligned else half * stride
    if pad:
        src_scr[pl.ds(0, pad), :] = jnp.zeros((pad, cb), F32)
        src_scr[pl.ds(pad + seq_len, pad), :] = jnp.zeros((pad, cb), F32)
    src_scr[pl.ds(pad, seq_len), :] = z_ref[...].astype(F32)
    w = w_ref[...]
    bias = b_ref[...]
    for r0 in range(0, seq_len, chunk):
        acc = jnp.broadcast_to(bias, (chunk, cb))
        for k in range(taps):
            lo = r0 + (k - half) * stride
            if aligned and (lo < 0 or lo + chunk > seq_len):
                continue
            acc = acc + w[k:k + 1, :] * src_scr[pl.ds(lo + pad, chunk), :]
        o_ref[pl.ds(r0, chunk), :] = acc.astype(o_ref.dtype)


def _dwconv(z, seq_len, stride, w, b):
    t, c = z.shape
    taps = w.shape[0]
    cb = min(c, 2 * V7X_LANES)
    chunk = _row_tile(seq_len, 64)
    pad = 0 if stride % chunk == 0 else (taps // 2) * stride
    return pl.pallas_call(
        functools.partial(_dwconv_kernel, stride=stride, chunk=chunk),
        grid=(t // seq_len, c // cb),
        in_specs=[pl.BlockSpec((seq_len, cb), lambda s, j: (s, j)),
                  pl.BlockSpec((taps, cb), lambda s, j: (0, j)),
                  pl.BlockSpec((1, cb), lambda s, j: (0, j))],
        out_specs=pl.BlockSpec((seq_len, cb), lambda s, j: (s, j)),
        out_shape=jax.ShapeDtypeStruct((t, c), BF16),
        scratch_shapes=[pltpu.VMEM((seq_len + 2 * pad, cb), F32)],
        compiler_params=_cparams(2),
        name="dwconv",
    )(z, w, b.reshape(1, c))


def _conf_out_kernel(z_ref, lg_ref, lb_ref, w_ref, x_ref, g1_ref, o_ref):
    z = z_ref[...].astype(F32)
    mu = jnp.mean(z, axis=-1, keepdims=True)
    zc = z - mu
    var = jnp.mean(zc * zc, axis=-1, keepdims=True)
    y = zc * lax.rsqrt(var + LN_EPS) * lg_ref[...] + lb_ref[...]
    m = jnp.dot(_silu(y).astype(BF16), w_ref[...], preferred_element_type=F32)
    o_ref[...] = x_ref[...] + g1_ref[0] * m


def _conf_out(z, x, mods, seq_len, ln_g, ln_b, w_out):
    t, d = x.shape
    ci = z.shape[1]
    tm = _row_tile(seq_len, 512)
    return pl.pallas_call(
        _conf_out_kernel,
        grid=(t // tm,),
        in_specs=[_row_spec(tm, ci), _full_spec((1, ci)), _full_spec((1, ci)),
                  _full_spec(w_out.shape), _row_spec(tm, d),
                  _mod_spec(mods.shape[0], tm, seq_len, d, 2)],
        out_specs=_row_spec(tm, d),
        out_shape=jax.ShapeDtypeStruct((t, d), F32),
        compiler_params=_cparams(1),
        name="conf_out",
    )(z, ln_g.reshape(1, ci), ln_b.reshape(1, ci), w_out, x, mods)


def _conformer(x, mods, seq_len, stride, norm_g, w_in, dw, dw_b, ln_g, ln_b, w_out):
    z = _conf_in(x, mods, seq_len, norm_g, w_in)
    z = _dwconv(z, seq_len, stride, dw, dw_b)
    return _conf_out(z, x, mods, seq_len, ln_g, ln_b, w_out)


def _sc_kernel(x_ref, g_ref, sc_ref, sh_ref, win_ref, cw_ref, wout_ref, g1_ref, o_ref, *, period):
    x = x_ref[...]
    tm, d = x.shape
    h = _modnorm(x, g_ref[...], sc_ref[0], sh_ref[0])
    y = jnp.dot(h.astype(BF16), win_ref[...], preferred_element_type=F32)
    gb, gc, v = y[:, :d], y[:, d:2 * d], y[:, 2 * d:]
    u = gc * v
    pos = lax.broadcasted_iota(I32, (tm, 1), 0) % period
    u_prev = jnp.where(pos == 0, 0.0, pltpu.roll(u, 1, 0))
    u_next = jnp.where(pos == period - 1, 0.0, pltpu.roll(u, tm - 1, 0))
    cw = cw_ref[...]
    conv = cw[0:1, :] * u_prev + cw[1:2, :] * u + cw[2:3, :] * u_next
    m = jnp.dot((gb * conv).astype(BF16), wout_ref[...], preferred_element_type=F32)
    o_ref[...] = x + g1_ref[0] * m


def _short_conv(x, mods, seq_len, period, norm_g, w_in, conv_w, w_out):
    t, d = x.shape
    tm = _row_tile(seq_len, 512)
    assert tm % period == 0 and conv_w.shape[0] == 3
    return pl.pallas_call(
        functools.partial(_sc_kernel, period=period),
        grid=(t // tm,),
        in_specs=[_row_spec(tm, d), _full_spec((1, d)),
                  _mod_spec(mods.shape[0], tm, seq_len, d, 1),
                  _mod_spec(mods.shape[0], tm, seq_len, d, 0),
                  _full_spec(w_in.shape), _full_spec(conv_w.shape), _full_spec(w_out.shape),
                  _mod_spec(mods.shape[0], tm, seq_len, d, 2)],
        out_specs=_row_spec(tm, d),
        out_shape=jax.ShapeDtypeStruct((t, d), F32),
        compiler_params=_cparams(1),
        name="short_conv",
    )(x, norm_g.reshape(1, d), mods, mods, w_in, conv_w, w_out, mods)


def _s5_prep_kernel(are_ref, aim_ref, ldt_ref, bre_ref, bim_ref, cre_ref, cim_ref,
                    a16r_ref, a16i_ref, winr_ref, wini_ref, woutr_ref, wouti_ref, k_ref, *, reverse):
    a_re = jnp.minimum(are_ref[0], S5_DT_FLOOR)
    a_im = aim_ref[0]
    dt = jnp.exp(ldt_ref[0])
    b_re, b_im = bre_ref[0], bim_ref[0]
    c_re, c_im = cre_ref[0], cim_ref[0]
    cg = b_re.shape[1]
    tc = S5_CHUNK

    def power(n):
        mag = jnp.exp((n * dt) * a_re)
        ang = (n * dt) * a_im
        return mag * jnp.cos(ang), mag * jnp.sin(ang)

    abar_re, abar_im = power(1)
    den = a_re * a_re + a_im * a_im
    n_re = abar_re - 1.0
    n_im = abar_im
    k_re = (n_re * a_re + n_im * a_im) / den
    k_im = (n_im * a_re - n_re * a_im) / den
    bb_re = k_re * b_re - k_im * b_im
    bb_im = k_re * b_im + k_im * b_re

    e16r, e16i = power(tc)
    a16r_ref[0] = e16r
    a16i_ref[0] = e16i
    for t in range(tc):
        er, ei = power(t if reverse else tc - 1 - t)
        winr_ref[0, :, t * cg:(t + 1) * cg, :] = er * bb_re - ei * bb_im
        wini_ref[0, :, t * cg:(t + 1) * cg, :] = er * bb_im + ei * bb_re
        er, ei = power(tc - t if reverse else t + 1)
        woutr_ref[0, :, t * cg:(t + 1) * cg, :] = c_re * er - c_im * ei
        wouti_ref[0, :, t * cg:(t + 1) * cg, :] = -(c_re * ei + c_im * er)
        er, ei = power(t)
        m_re = c_re * er - c_im * ei
        m_im = c_re * ei + c_im * er
        k_ref[0, :, t * cg:(t + 1) * cg, :] = (
            jnp.einsum('gap,gbp->gab', m_re, bb_re, preferred_element_type=F32, precision=HIGHEST)
            - jnp.einsum('gap,gbp->gab', m_im, bb_im, preferred_element_type=F32, precision=HIGHEST))


def _s5_prep(a_re, a_im, log_dt, b_re, b_im, c_re, c_im, reverse):
    g, p = a_re.shape
    cg = b_re.shape[2]
    gb = min(g, 8)
    rows = S5_CHUNK * cg
    a4 = lambda a: a.reshape(1, g, 1, p)
    bt = lambda b: jnp.swapaxes(b, 1, 2).reshape(1, g, cg, p)
    spec_a = pl.BlockSpec((1, gb, 1, p), lambda i: (0, i, 0, 0))
    spec_b = pl.BlockSpec((1, gb, cg, p), lambda i: (0, i, 0, 0))
    spec_w = pl.BlockSpec((1, gb, rows, p), lambda i: (0, i, 0, 0))
    outs = pl.pallas_call(
        functools.partial(_s5_prep_kernel, reverse=reverse),
        grid=(g // gb,),
        in_specs=[spec_a, spec_a, pl.BlockSpec((1, gb, 1, 1), lambda i: (0, i, 0, 0)),
                  spec_b, spec_b, spec_b, spec_b],
        out_specs=[spec_a, spec_a, spec_w, spec_w, spec_w, spec_w,
                   pl.BlockSpec((1, gb, rows, cg), lambda i: (0, i, 0, 0))],
        out_shape=[jax.ShapeDtypeStruct((1, g, 1, p), F32)] * 2
        + [jax.ShapeDtypeStruct((1, g, rows, p), F32)] * 4
        + [jax.ShapeDtypeStruct((1, g, rows, cg), F32)],
        compiler_params=_cparams(1),
        name="s5_prep",
    )(a4(a_re), a4(a_im), log_dt.reshape(1, g, 1, 1), bt(b_re), bt(b_im),
      c_re.reshape(1, g, cg, p), c_im.reshape(1, g, cg, p))
    a16r, a16i, winr, wini, woutr, wouti, kk = [o[0] for o in outs]
    return a16r, a16i, winr, wini, woutr, wouti, kk


def _pair_blockdiag(w):
    g, r, p = w.shape
    w = w.reshape(g // 2, 2, r, p)
    z = jnp.zeros_like(w[:, 0])
    top = jnp.concatenate([w[:, 0], z], axis=-1)
    bot = jnp.concatenate([z, w[:, 1]], axis=-1)
    return jnp.concatenate([top, bot], axis=1)


def _toeplitz(kk, cg, reverse):
    g = kk.shape[0]
    tc = S5_CHUNK
    k5 = kk.reshape(g, tc, cg, cg)
    t_in = jnp.arange(tc)[None, :]
    t_out = jnp.arange(tc)[:, None]
    lag = (t_in - t_out) if reverse else (t_out - t_in)
    blocks = jnp.where((lag >= 0)[None, :, :, None, None], k5[:, jnp.clip(lag, 0, tc - 1)], 0.0)
    return blocks.transpose(0, 1, 3, 2, 4).reshape(g, tc * cg, tc * cg)


def _s5_core_kernel(vc_ref, vx_ref, a16r_ref, a16i_ref, winr_ref, wini_ref, woutr_ref, wouti_ref,
                    tz_ref, yc_ref, yx_ref, sre, sim, hre, him, *, n_seq):
    r = vc_ref.shape[1]
    n_cc = vc_ref.shape[2]
    n_col = n_cc + vx_ref.shape[2]
    n_chunks = n_col // n_seq
    n_ctx_chunks = n_cc // n_seq
    v = jnp.concatenate([jnp.concatenate([vc_ref[half], vx_ref[half]], axis=-1)
                         for half in range(2)], axis=0)
    for direction in range(2):
        sre[...] = lax.dot_general(v, winr_ref[direction, 0], (((0,), (0,)), ((), ())),
                                   preferred_element_type=F32)
        sim[...] = lax.dot_general(v, wini_ref[direction, 0], (((0,), (0,)), ((), ())),
                                   preferred_element_type=F32)
        ar = a16r_ref[direction, 0]
        ai = a16i_ref[direction, 0]

        def step(k, carry):
            h_r, h_i = carry
            rows = pl.ds(pl.multiple_of(k * n_seq, n_seq), n_seq)
            hre[rows, :] = h_r
            him[rows, :] = h_i
            return (ar * h_r - ai * h_i + sre[rows, :], ar * h_i + ai * h_r + sim[rows, :])

        zero = jnp.zeros((n_seq, sre.shape[1]), F32)
        if direction == 0:
            lax.fori_loop(0, n_chunks, step, (zero, zero))
        else:
            mid = lax.fori_loop(0, n_ctx_chunks,
                                lambda i, c: step(n_ctx_chunks - 1 - i, c), (zero, zero))
            lax.fori_loop(0, n_chunks - n_ctx_chunks,
                          lambda i, c: step(n_chunks - 1 - i, c), mid)

        h_r = hre[...].astype(BF16)
        h_i = him[...].astype(BF16)
        for half in range(2):
            rows = pl.ds(half * r, r)
            part = (lax.dot_general(woutr_ref[direction, 0, rows, :], h_r,
                                    (((1,), (1,)), ((), ())), preferred_element_type=F32)
                    + lax.dot_general(wouti_ref[direction, 0, rows, :], h_i,
                                      (((1,), (1,)), ((), ())), preferred_element_type=F32)
                    + jnp.dot(tz_ref[direction, half], v[half * r:(half + 1) * r, :],
                              preferred_element_type=F32))
            if direction == 0:
                yc_ref[half] = part[:, :n_cc]
                yx_ref[half] = part[:, n_cc:]
            else:
                yc_ref[half] = yc_ref[half] + part[:, :n_cc]
                yx_ref[half] = yx_ref[half] + part[:, n_cc:]


def _s5_core(vc, vx, ops_fwd, ops_bwd, n_seq):
    g, r, n_cc = vc.shape
    n_cx = vx.shape[2]
    cg = r // S5_CHUNK
    stacked = []
    for idx in range(6):
        stacked.append(jnp.stack([ops_fwd[idx], ops_bwd[idx]]))
    a16r, a16i, winr, wini, woutr, wouti = stacked
    p = a16r.shape[-1]
    pair_vec = lambda a: a.reshape(2, g // 2, 1, 2 * p)
    pair_mat = lambda w: jnp.stack([_pair_blockdiag(w[0]), _pair_blockdiag(w[1])]).astype(BF16)
    tz = jnp.stack([_toeplitz(ops_fwd[6], cg, False), _toeplitz(ops_bwd[6], cg, True)]).astype(BF16)
    vec_spec = pl.BlockSpec((2, 1, 1, 2 * p), lambda i: (0, i, 0, 0))
    mat_spec = pl.BlockSpec((2, 1, 2 * r, 2 * p), lambda i: (0, i, 0, 0))
    col_spec = lambda n: pl.BlockSpec((2, r, n), lambda i: (i, 0, 0))
    return pl.pallas_call(
        functools.partial(_s5_core_kernel, n_seq=n_seq),
        grid=(g // 2,),
        in_specs=[col_spec(n_cc), col_spec(n_cx),
                  vec_spec, vec_spec, mat_spec, mat_spec, mat_spec, mat_spec,
                  pl.BlockSpec((2, 2, r, r), lambda i: (0, i, 0, 0))],
        out_specs=[col_spec(n_cc), col_spec(n_cx)],
        out_shape=[jax.ShapeDtypeStruct((g, r, n_cc), F32), jax.ShapeDtypeStruct((g, r, n_cx), F32)],
        scratch_shapes=[pltpu.VMEM((n_cc + n_cx, 2 * p), F32)] * 4,
        compiler_params=_cparams(1),
        name="s5_core",
    )(vc, vx, pair_vec(a16r), pair_vec(a16i), pair_mat(winr), pair_mat(wini),
      pair_mat(woutr), pair_mat(wouti), tz)


def _s5_cols_kernel(x_ref, g_ref, sc_ref, sh_ref, v_ref, row_scr):
    nb, r, d = x_ref.shape
    g = v_ref.shape[0]
    cg = d // g
    tc = v_ref.shape[1] // cg
    nk = r // tc
    x = x_ref[...]
    y = x * lax.rsqrt(jnp.mean(x * x, axis=-1, keepdims=True) + RMS_EPS)
    h = ((y * g_ref[...]) * (1.0 + sc_ref[...]) + sh_ref[...]).reshape(nb * r, d)
    n_lane_blocks = row_scr.shape[0]
    gl = V7X_LANES // cg
    for c in range(n_lane_blocks):
        row_scr[c] = h[:, c * V7X_LANES:(c + 1) * V7X_LANES]
    for t in range(tc):
        for c in range(n_lane_blocks):
            z = jnp.concatenate([row_scr[c, pl.ds(t + tc * k, nb, stride=r), :] for k in range(nk)],
                                axis=0)
            v_ref[c * gl:(c + 1) * gl, t * cg:(t + 1) * cg, :] = (
                z.T.reshape(gl, cg, nk * nb).astype(v_ref.dtype))


def _s5_block_rows(n_seq, seq_len):
    return min(seq_len, S5_CHUNK * max(1, V7X_LANES // n_seq))


def _s5_cols(x, mods, n_seq, seq_len, norm_g, n_groups):
    d = x.shape[1]
    r = _s5_block_rows(n_seq, seq_len)
    cols = (r // S5_CHUNK) * n_seq
    rows = S5_CHUNK * (d // n_groups)
    n_mod = mods.shape[0]
    mod_spec = lambda j: pl.BlockSpec((n_mod, 1, d), lambda i: (0, 0, j))
    return pl.pallas_call(
        _s5_cols_kernel,
        grid=(seq_len // r,),
        in_specs=[pl.BlockSpec((n_seq, r, d), lambda i: (0, i, 0)), _full_spec((1, d)),
                  mod_spec(1), mod_spec(0)],
        out_specs=pl.BlockSpec((n_groups, rows, cols), lambda i: (0, 0, i)),
        out_shape=jax.ShapeDtypeStruct((n_groups, rows, (seq_len // S5_CHUNK) * n_seq), BF16),
        scratch_shapes=[pltpu.VMEM((d // V7X_LANES, n_seq * r, V7X_LANES), F32)],
        compiler_params=_cparams(1),
        name="s5_cols",
    )(x.reshape(n_seq, seq_len, d), norm_g.reshape(1, d), mods, mods)


def _s5_rows_kernel(y_ref, o_ref, row_scr):
    nb, r, d = o_ref.shape
    g = y_ref.shape[0]
    cg = d // g
    tc = y_ref.shape[1] // cg
    nk = r // tc
    n_lane_blocks = row_scr.shape[0]
    gl = V7X_LANES // cg
    for t in range(tc):
        for c in range(n_lane_blocks):
            z = y_ref[c * gl:(c + 1) * gl, t * cg:(t + 1) * cg, :].reshape(V7X_LANES, nk * nb).T
            for k in range(nk):
                row_scr[c, pl.ds(t + tc * k, nb, stride=r), :] = z[k * nb:(k + 1) * nb, :]
    o_ref[...] = jnp.concatenate([row_scr[c] for c in range(n_lane_blocks)],
                                 axis=-1).reshape(nb, r, d)


def _s5_rows(y, n_seq, seq_len):
    n_groups, rows, _ = y.shape
    d = n_groups * (rows // S5_CHUNK)
    r = _s5_block_rows(n_seq, seq_len)
    cols = (r // S5_CHUNK) * n_seq
    out = pl.pallas_call(
        _s5_rows_kernel,
        grid=(seq_len // r,),
        in_specs=[pl.BlockSpec((n_groups, rows, cols), lambda i: (0, 0, i))],
        out_specs=pl.BlockSpec((n_seq, r, d), lambda i: (0, i, 0)),
        out_shape=jax.ShapeDtypeStruct((n_seq, seq_len, d), F32),
        scratch_shapes=[pltpu.VMEM((d // V7X_LANES, n_seq * r, V7X_LANES), F32)],
        compiler_params=_cparams(1),
        name="s5_rows",
    )(y)
    return out.reshape(n_seq * seq_len, d)


def _s5_head_kernel(x_ref, y_ref, g_ref, sc_ref, sh_ref, d_ref, w_ref, g1_ref, o_ref):
    x = x_ref[...]
    h = _modnorm(x, g_ref[...], sc_ref[0], sh_ref[0])
    a = jax.nn.gelu(y_ref[...] + d_ref[...] * h, approximate=True)
    z = jnp.dot(a.astype(BF16), w_ref[...], preferred_element_type=F32)
    dm = z.shape[1] // 2
    o_ref[...] = x + g1_ref[0] * (z[:, :dm] * jax.nn.sigmoid(z[:, dm:]))


def _s5_head(x, y, mods, seq_len, norm_g, d_skip, w_glu):
    t, d = x.shape
    tm = _row_tile(seq_len, 512)
    return pl.pallas_call(
        _s5_head_kernel,
        grid=(t // tm,),
        in_specs=[_row_spec(tm, d), _row_spec(tm, d), _full_spec((1, d)),
                  _mod_spec(mods.shape[0], tm, seq_len, d, 1),
                  _mod_spec(mods.shape[0], tm, seq_len, d, 0),
                  _full_spec((1, d)), _full_spec(w_glu.shape),
                  _mod_spec(mods.shape[0], tm, seq_len, d, 2)],
        out_specs=_row_spec(tm, d),
        out_shape=jax.ShapeDtypeStruct((t, d), F32),
        compiler_params=_cparams(1),
        name="s5_head",
    )(x, y, norm_g.reshape(1, d), mods, mods, d_skip.reshape(1, d), w_glu, mods)


def _s5_mixer(x, ctx, mods_x, mods_c, n_seq, norm_g, a_re, a_im, log_dt, b_re, b_im, c_re, c_im,
              d_skip, w_glu):
    s_len, c_len = x.shape[0] // n_seq, ctx.shape[0] // n_seq
    g = a_re.shape[1]
    ops = [_s5_prep(a_re[k], a_im[k], log_dt[k], b_re[k], b_im[k], c_re[k], c_im[k], bool(k))
           for k in range(2)]
    vc = _s5_cols(ctx, mods_c, n_seq, c_len, norm_g, g)
    vx = _s5_cols(x, mods_x, n_seq, s_len, norm_g, g)
    y_c, y_x = _s5_core(vc, vx, ops[0], ops[1], n_seq)
    x = _s5_head(x, _s5_rows(y_x, n_seq, s_len), mods_x, s_len, norm_g, d_skip, w_glu)
    ctx = _s5_head(ctx, _s5_rows(y_c, n_seq, c_len), mods_c, c_len, norm_g, d_skip, w_glu)
    return x, ctx


def _pack_pairs(lo, hi):
    lo_bits = lax.bitcast_convert_type(lo.astype(BF16).astype(F32), U32)
    hi_bits = lax.bitcast_convert_type(hi.astype(BF16).astype(F32), U32)
    return lax.shift_right_logical(lo_bits, jnp.uint32(16)) | (hi_bits & jnp.uint32(0xFFFF0000))


def _unpack_pairs(words):
    lo = lax.bitcast_convert_type(lax.shift_left(words, jnp.uint32(16)), F32)
    hi = lax.bitcast_convert_type(words & jnp.uint32(0xFFFF0000), F32)
    return lo, hi


def _pack_row_chunks(rows, pp):
    lanes = V7X_LANES
    return [_pack_pairs(rows[:, c * lanes:(c + 1) * lanes], rows[:, (c + pp) * lanes:(c + pp + 1) * lanes])
            for c in range(pp)]


def _unpack_row_chunks(chunks):
    pairs = [_unpack_pairs(w) for w in chunks]
    return jnp.concatenate([lo for lo, _ in pairs] + [hi for _, hi in pairs], axis=-1)


def _router_kernel(x_ref, g_ref, sc_ref, sh_ref, wr_ref, br_ref, h_ref, r_ref, *, n_groups, epg):
    h = _modnorm(x_ref[...], g_ref[...], sc_ref[0], sh_ref[0])
    tm = x_ref.shape[0]
    pp = h_ref.shape[0] // tm
    for c, words in enumerate(_pack_row_chunks(h, pp)):
        h_ref[pl.ds(c, tm, stride=pp), :] = words
    logits = lax.dot_general(wr_ref[...], h, (((1,), (1,)), ((), ())), preferred_element_type=F32,
                             precision=HIGHEST) + br_ref[...]
    row = lax.broadcasted_iota(I32, logits.shape, 0)
    far = jnp.int32(1 << 20)

    def first_max(vals):
        m = jnp.max(vals, axis=0, keepdims=True)
        return m, jnp.min(jnp.where(vals == m, row, far), axis=0, keepdims=True)

    is_group = row < n_groups
    gl = jnp.where(is_group, logits, NEG_BIG)
    gmax, gidx = first_max(gl)
    gsum = jnp.sum(jnp.where(is_group, jnp.exp(gl - gmax), 0.0), axis=0, keepdims=True)
    g_w = 1.0 / gsum
    lo = n_groups + gidx * epg
    le = jnp.where((row >= lo) & (row < lo + epg), logits, NEG_BIG)
    m1, i1 = first_max(le)
    m2, i2 = first_max(jnp.where(row == i1, NEG_BIG, le))
    ratio = jnp.exp(m2 - m1)
    w1 = g_w / (1.0 + ratio)
    w2 = g_w * ratio / (1.0 + ratio)
    e1 = (i1 - n_groups).astype(F32)
    e2 = (i2 - n_groups).astype(F32)
    out_row = lax.broadcasted_iota(I32, r_ref.shape, 0)
    r_ref[...] = jnp.where(out_row == 0, e1, jnp.where(out_row == 1, e2, jnp.where(
        out_row == 2, w1, jnp.where(out_row == 3, w2, 0.0))))


def _router(x, mods, seq_len, norm_g, wr, br, n_groups, epg):
    t, d = x.shape
    tm = _row_tile(seq_len, 512)
    pp = d // (2 * V7X_LANES)
    return pl.pallas_call(
        functools.partial(_router_kernel, n_groups=n_groups, epg=epg),
        grid=(t // tm,),
        in_specs=[_row_spec(tm, d), _full_spec((1, d)),
                  _mod_spec(mods.shape[0], tm, seq_len, d, 4),
                  _mod_spec(mods.shape[0], tm, seq_len, d, 3),
                  _full_spec(wr.shape), _full_spec(br.shape)],
        out_specs=[pl.BlockSpec((tm * pp, V7X_LANES), lambda i: (i, 0)),
                   pl.BlockSpec((V7X_SUBLANES, tm), lambda i: (0, i))],
        out_shape=[jax.ShapeDtypeStruct((t * pp, V7X_LANES), U32),
                   jax.ShapeDtypeStruct((V7X_SUBLANES, t), F32)],
        compiler_params=_cparams(1),
        name="moe_router",
    )(x, norm_g.reshape(1, d), mods, mods, wr, br)


def _dispatch_lists(route, n_block, n_exp, cap):
    t = route.shape[1]
    n_sb = t // n_block
    n_assign = n_block * TOP_K
    ids = route[0:TOP_K].T.astype(I32).reshape(n_sb, n_assign)
    asg = jnp.broadcast_to(jnp.arange(n_assign, dtype=I32)[None, :], (n_sb, n_assign))
    _, asg_s = lax.sort((ids, asg), dimension=1, is_stable=True, num_keys=1)
    counts = jnp.sum((ids[:, :, None] == jnp.arange(n_exp, dtype=I32)).astype(I32), axis=1)
    offs = jnp.cumsum(counts, axis=1) - counts
    pad = ((0, 0), (0, cap - n_assign))
    return (counts.reshape(-1), offs.reshape(-1),
            jnp.pad(asg_s, pad, constant_values=n_assign).reshape(n_sb, 1, cap))


def _moe_kernel(cnt_ref, off_ref, asg_ref, h_ref, w13_ref, w2_ref, o_ref,
                lhs_a, lhs_b, ys_a, ys_b, *, n_exp, tile):
    sb = pl.program_id(0)
    e = pl.program_id(1)
    pp = w13_ref.shape[1] // (2 * V7X_LANES)
    n_tok = h_ref.shape[0] // pp
    de = w2_ref.shape[1]
    count = cnt_ref[sb * n_exp + e]
    seg = off_ref[sb * n_exp + e]
    n_out = o_ref.shape[1] // pp
    scratch = ((lhs_a, ys_a), (lhs_b, ys_b))

    def slab(index):
        return pl.ds(pl.multiple_of(index * pp, pp), pp)

    @pl.when(e == 0)
    def _():
        spare = n_out - n_tok * TOP_K
        o_ref[0, pl.ds(n_tok * TOP_K * pp, spare * pp), :] = jnp.zeros((spare * pp, V7X_LANES), U32)

    def run_tiles(base, sizes):
        starts = [base + sum(sizes[:k]) for k in range(len(sizes))]
        for (lhs_scr, _), start, size in zip(scratch, starts, sizes):
            stride = size + 1
            for mi in range(size):
                tok = jnp.minimum(lax.shift_right_logical(asg_ref[0, 0, start + mi], TOP_K_SHIFT),
                                  n_tok - 1)
                lhs_scr[pl.ds(mi, pp, stride=stride), :] = h_ref[slab(tok), :]
        for (lhs_scr, ys_scr), size in zip(scratch, sizes):
            stride = size + 1
            lhs = _unpack_row_chunks([lhs_scr[pl.ds(c * stride, size), :] for c in range(pp)])
            hid = jnp.dot(lhs.astype(BF16), w13_ref[0], preferred_element_type=F32)
            act = _silu(hid[:, :de]) * hid[:, de:]
            ys = jnp.dot(act.astype(BF16), w2_ref[0], preferred_element_type=F32)
            for c, words in enumerate(_pack_row_chunks(ys, pp)):
                ys_scr[pl.ds(c * stride, size), :] = words
        for (_, ys_scr), start, size in zip(scratch, starts, sizes):
            stride = size + 1
            for mi in range(size):
                o_ref[0, slab(asg_ref[0, 0, start + mi]), :] = ys_scr[pl.ds(mi, pp, stride=stride), :]

    half = tile // 2
    n_pairs = lax.div(count, 2 * tile)

    def pair(j, carry):
        run_tiles(seg + j * 2 * tile, (tile, tile))
        return carry
    lax.fori_loop(0, n_pairs, pair, 0)

    rem = count - n_pairs * 2 * tile
    rem_base = seg + n_pairs * 2 * tile
    for hi, sizes in ((half, (half,)), (tile, (tile,)), (tile + half, (tile, half)),
                      (2 * tile, (tile, tile))):
        @pl.when((rem > hi - half) & (rem <= hi))
        def _():
            run_tiles(rem_base, sizes)


def _moe_tile(n_block, n_exp):
    return max(4 * V7X_SUBLANES, n_block * TOP_K // n_exp)


def _moe_experts(h, route, w13, w2, n_block):
    n_exp, d, de2 = w13.shape
    de = de2 // 2
    pp = d // (2 * V7X_LANES)
    n_sb = h.shape[0] // (n_block * pp)
    n_assign = n_block * TOP_K
    tile = _moe_tile(n_block, n_exp)
    cap = -(-(n_assign + 2 * tile) // V7X_LANES) * V7X_LANES
    cnt, off, asg_list = _dispatch_lists(route, n_block, n_exp, cap)
    out_slabs = n_assign + V7X_SUBLANES
    grid_spec = pltpu.PrefetchScalarGridSpec(
        num_scalar_prefetch=2,
        grid=(n_sb, n_exp),
        in_specs=[pl.BlockSpec((1, 1, cap), lambda s, e, *_: (s, 0, 0), memory_space=pltpu.SMEM),
                  pl.BlockSpec((n_block * pp, V7X_LANES), lambda s, e, *_: (s, 0),
                               pipeline_mode=pl.Buffered(1)),
                  pl.BlockSpec((1, d, de2), lambda s, e, *_: (e, 0, 0)),
                  pl.BlockSpec((1, de, d), lambda s, e, *_: (e, 0, 0))],
        out_specs=pl.BlockSpec((1, out_slabs * pp, V7X_LANES), lambda s, e, *_: (s, 0, 0)),
        scratch_shapes=[pltpu.VMEM((pp * (tile + 1), V7X_LANES), U32)] * 4)
    return pl.pallas_call(
        functools.partial(_moe_kernel, n_exp=n_exp, tile=tile),
        grid_spec=grid_spec,
        out_shape=jax.ShapeDtypeStruct((n_sb, out_slabs * pp, V7X_LANES), U32),
        compiler_params=_cparams(2),
        name="moe_experts",
    )(cnt, off, asg_list, h, w13, w2)


def _residual_kernel(x_ref, y_ref, w_ref, g2_ref, fg_ref, o_ref, *, final_norm):
    tm = x_ref.shape[0]
    wts = w_ref[...]
    pp = y_ref.shape[1] // (tm * TOP_K)
    y = sum(wts[:, k:k + 1]
            * _unpack_row_chunks([y_ref[0, pl.ds(k * pp + c, tm, stride=TOP_K * pp), :]
                                  for c in range(pp)])
            for k in range(TOP_K))
    x = x_ref[...] + g2_ref[0] * y
    if final_norm:
        x = x * lax.rsqrt(jnp.mean(x * x, axis=-1, keepdims=True) + RMS_EPS) * fg_ref[...]
    o_ref[...] = x


def _residual(x, y, wts, mods, seq_len, n_block, final_g, final_norm):
    t, d = x.shape
    tm = _row_tile(min(seq_len, n_block), 512)
    per_sb = n_block // tm
    pp = d // (2 * V7X_LANES)
    return pl.pallas_call(
        functools.partial(_residual_kernel, final_norm=final_norm),
        grid=(t // tm,),
        in_specs=[_row_spec(tm, d),
                  pl.BlockSpec((1, tm * TOP_K * pp, V7X_LANES), lambda i: (i // per_sb, i % per_sb, 0)),
                  _row_spec(tm, TOP_K),
                  _mod_spec(mods.shape[0], tm, seq_len, d, 5), _full_spec((1, d))],
        out_specs=_row_spec(tm, d),
        out_shape=jax.ShapeDtypeStruct((t, d), F32),
        compiler_params=_cparams(1),
        name="moe_residual",
    )(x, y, wts, mods, final_g.reshape(1, d))


def _moe_block(t):
    n = min(t, 4096)
    while t % n:
        n //= 2
    return n


def _moe(x, mods, seq_len, norm_g, wr, br, w13, w2, n_groups, final_g, final_norm):
    n_exp = w13.shape[0]
    n_block = _moe_block(x.shape[0])
    h, route = _router(x, mods, seq_len, norm_g, wr, br, n_groups, n_exp // n_groups)
    y = _moe_experts(h, route, w13, w2, n_block)
    return _residual(x, y, route[TOP_K:2 * TOP_K].T, mods, seq_len, n_block, final_g, final_norm)


def kernel(x, c, ctx, c_ctx, ada_w, ada_b, norm1_g, norm2_g, conf_w_in, conf_dw, conf_dw_b, conf_ln_g, conf_ln_b, conf_w_out, sc_w_in, sc_conv, sc_w_out, s5_a_re, s5_a_im, s5_log_dt, s5_b_re, s5_b_im, s5_c_re, s5_c_im, s5_d, s5_w_glu, moe_wg, moe_bg, moe_we, moe_be, moe_w13, moe_w2, final_g):
    b, s, d = x.shape
    lc = ctx.shape[1]
    depth = ada_w.shape[0]
    n_groups = moe_wg.shape[-1]
    n_exp = moe_we.shape[-1]
    assert s % GRID_W == 0

    rows = (b + 1 + V7X_SUBLANES - 1) // V7X_SUBLANES * V7X_SUBLANES
    cin = jnp.zeros((rows, d), F32).at[:b].set(c).at[b].set(c_ctx)
    table = _ada_table(cin, ada_w, ada_b)

    xs = x.reshape(b * s, d)
    cs = ctx.reshape(b * lc, d)
    for i in range(depth):
        kind, j = i % N_MIXERS, i // N_MIXERS
        update_ctx = i < depth - 1
        mods_x = table[i, :b].reshape(b, 1, 6 * d)
        mods_c = table[i, b].reshape(1, 1, 6 * d)
        if kind == 0:
            args = (norm1_g[i], conf_w_in[j].astype(BF16), conf_dw[j], conf_dw_b[j],
                    conf_ln_g[j], conf_ln_b[j], conf_w_out[j].astype(BF16))
            xs = _conformer(xs, mods_x, s, GRID_W, *args)
            if update_ctx:
                cs = _conformer(cs, mods_c, lc, 1, *args)
        elif kind == 1:
            args = (norm1_g[i], sc_w_in[j].astype(BF16), sc_conv[j], sc_w_out[j].astype(BF16))
            xs = _short_conv(xs, mods_x, s, GRID_W, *args)
            if update_ctx:
                cs = _short_conv(cs, mods_c, lc, lc, *args)
        else:
            xs, cs_new = _s5_mixer(xs, cs, mods_x, mods_c, b, norm1_g[i], s5_a_re[j], s5_a_im[j],
                                   s5_log_dt[j], s5_b_re[j], s5_b_im[j], s5_c_re[j], s5_c_im[j],
                                   s5_d[j], s5_w_glu[j].astype(BF16))
            if update_ctx:
                cs = cs_new

        n_logit = -(-(n_groups + n_exp) // V7X_SUBLANES) * V7X_SUBLANES
        wr = jnp.zeros((n_logit, d), F32).at[:n_groups].set(moe_wg[i].T)
        wr = wr.at[n_groups:n_groups + n_exp].set(moe_we[i].T)
        br = jnp.zeros((n_logit, 1), F32).at[:n_groups, 0].set(moe_bg[i])
        br = br.at[n_groups:n_groups + n_exp, 0].set(moe_be[i])
        moe_args = (norm2_g[i], wr, br, moe_w13[i].astype(BF16), moe_w2[i].astype(BF16), n_groups,
                    final_g)
        xs = _moe(xs, mods_x, s, *moe_args, final_norm=(i == depth - 1))
        if update_ctx:
            cs = _moe(cs, mods_c, lc, *moe_args, final_norm=False)
    return xs.reshape(b, s, d)
```

```python
import functools

import jax
import jax.numpy as jnp
from jax import lax
from jax.experimental import pallas as pl
from jax.experimental.pallas import tpu as pltpu

F32 = jnp.float32
BF16 = jnp.bfloat16
I32 = jnp.int32
U32 = jnp.uint32
HIGHEST = lax.Precision.HIGHEST

GRID_W = 64
N_MIXERS = 3
TOP_K = 2
RMS_EPS = 1e-6
LN_EPS = 1e-5
S5_DT_FLOOR = -1e-4

V7X_VMEM_BYTES = 64 * 1024 * 1024
V7X_LANES = 128
V7X_SUBLANES = 8
VMEM_LIMIT_BYTES = V7X_VMEM_BYTES - 6 * 1024 * 1024

S5_CHUNK = 16
MOE_EXPERTS_PER_STEP = 2
NEG_BIG = -1e30


def _cparams(n_axes):
    return pltpu.CompilerParams(dimension_semantics=("arbitrary",) * n_axes,
                                vmem_limit_bytes=VMEM_LIMIT_BYTES)


def _row_tile(seq_len, want):
    t = min(seq_len, want)
    while seq_len % t or t % V7X_SUBLANES:
        t -= 1
    return t


def _modnorm(x, g, sc, sh):
    y = x * lax.rsqrt(jnp.mean(x * x, axis=-1, keepdims=True) + RMS_EPS)
    return (y * g) * (1.0 + sc) + sh


def _silu(v):
    return v * jax.nn.sigmoid(v)


def _mod_spec(n_mod, tm, seq_len, d, j):
    if n_mod == 1:
        return pl.BlockSpec((1, 1, d), lambda t: (0, 0, j))
    return pl.BlockSpec((1, 1, d), lambda t: ((t * tm) // seq_len, 0, j))


def _row_spec(tm, d):
    return pl.BlockSpec((tm, d), lambda t: (t, 0))


def _full_spec(shape):
    nd = len(shape)
    return pl.BlockSpec(shape, lambda *_: (0,) * nd)


def _ada_kernel(c_ref, w_ref, b_ref, o_ref):
    o_ref[0] = jnp.dot(_silu(c_ref[...]), w_ref[0], preferred_element_type=F32,
                       precision=HIGHEST) + b_ref[0]


def _ada_table(cin, ada_w, ada_b):
    depth, d, d6 = ada_w.shape
    r = cin.shape[0]
    tn = d6 // 6
    return pl.pallas_call(
        _ada_kernel,
        grid=(depth, d6 // tn),
        in_specs=[pl.BlockSpec((r, d), lambda i, j: (0, 0)),
                  pl.BlockSpec((1, d, tn), lambda i, j: (i, 0, j)),
                  pl.BlockSpec((1, 1, tn), lambda i, j: (i, 0, j))],
        out_specs=pl.BlockSpec((1, r, tn), lambda i, j: (i, 0, j)),
        out_shape=jax.ShapeDtypeStruct((depth, r, d6), F32),
        compiler_params=_cparams(2),
        name="ada_table",
    )(cin, ada_w, ada_b.reshape(depth, 1, d6))


def _conf_in_kernel(x_ref, g_ref, sc_ref, sh_ref, w_ref, z_ref):
    h = _modnorm(x_ref[...], g_ref[...], sc_ref[0], sh_ref[0])
    y = jnp.dot(h.astype(BF16), w_ref[...], preferred_element_type=F32)
    ci = y.shape[1] // 2
    z_ref[...] = (y[:, :ci] * jax.nn.sigmoid(y[:, ci:])).astype(z_ref.dtype)


def _conf_in(x, mods, seq_len, norm_g, w_in):
    t, d = x.shape
    tm = _row_tile(seq_len, 512)
    ci = w_in.shape[1] // 2
    return pl.pallas_call(
        _conf_in_kernel,
        grid=(t // tm,),
        in_specs=[_row_spec(tm, d), _full_spec((1, d)),
                  _mod_spec(mods.shape[0], tm, seq_len, d, 1),
                  _mod_spec(mods.shape[0], tm, seq_len, d, 0),
                  _full_spec(w_in.shape)],
        out_specs=_row_spec(tm, ci),
        out_shape=jax.ShapeDtypeStruct((t, ci), BF16),
        compiler_params=_cparams(1),
        name="conf_in",
    )(x, norm_g.reshape(1, d), mods, mods, w_in)


def _dwconv_kernel(z_ref, w_ref, b_ref, o_ref, src_scr, *, stride, chunk):
    seq_len, cb = z_ref.shape
    taps = w_ref.shape[0]
    half = taps // 2
    aligned = stride % chunk == 0
    pad = 0 if aligned else half * stride
    if pad:
        src_scr[pl.ds(0, pad), :] = jnp.zeros((pad, cb), F32)
        src_scr[pl.ds(pad + seq_len, pad), :] = jnp.zeros((pad, cb), F32)
    src_scr[pl.ds(pad, seq_len), :] = z_ref[...].astype(F32)
    w = w_ref[...]
    bias = b_ref[...]
    for r0 in range(0, seq_len, chunk):
        acc = jnp.broadcast_to(bias, (chunk, cb))
        for k in range(taps):
            lo = r0 + (k - half) * stride
            if aligned and (lo < 0 or lo + chunk > seq_len):
                continue
            acc = acc + w[k:k + 1, :] * src_scr[pl.ds(lo + pad, chunk), :]
        o_ref[pl.ds(r0, chunk), :] = acc.astype(o_ref.dtype)


def _dwconv(z, seq_len, stride, w, b):
    t, c = z.shape
    taps = w.shape[0]
    cb = min(c, 2 * V7X_LANES)
    chunk = _row_tile(seq_len, 64)
    pad = 0 if stride % chunk == 0 else (taps // 2) * stride
    return pl.pallas_call(
        functools.partial(_dwconv_kernel, stride=stride, chunk=chunk),
        grid=(t // seq_len, c // cb),
        in_specs=[pl.BlockSpec((seq_len, cb), lambda s, j: (s, j)),
                  pl.BlockSpec((taps, cb), lambda s, j: (0, j)),
                  pl.BlockSpec((1, cb), lambda s, j: (0, j))],
        out_specs=pl.BlockSpec((seq_len, cb), lambda s, j: (s, j)),
        out_shape=jax.ShapeDtypeStruct((t, c), BF16),
        scratch_shapes=[pltpu.VMEM((seq_len + 2 * pad, cb), F32)],
        compiler_params=_cparams(2),
        name="dwconv",
    )(z, w, b.reshape(1, c))


def _conf_out_kernel(z_ref, lg_ref, lb_ref, w_ref, x_ref, g1_ref, o_ref):
    z = z_ref[...].astype(F32)
    mu = jnp.mean(z, axis=-1, keepdims=True)
    zc = z - mu
    var = jnp.mean(zc * zc, axis=-1, keepdims=True)
    y = zc * lax.rsqrt(var + LN_EPS) * lg_ref[...] + lb_ref[...]
    m = jnp.dot(_silu(y).astype(BF16), w_ref[...], preferred_element_type=F32)
    o_ref[...] = x_ref[...] + g1_ref[0] * m


def _conf_out(z, x, mods, seq_len, ln_g, ln_b, w_out):
    t, d = x.shape
    ci = z.shape[1]
    tm = _row_tile(seq_len, 512)
    return pl.pallas_call(
        _conf_out_kernel,
        grid=(t // tm,),
        in_specs=[_row_spec(tm, ci), _full_spec((1, ci)), _full_spec((1, ci)),
                  _full_spec(w_out.shape), _row_spec(tm, d),
                  _mod_spec(mods.shape[0], tm, seq_len, d, 2)],
        out_specs=_row_spec(tm, d),
        out_shape=jax.ShapeDtypeStruct((t, d), F32),
        compiler_params=_cparams(1),
        name="conf_out",
    )(z, ln_g.reshape(1, ci), ln_b.reshape(1, ci), w_out, x, mods)


def _conformer(x, mods, seq_len, stride, norm_g, w_in, dw, dw_b, ln_g, ln_b, w_out):
    z = _conf_in(x, mods, seq_len, norm_g, w_in)
    z = _dwconv(z, seq_len, stride, dw, dw_b)
    return _conf_out(z, x, mods, seq_len, ln_g, ln_b, w_out)


def _sc_kernel(x_ref, g_ref, sc_ref, sh_ref, win_ref, cw_ref, wout_ref, g1_ref, o_ref, *, period):
    x = x_ref[...]
    tm, d = x.shape
    h = _modnorm(x, g_ref[...], sc_ref[0], sh_ref[0])
    y = jnp.dot(h.astype(BF16), win_ref[...], preferred_element_type=F32)
    gb, gc, v = y[:, :d], y[:, d:2 * d], y[:, 2 * d:]
    u = gc * v
    pos = lax.broadcasted_iota(I32, (tm, 1), 0) % period
    u_prev = jnp.where(pos == 0, 0.0, pltpu.roll(u, 1, 0))
    u_next = jnp.where(pos == period - 1, 0.0, pltpu.roll(u, tm - 1, 0))
    cw = cw_ref[...]
    conv = cw[0:1, :] * u_prev + cw[1:2, :] * u + cw[2:3, :] * u_next
    m = jnp.dot((gb * conv).astype(BF16), wout_ref[...], preferred_element_type=F32)
    o_ref[...] = x + g1_ref[0] * m


def _short_conv(x, mods, seq_len, period, norm_g, w_in, conv_w, w_out):
    t, d = x.shape
    tm = _row_tile(seq_len, 512)
    assert tm % period == 0 and conv_w.shape[0] == 3
    return pl.pallas_call(
        functools.partial(_sc_kernel, period=period),
        grid=(t // tm,),
        in_specs=[_row_spec(tm, d), _full_spec((1, d)),
                  _mod_spec(mods.shape[0], tm, seq_len, d, 1),
                  _mod_spec(mods.shape[0], tm, seq_len, d, 0),
                  _full_spec(w_in.shape), _full_spec(conv_w.shape), _full_spec(w_out.shape),
                  _mod_spec(mods.shape[0], tm, seq_len, d, 2)],
        out_specs=_row_spec(tm, d),
        out_shape=jax.ShapeDtypeStruct((t, d), F32),
        compiler_params=_cparams(1),
        name="short_conv",
    )(x, norm_g.reshape(1, d), mods, mods, w_in, conv_w, w_out, mods)


def _s5_prep_kernel(are_ref, aim_ref, ldt_ref, bre_ref, bim_ref, cre_ref, cim_ref,
                    a16r_ref, a16i_ref, winr_ref, wini_ref, woutr_ref, wouti_ref, k_ref, *, reverse):
    a_re = jnp.minimum(are_ref[0], S5_DT_FLOOR)
    a_im = aim_ref[0]
    dt = jnp.exp(ldt_ref[0])
    b_re, b_im = bre_ref[0], bim_ref[0]
    c_re, c_im = cre_ref[0], cim_ref[0]
    cg = b_re.shape[1]
    tc = S5_CHUNK

    def power(n):
        mag = jnp.exp((n * dt) * a_re)
        ang = (n * dt) * a_im
        return mag * jnp.cos(ang), mag * jnp.sin(ang)

    abar_re, abar_im = power(1)
    den = a_re * a_re + a_im * a_im
    n_re = abar_re - 1.0
    n_im = abar_im
    k_re = (n_re * a_re + n_im * a_im) / den
    k_im = (n_im * a_re - n_re * a_im) / den
    bb_re = k_re * b_re - k_im * b_im
    bb_im = k_re * b_im + k_im * b_re

    e16r, e16i = power(tc)
    a16r_ref[0] = e16r
    a16i_ref[0] = e16i
    for t in range(tc):
        er, ei = power(t if reverse else tc - 1 - t)
        winr_ref[0, :, t * cg:(t + 1) * cg, :] = er * bb_re - ei * bb_im
        wini_ref[0, :, t * cg:(t + 1) * cg, :] = er * bb_im + ei * bb_re
        er, ei = power(tc - t if reverse else t + 1)
        woutr_ref[0, :, t * cg:(t + 1) * cg, :] = c_re * er - c_im * ei
        wouti_ref[0, :, t * cg:(t + 1) * cg, :] = -(c_re * ei + c_im * er)
        er, ei = power(t)
        m_re = c_re * er - c_im * ei
        m_im = c_re * ei + c_im * er
        k_ref[0, :, t * cg:(t + 1) * cg, :] = (
            jnp.einsum('gap,gbp->gab', m_re, bb_re, preferred_element_type=F32, precision=HIGHEST)
            - jnp.einsum('gap,gbp->gab', m_im, bb_im, preferred_element_type=F32, precision=HIGHEST))


def _s5_prep(a_re, a_im, log_dt, b_re, b_im, c_re, c_im, reverse):
    g, p = a_re.shape
    cg = b_re.shape[2]
    gb = min(g, 8)
    rows = S5_CHUNK * cg
    a4 = lambda a: a.reshape(1, g, 1, p)
    bt = lambda b: jnp.swapaxes(b, 1, 2).reshape(1, g, cg, p)
    spec_a = pl.BlockSpec((1, gb, 1, p), lambda i: (0, i, 0, 0))
    spec_b = pl.BlockSpec((1, gb, cg, p), lambda i: (0, i, 0, 0))
    spec_w = pl.BlockSpec((1, gb, rows, p), lambda i: (0, i, 0, 0))
    outs = pl.pallas_call(
        functools.partial(_s5_prep_kernel, reverse=reverse),
        grid=(g // gb,),
        in_specs=[spec_a, spec_a, pl.BlockSpec((1, gb, 1, 1), lambda i: (0, i, 0, 0)),
                  spec_b, spec_b, spec_b, spec_b],
        out_specs=[spec_a, spec_a, spec_w, spec_w, spec_w, spec_w,
                   pl.BlockSpec((1, gb, rows, cg), lambda i: (0, i, 0, 0))],
        out_shape=[jax.ShapeDtypeStruct((1, g, 1, p), F32)] * 2
        + [jax.ShapeDtypeStruct((1, g, rows, p), F32)] * 4
        + [jax.ShapeDtypeStruct((1, g, rows, cg), F32)],
        compiler_params=_cparams(1),
        name="s5_prep",
    )(a4(a_re), a4(a_im), log_dt.reshape(1, g, 1, 1), bt(b_re), bt(b_im),
      c_re.reshape(1, g, cg, p), c_im.reshape(1, g, cg, p))
    a16r, a16i, winr, wini, woutr, wouti, kk = [o[0] for o in outs]
    return a16r, a16i, winr, wini, woutr, wouti, kk


def _pair_blockdiag(w):
    g, r, p = w.shape
    w = w.reshape(g // 2, 2, r, p)
    z = jnp.zeros_like(w[:, 0])
    top = jnp.concatenate([w[:, 0], z], axis=-1)
    bot = jnp.concatenate([z, w[:, 1]], axis=-1)
    return jnp.concatenate([top, bot], axis=1)


def _toeplitz(kk, cg, reverse):
    g = kk.shape[0]
    tc = S5_CHUNK
    k5 = kk.reshape(g, tc, cg, cg)
    t_in = jnp.arange(tc)[None, :]
    t_out = jnp.arange(tc)[:, None]
    lag = (t_in - t_out) if reverse else (t_out - t_in)
    blocks = jnp.where((lag >= 0)[None, :, :, None, None], k5[:, jnp.clip(lag, 0, tc - 1)], 0.0)
    return blocks.transpose(0, 1, 3, 2, 4).reshape(g, tc * cg, tc * cg)


def _s5_core_kernel(vc_ref, vx_ref, a16r_ref, a16i_ref, winr_ref, wini_ref, woutr_ref, wouti_ref,
                    tz_ref, yc_ref, yx_ref, sre, sim, hre, him, *, n_seq):
    r = vc_ref.shape[1]
    n_cc = vc_ref.shape[2]
    n_col = n_cc + vx_ref.shape[2]
    n_chunks = n_col // n_seq
    n_ctx_chunks = n_cc // n_seq
    v = jnp.concatenate([jnp.concatenate([vc_ref[half], vx_ref[half]], axis=-1)
                         for half in range(2)], axis=0)
    for direction in range(2):
        sre[...] = lax.dot_general(v, winr_ref[direction, 0], (((0,), (0,)), ((), ())),
                                   preferred_element_type=F32)
        sim[...] = lax.dot_general(v, wini_ref[direction, 0], (((0,), (0,)), ((), ())),
                                   preferred_element_type=F32)
        ar = a16r_ref[direction, 0]
        ai = a16i_ref[direction, 0]

        def step(k, carry):
            h_r, h_i = carry
            rows = pl.ds(pl.multiple_of(k * n_seq, n_seq), n_seq)
            hre[rows, :] = h_r
            him[rows, :] = h_i
            return (ar * h_r - ai * h_i + sre[rows, :], ar * h_i + ai * h_r + sim[rows, :])

        zero = jnp.zeros((n_seq, sre.shape[1]), F32)
        if direction == 0:
            lax.fori_loop(0, n_chunks, step, (zero, zero))
        else:
            mid = lax.fori_loop(0, n_ctx_chunks,
                                lambda i, c: step(n_ctx_chunks - 1 - i, c), (zero, zero))
            lax.fori_loop(0, n_chunks - n_ctx_chunks,
                          lambda i, c: step(n_chunks - 1 - i, c), mid)

        h_r = hre[...].astype(BF16)
        h_i = him[...].astype(BF16)
        for half in range(2):
            rows = pl.ds(half * r, r)
            part = (lax.dot_general(woutr_ref[direction, 0, rows, :], h_r,
                                    (((1,), (1,)), ((), ())), preferred_element_type=F32)
                    + lax.dot_general(wouti_ref[direction, 0, rows, :], h_i,
                                      (((1,), (1,)), ((), ())), preferred_element_type=F32)
                    + jnp.dot(tz_ref[direction, half], v[half * r:(half + 1) * r, :],
                              preferred_element_type=F32))
            if direction == 0:
                yc_ref[half] = part[:, :n_cc]
                yx_ref[half] = part[:, n_cc:]
            else:
                yc_ref[half] = yc_ref[half] + part[:, :n_cc]
                yx_ref[half] = yx_ref[half] + part[:, n_cc:]


def _s5_core(vc, vx, ops_fwd, ops_bwd, n_seq):
    g, r, n_cc = vc.shape
    n_cx = vx.shape[2]
    cg = r // S5_CHUNK
    stacked = []
    for idx in range(6):
        stacked.append(jnp.stack([ops_fwd[idx], ops_bwd[idx]]))
    a16r, a16i, winr, wini, woutr, wouti = stacked
    p = a16r.shape[-1]
    pair_vec = lambda a: a.reshape(2, g // 2, 1, 2 * p)
    pair_mat = lambda w: jnp.stack([_pair_blockdiag(w[0]), _pair_blockdiag(w[1])]).astype(BF16)
    tz = jnp.stack([_toeplitz(ops_fwd[6], cg, False), _toeplitz(ops_bwd[6], cg, True)]).astype(BF16)
    vec_spec = pl.BlockSpec((2, 1, 1, 2 * p), lambda i: (0, i, 0, 0))
    mat_spec = pl.BlockSpec((2, 1, 2 * r, 2 * p), lambda i: (0, i, 0, 0))
    col_spec = lambda n: pl.BlockSpec((2, r, n), lambda i: (i, 0, 0))
    return pl.pallas_call(
        functools.partial(_s5_core_kernel, n_seq=n_seq),
        grid=(g // 2,),
        in_specs=[col_spec(n_cc), col_spec(n_cx),
                  vec_spec, vec_spec, mat_spec, mat_spec, mat_spec, mat_spec,
                  pl.BlockSpec((2, 2, r, r), lambda i: (0, i, 0, 0))],
        out_specs=[col_spec(n_cc), col_spec(n_cx)],
        out_shape=[jax.ShapeDtypeStruct((g, r, n_cc), F32), jax.ShapeDtypeStruct((g, r, n_cx), F32)],
        scratch_shapes=[pltpu.VMEM((n_cc + n_cx, 2 * p), F32)] * 4,
        compiler_params=_cparams(1),
        name="s5_core",
    )(vc, vx, pair_vec(a16r), pair_vec(a16i), pair_mat(winr), pair_mat(wini),
      pair_mat(woutr), pair_mat(wouti), tz)


def _s5_cols_kernel(x_ref, g_ref, sc_ref, sh_ref, v_ref, row_scr):
    nb, r, d = x_ref.shape
    g = v_ref.shape[0]
    cg = d // g
    tc = v_ref.shape[1] // cg
    nk = r // tc
    x = x_ref[...]
    y = x * lax.rsqrt(jnp.mean(x * x, axis=-1, keepdims=True) + RMS_EPS)
    h = ((y * g_ref[...]) * (1.0 + sc_ref[...]) + sh_ref[...]).reshape(nb * r, d)
    n_lane_blocks = row_scr.shape[0]
    gl = V7X_LANES // cg
    for c in range(n_lane_blocks):
        row_scr[c] = h[:, c * V7X_LANES:(c + 1) * V7X_LANES]
    for t in range(tc):
        for c in range(n_lane_blocks):
            z = jnp.concatenate([row_scr[c, pl.ds(t + tc * k, nb, stride=r), :] for k in range(nk)],
                                axis=0)
            v_ref[c * gl:(c + 1) * gl, t * cg:(t + 1) * cg, :] = (
                z.T.reshape(gl, cg, nk * nb).astype(v_ref.dtype))


def _s5_block_rows(n_seq, seq_len):
    return min(seq_len, S5_CHUNK * max(1, V7X_LANES // n_seq))


def _s5_cols(x, mods, n_seq, seq_len, norm_g, n_groups):
    d = x.shape[1]
    r = _s5_block_rows(n_seq, seq_len)
    cols = (r // S5_CHUNK) * n_seq
    rows = S5_CHUNK * (d // n_groups)
    n_mod = mods.shape[0]
    mod_spec = lambda j: pl.BlockSpec((n_mod, 1, d), lambda i: (0, 0, j))
    return pl.pallas_call(
        _s5_cols_kernel,
        grid=(seq_len // r,),
        in_specs=[pl.BlockSpec((n_seq, r, d), lambda i: (0, i, 0)), _full_spec((1, d)),
                  mod_spec(1), mod_spec(0)],
        out_specs=pl.BlockSpec((n_groups, rows, cols), lambda i: (0, 0, i)),
        out_shape=jax.ShapeDtypeStruct((n_groups, rows, (seq_len // S5_CHUNK) * n_seq), BF16),
        scratch_shapes=[pltpu.VMEM((d // V7X_LANES, n_seq * r, V7X_LANES), F32)],
        compiler_params=_cparams(1),
        name="s5_cols",
    )(x.reshape(n_seq, seq_len, d), norm_g.reshape(1, d), mods, mods)


def _s5_rows_kernel(y_ref, o_ref, row_scr):
    nb, r, d = o_ref.shape
    g = y_ref.shape[0]
    cg = d // g
    tc = y_ref.shape[1] // cg
    nk = r // tc
    n_lane_blocks = row_scr.shape[0]
    gl = V7X_LANES // cg
    for t in range(tc):
        for c in range(n_lane_blocks):
            z = y_ref[c * gl:(c + 1) * gl, t * cg:(t + 1) * cg, :].reshape(V7X_LANES, nk * nb).T
            for k in range(nk):
                row_scr[c, pl.ds(t + tc * k, nb, stride=r), :] = z[k * nb:(k + 1) * nb, :]
    o_ref[...] = jnp.concatenate([row_scr[c] for c in range(n_lane_blocks)],
                                 axis=-1).reshape(nb, r, d)


def _s5_rows(y, n_seq, seq_len):
    n_groups, rows, _ = y.shape
    d = n_groups * (rows // S5_CHUNK)
    r = _s5_block_rows(n_seq, seq_len)
    cols = (r // S5_CHUNK) * n_seq
    out = pl.pallas_call(
        _s5_rows_kernel,
        grid=(seq_len // r,),
        in_specs=[pl.BlockSpec((n_groups, rows, cols), lambda i: (0, 0, i))],
        out_specs=pl.BlockSpec((n_seq, r, d), lambda i: (0, i, 0)),
        out_shape=jax.ShapeDtypeStruct((n_seq, seq_len, d), F32),
        scratch_shapes=[pltpu.VMEM((d // V7X_LANES, n_seq * r, V7X_LANES), F32)],
        compiler_params=_cparams(1),
        name="s5_rows",
    )(y)
    return out.reshape(n_seq * seq_len, d)


def _s5_head_kernel(x_ref, y_ref, g_ref, sc_ref, sh_ref, d_ref, w_ref, g1_ref, o_ref):
    x = x_ref[...]
    h = _modnorm(x, g_ref[...], sc_ref[0], sh_ref[0])
    a = jax.nn.gelu(y_ref[...] + d_ref[...] * h, approximate=True)
    z = jnp.dot(a.astype(BF16), w_ref[...], preferred_element_type=F32)
    dm = z.shape[1] // 2
    o_ref[...] = x + g1_ref[0] * (z[:, :dm] * jax.nn.sigmoid(z[:, dm:]))


def _s5_head(x, y, mods, seq_len, norm_g, d_skip, w_glu):
    t, d = x.shape
    tm = _row_tile(seq_len, 512)
    return pl.pallas_call(
        _s5_head_kernel,
        grid=(t // tm,),
        in_specs=[_row_spec(tm, d), _row_spec(tm, d), _full_spec((1, d)),
                  _mod_spec(mods.shape[0], tm, seq_len, d, 1),
                  _mod_spec(mods.shape[0], tm, seq_len, d, 0),
                  _full_spec((1, d)), _full_spec(w_glu.shape),
                  _mod_spec(mods.shape[0], tm, seq_len, d, 2)],
        out_specs=_row_spec(tm, d),
        out_shape=jax.ShapeDtypeStruct((t, d), F32),
        compiler_params=_cparams(1),
        name="s5_head",
    )(x, y, norm_g.reshape(1, d), mods, mods, d_skip.reshape(1, d), w_glu, mods)


def _s5_mixer(x, ctx, mods_x, mods_c, n_seq, norm_g, a_re, a_im, log_dt, b_re, b_im, c_re, c_im,
              d_skip, w_glu):
    s_len, c_len = x.shape[0] // n_seq, ctx.shape[0] // n_seq
    g = a_re.shape[1]
    ops = [_s5_prep(a_re[k], a_im[k], log_dt[k], b_re[k], b_im[k], c_re[k], c_im[k], bool(k))
           for k in range(2)]
    vc = _s5_cols(ctx, mods_c, n_seq, c_len, norm_g, g)
    vx = _s5_cols(x, mods_x, n_seq, s_len, norm_g, g)
    y_c, y_x = _s5_core(vc, vx, ops[0], ops[1], n_seq)
    x = _s5_head(x, _s5_rows(y_x, n_seq, s_len), mods_x, s_len, norm_g, d_skip, w_glu)
    ctx = _s5_head(ctx, _s5_rows(y_c, n_seq, c_len), mods_c, c_len, norm_g, d_skip, w_glu)
    return x, ctx


def _pack_pairs(lo, hi):
    lo_bits = lax.bitcast_convert_type(lo.astype(BF16).astype(F32), U32)
    hi_bits = lax.bitcast_convert_type(hi.astype(BF16).astype(F32), U32)
    return lax.shift_right_logical(lo_bits, jnp.uint32(16)) | (hi_bits & jnp.uint32(0xFFFF0000))


def _unpack_pairs(words):
    lo = lax.bitcast_convert_type(lax.shift_left(words, jnp.uint32(16)), F32)
    hi = lax.bitcast_convert_type(words & jnp.uint32(0xFFFF0000), F32)
    return lo, hi


def _pack_row_chunks(rows, pp):
    lanes = V7X_LANES
    return [_pack_pairs(rows[:, c * lanes:(c + 1) * lanes], rows[:, (c + pp) * lanes:(c + pp + 1) * lanes])
            for c in range(pp)]


def _unpack_row_chunks(chunks):
    pairs = [_unpack_pairs(w) for w in chunks]
    return jnp.concatenate([lo for lo, _ in pairs] + [hi for _, hi in pairs], axis=-1)


def _router_kernel(x_ref, g_ref, sc_ref, sh_ref, wr_ref, br_ref, h_ref, r_ref, *, n_groups, epg):
    h = _modnorm(x_ref[...], g_ref[...], sc_ref[0], sh_ref[0])
    tm = x_ref.shape[0]
    pp = h_ref.shape[0] // tm
    for c, words in enumerate(_pack_row_chunks(h, pp)):
        h_ref[pl.ds(c, tm, stride=pp), :] = words
    logits = lax.dot_general(wr_ref[...], h, (((1,), (1,)), ((), ())), preferred_element_type=F32,
                             precision=HIGHEST) + br_ref[...]
    row = lax.broadcasted_iota(I32, logits.shape, 0)
    far = jnp.int32(1 << 20)

    def first_max(vals):
        m = jnp.max(vals, axis=0, keepdims=True)
        return m, jnp.min(jnp.where(vals == m, row, far), axis=0, keepdims=True)

    is_group = row < n_groups
    gl = jnp.where(is_group, logits, NEG_BIG)
    gmax, gidx = first_max(gl)
    gsum = jnp.sum(jnp.where(is_group, jnp.exp(gl - gmax), 0.0), axis=0, keepdims=True)
    g_w = 1.0 / gsum
    lo = n_groups + gidx * epg
    le = jnp.where((row >= lo) & (row < lo + epg), logits, NEG_BIG)
    m1, i1 = first_max(le)
    m2, i2 = first_max(jnp.where(row == i1, NEG_BIG, le))
    ratio = jnp.exp(m2 - m1)
    w1 = g_w / (1.0 + ratio)
    w2 = g_w * ratio / (1.0 + ratio)
    e1 = (i1 - n_groups).astype(F32)
    e2 = (i2 - n_groups).astype(F32)
    out_row = lax.broadcasted_iota(I32, r_ref.shape, 0)
    r_ref[...] = jnp.where(out_row == 0, e1, jnp.where(out_row == 1, e2, jnp.where(
        out_row == 2, w1, jnp.where(out_row == 3, w2, 0.0))))


def _router(x, mods, seq_len, norm_g, wr, br, n_groups, epg):
    t, d = x.shape
    tm = _row_tile(seq_len, 512)
    pp = d // (2 * V7X_LANES)
    return pl.pallas_call(
        functools.partial(_router_kernel, n_groups=n_groups, epg=epg),
        grid=(t // tm,),
        in_specs=[_row_spec(tm, d), _full_spec((1, d)),
                  _mod_spec(mods.shape[0], tm, seq_len, d, 4),
                  _mod_spec(mods.shape[0], tm, seq_len, d, 3),
                  _full_spec(wr.shape), _full_spec(br.shape)],
        out_specs=[pl.BlockSpec((tm * pp, V7X_LANES), lambda i: (i, 0)),
                   pl.BlockSpec((V7X_SUBLANES, tm), lambda i: (0, i))],
        out_shape=[jax.ShapeDtypeStruct((t * pp, V7X_LANES), U32),
                   jax.ShapeDtypeStruct((V7X_SUBLANES, t), F32)],
        compiler_params=_cparams(1),
        name="moe_router",
    )(x, norm_g.reshape(1, d), mods, mods, wr, br)


def _dispatch_lists(route, n_block, n_exp, cap, pp):
    t = route.shape[1]
    n_sb = t // n_block
    n_assign = n_block * TOP_K
    ids = route[0:TOP_K].T.astype(I32).reshape(n_sb, n_assign)
    asg = jnp.broadcast_to(jnp.arange(n_assign, dtype=I32)[None, :], (n_sb, n_assign))
    _, asg_s = lax.sort((ids, asg), dimension=1, is_stable=True, num_keys=1)
    counts = jnp.sum((ids[:, :, None] == jnp.arange(n_exp, dtype=I32)).astype(I32), axis=1)
    offs = jnp.cumsum(counts, axis=1) - counts
    asg_s = jnp.pad(asg_s, ((0, 0), (0, cap - n_assign)), constant_values=n_assign)
    src = jnp.minimum(asg_s // TOP_K, n_block - 1) * pp
    return (counts.reshape(-1), offs.reshape(-1),
            src.reshape(n_sb, 1, cap), (asg_s * pp).reshape(n_sb, 1, cap))


def _moe_kernel(cnt_ref, off_ref, src_ref, dst_ref, h_ref, w13_ref, w2_ref, o_ref,
                lhs_a, lhs_b, ys_a, ys_b, *, n_exp, tile):
    sb = pl.program_id(0)
    eb = pl.program_id(1)
    n_local = w13_ref.shape[0]
    pp = w13_ref.shape[1] // (2 * V7X_LANES)
    de = w2_ref.shape[1]
    scratch = ((lhs_a, ys_a), (lhs_b, ys_b))

    def slab(row):
        return pl.ds(pl.multiple_of(row, pp), pp)

    @pl.when(eb == 0)
    def _():
        n_live = h_ref.shape[0] * TOP_K
        spare = o_ref.shape[1] - n_live
        o_ref[0, pl.ds(n_live, spare), :] = jnp.zeros((spare, V7X_LANES), U32)

    def run_tiles(q, base, sizes):
        starts = [base + sum(sizes[:k]) for k in range(len(sizes))]
        for (lhs_scr, _), start, size in zip(scratch, starts, sizes):
            stride = size + 1
            for mi in range(size):
                lhs_scr[pl.ds(mi, pp, stride=stride), :] = h_ref[slab(src_ref[0, 0, start + mi]), :]
        for (lhs_scr, ys_scr), size in zip(scratch, sizes):
            stride = size + 1
            lhs = _unpack_row_chunks([lhs_scr[pl.ds(c * stride, size), :] for c in range(pp)])
            hid = jnp.dot(lhs.astype(BF16), w13_ref[q], preferred_element_type=F32)
            act = _silu(hid[:, :de]) * hid[:, de:]
            ys = jnp.dot(act.astype(BF16), w2_ref[q], preferred_element_type=F32)
            for c, words in enumerate(_pack_row_chunks(ys, pp)):
                ys_scr[pl.ds(c * stride, size), :] = words
        for (_, ys_scr), start, size in zip(scratch, starts, sizes):
            stride = size + 1
            for mi in range(size):
                o_ref[0, slab(dst_ref[0, 0, start + mi]), :] = ys_scr[pl.ds(mi, pp, stride=stride), :]

    half = tile // 2

    def one_expert(q, carry):
        e = eb * n_local + q
        count = cnt_ref[sb * n_exp + e]
        seg = off_ref[sb * n_exp + e]
        n_pairs = lax.div(count, 2 * tile)

        def pair(j, c):
            run_tiles(q, seg + j * 2 * tile, (tile, tile))
            return c
        lax.fori_loop(0, n_pairs, pair, 0)

        rem = count - n_pairs * 2 * tile
        rem_base = seg + n_pairs * 2 * tile
        for hi, sizes in ((half, (half,)), (tile, (tile,)), (tile + half, (tile, half)),
                          (2 * tile, (tile, tile))):
            @pl.when((rem > hi - half) & (rem <= hi))
            def _():
                run_tiles(q, rem_base, sizes)
        return carry

    lax.fori_loop(0, n_local, one_expert, 0)


def _moe_tile(n_block, n_exp):
    return max(4 * V7X_SUBLANES, n_block * TOP_K // n_exp)


def _moe_experts(h, route, w13, w2, n_block):
    n_exp, d, de2 = w13.shape
    de = de2 // 2
    pp = d // (2 * V7X_LANES)
    n_sb = h.shape[0] // (n_block * pp)
    n_assign = n_block * TOP_K
    tile = _moe_tile(n_block, n_exp)
    cap = -(-(n_assign + 2 * tile) // V7X_LANES) * V7X_LANES
    cnt, off, src_list, dst_list = _dispatch_lists(route, n_block, n_exp, cap, pp)
    out_slabs = n_assign + V7X_SUBLANES
    eb = MOE_EXPERTS_PER_STEP
    list_spec = pl.BlockSpec((1, 1, cap), lambda s, e, *_: (s, 0, 0), memory_space=pltpu.SMEM)
    grid_spec = pltpu.PrefetchScalarGridSpec(
        num_scalar_prefetch=2,
        grid=(n_sb, n_exp // eb),
        in_specs=[list_spec, list_spec,
                  pl.BlockSpec((n_block * pp, V7X_LANES), lambda s, e, *_: (s, 0),
                               pipeline_mode=pl.Buffered(1)),
                  pl.BlockSpec((eb, d, de2), lambda s, e, *_: (e, 0, 0)),
                  pl.BlockSpec((eb, de, d), lambda s, e, *_: (e, 0, 0))],
        out_specs=pl.BlockSpec((1, out_slabs * pp, V7X_LANES), lambda s, e, *_: (s, 0, 0)),
        scratch_shapes=[pltpu.VMEM((pp * (tile + 1), V7X_LANES), U32)] * 4)
    return pl.pallas_call(
        functools.partial(_moe_kernel, n_exp=n_exp, tile=tile),
        grid_spec=grid_spec,
        out_shape=jax.ShapeDtypeStruct((n_sb, out_slabs * pp, V7X_LANES), U32),
        compiler_params=_cparams(2),
        name="moe_experts",
    )(cnt, off, src_list, dst_list, h, w13, w2)


def _residual_kernel(x_ref, y_ref, w_ref, g2_ref, fg_ref, o_ref, *, final_norm):
    tm = x_ref.shape[0]
    wts = w_ref[...]
    pp = y_ref.shape[1] // (tm * TOP_K)
    y = sum(wts[:, k:k + 1]
            * _unpack_row_chunks([y_ref[0, pl.ds(k * pp + c, tm, stride=TOP_K * pp), :]
                                  for c in range(pp)])
            for k in range(TOP_K))
    x = x_ref[...] + g2_ref[0] * y
    if final_norm:
        x = x * lax.rsqrt(jnp.mean(x * x, axis=-1, keepdims=True) + RMS_EPS) * fg_ref[...]
    o_ref[...] = x


def _residual(x, y, wts, mods, seq_len, n_block, final_g, final_norm):
    t, d = x.shape
    tm = _row_tile(min(seq_len, n_block), 512)
    per_sb = n_block // tm
    pp = d // (2 * V7X_LANES)
    return pl.pallas_call(
        functools.partial(_residual_kernel, final_norm=final_norm),
        grid=(t // tm,),
        in_specs=[_row_spec(tm, d),
                  pl.BlockSpec((1, tm * TOP_K * pp, V7X_LANES), lambda i: (i // per_sb, i % per_sb, 0)),
                  _row_spec(tm, TOP_K),
                  _mod_spec(mods.shape[0], tm, seq_len, d, 5), _full_spec((1, d))],
        out_specs=_row_spec(tm, d),
        out_shape=jax.ShapeDtypeStruct((t, d), F32),
        compiler_params=_cparams(1),
        name="moe_residual",
    )(x, y, wts, mods, final_g.reshape(1, d))


def _moe_block(t):
    n = min(t, 4096)
    while t % n:
        n //= 2
    return n


def _moe(x, mods, seq_len, norm_g, wr, br, w13, w2, n_groups, final_g, final_norm):
    n_exp = w13.shape[0]
    n_block = _moe_block(x.shape[0])
    h, route = _router(x, mods, seq_len, norm_g, wr, br, n_groups, n_exp // n_groups)
    y = _moe_experts(h, route, w13, w2, n_block)
    return _residual(x, y, route[TOP_K:2 * TOP_K].T, mods, seq_len, n_block, final_g, final_norm)


def kernel(x, c, ctx, c_ctx, ada_w, ada_b, norm1_g, norm2_g, conf_w_in, conf_dw, conf_dw_b, conf_ln_g, conf_ln_b, conf_w_out, sc_w_in, sc_conv, sc_w_out, s5_a_re, s5_a_im, s5_log_dt, s5_b_re, s5_b_im, s5_c_re, s5_c_im, s5_d, s5_w_glu, moe_wg, moe_bg, moe_we, moe_be, moe_w13, moe_w2, final_g):
    b, s, d = x.shape
    lc = ctx.shape[1]
    depth = ada_w.shape[0]
    n_groups = moe_wg.shape[-1]
    n_exp = moe_we.shape[-1]
    assert s % GRID_W == 0

    rows = (b + 1 + V7X_SUBLANES - 1) // V7X_SUBLANES * V7X_SUBLANES
    cin = jnp.zeros((rows, d), F32).at[:b].set(c).at[b].set(c_ctx)
    table = _ada_table(cin, ada_w, ada_b)

    xs = x.reshape(b * s, d)
    cs = ctx.reshape(b * lc, d)
    for i in range(depth):
        kind, j = i % N_MIXERS, i // N_MIXERS
        update_ctx = i < depth - 1
        mods_x = table[i, :b].reshape(b, 1, 6 * d)
        mods_c = table[i, b].reshape(1, 1, 6 * d)
        if kind == 0:
            args = (norm1_g[i], conf_w_in[j].astype(BF16), conf_dw[j], conf_dw_b[j],
                    conf_ln_g[j], conf_ln_b[j], conf_w_out[j].astype(BF16))
            xs = _conformer(xs, mods_x, s, GRID_W, *args)
            if update_ctx:
                cs = _conformer(cs, mods_c, lc, 1, *args)
        elif kind == 1:
            args = (norm1_g[i], sc_w_in[j].astype(BF16), sc_conv[j], sc_w_out[j].astype(BF16))
            xs = _short_conv(xs, mods_x, s, GRID_W, *args)
            if update_ctx:
                cs = _short_conv(cs, mods_c, lc, lc, *args)
        else:
            xs, cs_new = _s5_mixer(xs, cs, mods_x, mods_c, b, norm1_g[i], s5_a_re[j], s5_a_im[j],
                                   s5_log_dt[j], s5_b_re[j], s5_b_im[j], s5_c_re[j], s5_c_im[j],
                                   s5_d[j], s5_w_glu[j].astype(BF16))
            if update_ctx:
                cs = cs_new

        n_logit = -(-(n_groups + n_exp) // V7X_SUBLANES) * V7X_SUBLANES
        wr = jnp.zeros((n_logit, d), F32).at[:n_groups].set(moe_wg[i].T)
        wr = wr.at[n_groups:n_groups + n_exp].set(moe_we[i].T)
        br = jnp.zeros((n_logit, 1), F32).at[:n_groups, 0].set(moe_bg[i])
        br = br.at[n_groups:n_groups + n_exp, 0].set(moe_be[i])
        moe_args = (norm2_g[i], wr, br, moe_w13[i].astype(BF16), moe_w2[i].astype(BF16), n_groups,
                    final_g)
        xs = _moe(xs, mods_x, s, *moe_args, final_norm=(i == depth - 1))
        if update_ctx:
            cs = _moe(cs, mods_c, lc, *moe_args, final_norm=False)
    return xs.reshape(b, s, d)
```

```python
import functools

import jax
import jax.numpy as jnp
from jax import lax
from jax.experimental import pallas as pl
from jax.experimental.pallas import tpu as pltpu

F32 = jnp.float32
BF16 = jnp.bfloat16
I32 = jnp.int32
U32 = jnp.uint32
HIGHEST = lax.Precision.HIGHEST

GRID_W = 64
N_MIXERS = 3
TOP_K = 2
RMS_EPS = 1e-6
LN_EPS = 1e-5
S5_DT_FLOOR = -1e-4

V7X_VMEM_BYTES = 64 * 1024 * 1024
V7X_LANES = 128
V7X_SUBLANES = 8
VMEM_LIMIT_BYTES = V7X_VMEM_BYTES - 6 * 1024 * 1024

S5_CHUNK = 16
MOE_EXPERTS_PER_STEP = 2
NEG_BIG = -1e30


def _cparams(n_axes):
    return pltpu.CompilerParams(dimension_semantics=("arbitrary",) * n_axes,
                                vmem_limit_bytes=VMEM_LIMIT_BYTES)


def _row_tile(seq_len, want):
    t = min(seq_len, want)
    while seq_len % t or t % V7X_SUBLANES:
        t -= 1
    return t


def _modnorm(x, g, sc, sh):
    y = x * lax.rsqrt(jnp.mean(x * x, axis=-1, keepdims=True) + RMS_EPS)
    return (y * g) * (1.0 + sc) + sh


def _silu(v):
    return v * jax.nn.sigmoid(v)


def _mod_spec(n_mod, tm, seq_len, d, j):
    if n_mod == 1:
        return pl.BlockSpec((1, 1, d), lambda t: (0, 0, j))
    return pl.BlockSpec((1, 1, d), lambda t: ((t * tm) // seq_len, 0, j))


def _row_spec(tm, d):
    return pl.BlockSpec((tm, d), lambda t: (t, 0))


def _full_spec(shape):
    nd = len(shape)
    return pl.BlockSpec(shape, lambda *_: (0,) * nd)


def _ada_kernel(c_ref, w_ref, b_ref, o_ref):
    o_ref[0] = jnp.dot(_silu(c_ref[...]), w_ref[0], preferred_element_type=F32,
                       precision=HIGHEST) + b_ref[0]


def _ada_table(cin, ada_w, ada_b):
    depth, d, d6 = ada_w.shape
    r = cin.shape[0]
    tn = d6 // 6
    return pl.pallas_call(
        _ada_kernel,
        grid=(depth, d6 // tn),
        in_specs=[pl.BlockSpec((r, d), lambda i, j: (0, 0)),
                  pl.BlockSpec((1, d, tn), lambda i, j: (i, 0, j)),
                  pl.BlockSpec((1, 1, tn), lambda i, j: (i, 0, j))],
        out_specs=pl.BlockSpec((1, r, tn), lambda i, j: (i, 0, j)),
        out_shape=jax.ShapeDtypeStruct((depth, r, d6), F32),
        compiler_params=_cparams(2),
        name="ada_table",
    )(cin, ada_w, ada_b.reshape(depth, 1, d6))


def _conf_in_kernel(x_ref, g_ref, sc_ref, sh_ref, w_ref, z_ref):
    h = _modnorm(x_ref[...], g_ref[...], sc_ref[0], sh_ref[0])
    y = jnp.dot(h.astype(BF16), w_ref[...], preferred_element_type=F32)
    ci = y.shape[1] // 2
    z_ref[...] = (y[:, :ci] * jax.nn.sigmoid(y[:, ci:])).astype(z_ref.dtype)


def _conf_in(x, mods, seq_len, norm_g, w_in):
    t, d = x.shape
    tm = _row_tile(seq_len, 512)
    ci = w_in.shape[1] // 2
    return pl.pallas_call(
        _conf_in_kernel,
        grid=(t // tm,),
        in_specs=[_row_spec(tm, d), _full_spec((1, d)),
                  _mod_spec(mods.shape[0], tm, seq_len, d, 1),
                  _mod_spec(mods.shape[0], tm, seq_len, d, 0),
                  _full_spec(w_in.shape)],
        out_specs=_row_spec(tm, ci),
        out_shape=jax.ShapeDtypeStruct((t, ci), BF16),
        compiler_params=_cparams(1),
        name="conf_in",
    )(x, norm_g.reshape(1, d), mods, mods, w_in)


def _dwconv_kernel(z_ref, w_ref, b_ref, o_ref, src_scr, *, stride, chunk):
    seq_len, cb = z_ref.shape
    taps = w_ref.shape[0]
    half = taps // 2
    aligned = stride % chunk == 0
    pad = 0 if aligned else half * stride
    if pad:
        src_scr[pl.ds(0, pad), :] = jnp.zeros((pad, cb), F32)
        src_scr[pl.ds(pad + seq_len, pad), :] = jnp.zeros((pad, cb), F32)
    src_scr[pl.ds(pad, seq_len), :] = z_ref[...].astype(F32)
    w = w_ref[...]
    bias = b_ref[...]
    for r0 in range(0, seq_len, chunk):
        acc = jnp.broadcast_to(bias, (chunk, cb))
        for k in range(taps):
            lo = r0 + (k - half) * stride
            if aligned and (lo < 0 or lo + chunk > seq_len):
                continue
            acc = acc + w[k:k + 1, :] * src_scr[pl.ds(lo + pad, chunk), :]
        o_ref[pl.ds(r0, chunk), :] = acc.astype(o_ref.dtype)


def _dwconv(z, seq_len, stride, w, b):
    t, c = z.shape
    taps = w.shape[0]
    cb = min(c, 2 * V7X_LANES)
    chunk = _row_tile(seq_len, 64)
    pad = 0 if stride % chunk == 0 else (taps // 2) * stride
    return pl.pallas_call(
        functools.partial(_dwconv_kernel, stride=stride, chunk=chunk),
        grid=(t // seq_len, c // cb),
        in_specs=[pl.BlockSpec((seq_len, cb), lambda s, j: (s, j)),
                  pl.BlockSpec((taps, cb), lambda s, j: (0, j)),
                  pl.BlockSpec((1, cb), lambda s, j: (0, j))],
        out_specs=pl.BlockSpec((seq_len, cb), lambda s, j: (s, j)),
        out_shape=jax.ShapeDtypeStruct((t, c), BF16),
        scratch_shapes=[pltpu.VMEM((seq_len + 2 * pad, cb), F32)],
        compiler_params=_cparams(2),
        name="dwconv",
    )(z, w, b.reshape(1, c))


def _conf_out_kernel(z_ref, lg_ref, lb_ref, w_ref, x_ref, g1_ref, o_ref):
    z = z_ref[...].astype(F32)
    mu = jnp.mean(z, axis=-1, keepdims=True)
    zc = z - mu
    var = jnp.mean(zc * zc, axis=-1, keepdims=True)
    y = zc * lax.rsqrt(var + LN_EPS) * lg_ref[...] + lb_ref[...]
    m = jnp.dot(_silu(y).astype(BF16), w_ref[...], preferred_element_type=F32)
    o_ref[...] = x_ref[...] + g1_ref[0] * m


def _conf_out(z, x, mods, seq_len, ln_g, ln_b, w_out):
    t, d = x.shape
    ci = z.shape[1]
    tm = _row_tile(seq_len, 1024)
    return pl.pallas_call(
        _conf_out_kernel,
        grid=(t // tm,),
        in_specs=[_row_spec(tm, ci), _full_spec((1, ci)), _full_spec((1, ci)),
                  _full_spec(w_out.shape), _row_spec(tm, d),
                  _mod_spec(mods.shape[0], tm, seq_len, d, 2)],
        out_specs=_row_spec(tm, d),
        out_shape=jax.ShapeDtypeStruct((t, d), F32),
        compiler_params=_cparams(1),
        name="conf_out",
    )(z, ln_g.reshape(1, ci), ln_b.reshape(1, ci), w_out, x, mods)


def _conformer(x, mods, seq_len, stride, norm_g, w_in, dw, dw_b, ln_g, ln_b, w_out):
    z = _conf_in(x, mods, seq_len, norm_g, w_in)
    z = _dwconv(z, seq_len, stride, dw, dw_b)
    return _conf_out(z, x, mods, seq_len, ln_g, ln_b, w_out)


def _sc_kernel(x_ref, g_ref, sc_ref, sh_ref, win_ref, cw_ref, wout_ref, g1_ref, o_ref, *, period):
    x = x_ref[...]
    tm, d = x.shape
    h = _modnorm(x, g_ref[...], sc_ref[0], sh_ref[0])
    y = jnp.dot(h.astype(BF16), win_ref[...], preferred_element_type=F32)
    gb, gc, v = y[:, :d], y[:, d:2 * d], y[:, 2 * d:]
    u = gc * v
    pos = lax.broadcasted_iota(I32, (tm, 1), 0) % period
    u_prev = jnp.where(pos == 0, 0.0, pltpu.roll(u, 1, 0))
    u_next = jnp.where(pos == period - 1, 0.0, pltpu.roll(u, tm - 1, 0))
    cw = cw_ref[...]
    conv = cw[0:1, :] * u_prev + cw[1:2, :] * u + cw[2:3, :] * u_next
    m = jnp.dot((gb * conv).astype(BF16), wout_ref[...], preferred_element_type=F32)
    o_ref[...] = x + g1_ref[0] * m


def _short_conv(x, mods, seq_len, period, norm_g, w_in, conv_w, w_out):
    t, d = x.shape
    tm = _row_tile(seq_len, 512)
    assert tm % period == 0 and conv_w.shape[0] == 3
    return pl.pallas_call(
        functools.partial(_sc_kernel, period=period),
        grid=(t // tm,),
        in_specs=[_row_spec(tm, d), _full_spec((1, d)),
                  _mod_spec(mods.shape[0], tm, seq_len, d, 1),
                  _mod_spec(mods.shape[0], tm, seq_len, d, 0),
                  _full_spec(w_in.shape), _full_spec(conv_w.shape), _full_spec(w_out.shape),
                  _mod_spec(mods.shape[0], tm, seq_len, d, 2)],
        out_specs=_row_spec(tm, d),
        out_shape=jax.ShapeDtypeStruct((t, d), F32),
        compiler_params=_cparams(1),
        name="short_conv",
    )(x, norm_g.reshape(1, d), mods, mods, w_in, conv_w, w_out, mods)


def _s5_prep_kernel(are_ref, aim_ref, ldt_ref, bre_ref, bim_ref, cre_ref, cim_ref,
                    a16r_ref, a16i_ref, winr_ref, wini_ref, woutr_ref, wouti_ref, k_ref, *, reverse):
    a_re = jnp.minimum(are_ref[0], S5_DT_FLOOR)
    a_im = aim_ref[0]
    dt = jnp.exp(ldt_ref[0])
    b_re, b_im = bre_ref[0], bim_ref[0]
    c_re, c_im = cre_ref[0], cim_ref[0]
    cg = b_re.shape[1]
    tc = S5_CHUNK

    def power(n):
        mag = jnp.exp((n * dt) * a_re)
        ang = (n * dt) * a_im
        return mag * jnp.cos(ang), mag * jnp.sin(ang)

    abar_re, abar_im = power(1)
    den = a_re * a_re + a_im * a_im
    n_re = abar_re - 1.0
    n_im = abar_im
    k_re = (n_re * a_re + n_im * a_im) / den
    k_im = (n_im * a_re - n_re * a_im) / den
    bb_re = k_re * b_re - k_im * b_im
    bb_im = k_re * b_im + k_im * b_re

    e16r, e16i = power(tc)
    a16r_ref[0] = e16r
    a16i_ref[0] = e16i
    for t in range(tc):
        er, ei = power(t if reverse else tc - 1 - t)
        winr_ref[0, :, t * cg:(t + 1) * cg, :] = er * bb_re - ei * bb_im
        wini_ref[0, :, t * cg:(t + 1) * cg, :] = er * bb_im + ei * bb_re
        er, ei = power(tc - t if reverse else t + 1)
        woutr_ref[0, :, t * cg:(t + 1) * cg, :] = c_re * er - c_im * ei
        wouti_ref[0, :, t * cg:(t + 1) * cg, :] = -(c_re * ei + c_im * er)
        er, ei = power(t)
        m_re = c_re * er - c_im * ei
        m_im = c_re * ei + c_im * er
        k_ref[0, :, t * cg:(t + 1) * cg, :] = (
            jnp.einsum('gap,gbp->gab', m_re, bb_re, preferred_element_type=F32, precision=HIGHEST)
            - jnp.einsum('gap,gbp->gab', m_im, bb_im, preferred_element_type=F32, precision=HIGHEST))


def _s5_prep(a_re, a_im, log_dt, b_re, b_im, c_re, c_im, reverse):
    g, p = a_re.shape
    cg = b_re.shape[2]
    gb = min(g, 8)
    rows = S5_CHUNK * cg
    a4 = lambda a: a.reshape(1, g, 1, p)
    bt = lambda b: jnp.swapaxes(b, 1, 2).reshape(1, g, cg, p)
    spec_a = pl.BlockSpec((1, gb, 1, p), lambda i: (0, i, 0, 0))
    spec_b = pl.BlockSpec((1, gb, cg, p), lambda i: (0, i, 0, 0))
    spec_w = pl.BlockSpec((1, gb, rows, p), lambda i: (0, i, 0, 0))
    outs = pl.pallas_call(
        functools.partial(_s5_prep_kernel, reverse=reverse),
        grid=(g // gb,),
        in_specs=[spec_a, spec_a, pl.BlockSpec((1, gb, 1, 1), lambda i: (0, i, 0, 0)),
                  spec_b, spec_b, spec_b, spec_b],
        out_specs=[spec_a, spec_a, spec_w, spec_w, spec_w, spec_w,
                   pl.BlockSpec((1, gb, rows, cg), lambda i: (0, i, 0, 0))],
        out_shape=[jax.ShapeDtypeStruct((1, g, 1, p), F32)] * 2
        + [jax.ShapeDtypeStruct((1, g, rows, p), F32)] * 4
        + [jax.ShapeDtypeStruct((1, g, rows, cg), F32)],
        compiler_params=_cparams(1),
        name="s5_prep",
    )(a4(a_re), a4(a_im), log_dt.reshape(1, g, 1, 1), bt(b_re), bt(b_im),
      c_re.reshape(1, g, cg, p), c_im.reshape(1, g, cg, p))
    a16r, a16i, winr, wini, woutr, wouti, kk = [o[0] for o in outs]
    return a16r, a16i, winr, wini, woutr, wouti, kk


def _pair_blockdiag(w):
    g, r, p = w.shape
    w = w.reshape(g // 2, 2, r, p)
    z = jnp.zeros_like(w[:, 0])
    top = jnp.concatenate([w[:, 0], z], axis=-1)
    bot = jnp.concatenate([z, w[:, 1]], axis=-1)
    return jnp.concatenate([top, bot], axis=1)


def _toeplitz(kk, cg, reverse):
    g = kk.shape[0]
    tc = S5_CHUNK
    k5 = kk.reshape(g, tc, cg, cg)
    t_in = jnp.arange(tc)[None, :]
    t_out = jnp.arange(tc)[:, None]
    lag = (t_in - t_out) if reverse else (t_out - t_in)
    blocks = jnp.where((lag >= 0)[None, :, :, None, None], k5[:, jnp.clip(lag, 0, tc - 1)], 0.0)
    return blocks.transpose(0, 1, 3, 2, 4).reshape(g, tc * cg, tc * cg)


def _s5_core_kernel(vc_ref, vx_ref, a16r_ref, a16i_ref, winr_ref, wini_ref, woutr_ref, wouti_ref,
                    tz_ref, yc_ref, yx_ref, sre, sim, hre, him, *, n_seq):
    r = vc_ref.shape[1]
    n_cc = vc_ref.shape[2]
    n_col = n_cc + vx_ref.shape[2]
    n_chunks = n_col // n_seq
    n_ctx_chunks = n_cc // n_seq
    v = jnp.concatenate([jnp.concatenate([vc_ref[half], vx_ref[half]], axis=-1)
                         for half in range(2)], axis=0)
    for direction in range(2):
        sre[...] = lax.dot_general(v, winr_ref[direction, 0], (((0,), (0,)), ((), ())),
                                   preferred_element_type=F32)
        sim[...] = lax.dot_general(v, wini_ref[direction, 0], (((0,), (0,)), ((), ())),
                                   preferred_element_type=F32)
        ar = a16r_ref[direction, 0]
        ai = a16i_ref[direction, 0]

        def step(k, carry):
            h_r, h_i = carry
            rows = pl.ds(pl.multiple_of(k * n_seq, n_seq), n_seq)
            hre[rows, :] = h_r
            him[rows, :] = h_i
            return (ar * h_r - ai * h_i + sre[rows, :], ar * h_i + ai * h_r + sim[rows, :])

        zero = jnp.zeros((n_seq, sre.shape[1]), F32)
        if direction == 0:
            lax.fori_loop(0, n_chunks, step, (zero, zero))
        else:
            mid = lax.fori_loop(0, n_ctx_chunks,
                                lambda i, c: step(n_ctx_chunks - 1 - i, c), (zero, zero))
            lax.fori_loop(0, n_chunks - n_ctx_chunks,
                          lambda i, c: step(n_chunks - 1 - i, c), mid)

        h_r = hre[...].astype(BF16)
        h_i = him[...].astype(BF16)
        for half in range(2):
            rows = pl.ds(half * r, r)
            part = (lax.dot_general(woutr_ref[direction, 0, rows, :], h_r,
                                    (((1,), (1,)), ((), ())), preferred_element_type=F32)
                    + lax.dot_general(wouti_ref[direction, 0, rows, :], h_i,
                                      (((1,), (1,)), ((), ())), preferred_element_type=F32)
                    + jnp.dot(tz_ref[direction, half], v[half * r:(half + 1) * r, :],
                              preferred_element_type=F32))
            if direction == 0:
                yc_ref[half] = part[:, :n_cc]
                yx_ref[half] = part[:, n_cc:]
            else:
                yc_ref[half] = yc_ref[half] + part[:, :n_cc]
                yx_ref[half] = yx_ref[half] + part[:, n_cc:]


def _s5_core(vc, vx, ops_fwd, ops_bwd, n_seq):
    g, r, n_cc = vc.shape
    n_cx = vx.shape[2]
    cg = r // S5_CHUNK
    stacked = []
    for idx in range(6):
        stacked.append(jnp.stack([ops_fwd[idx], ops_bwd[idx]]))
    a16r, a16i, winr, wini, woutr, wouti = stacked
    p = a16r.shape[-1]
    pair_vec = lambda a: a.reshape(2, g // 2, 1, 2 * p)
    pair_mat = lambda w: jnp.stack([_pair_blockdiag(w[0]), _pair_blockdiag(w[1])]).astype(BF16)
    tz = jnp.stack([_toeplitz(ops_fwd[6], cg, False), _toeplitz(ops_bwd[6], cg, True)]).astype(BF16)
    vec_spec = pl.BlockSpec((2, 1, 1, 2 * p), lambda i: (0, i, 0, 0))
    mat_spec = pl.BlockSpec((2, 1, 2 * r, 2 * p), lambda i: (0, i, 0, 0))
    col_spec = lambda n: pl.BlockSpec((2, r, n), lambda i: (i, 0, 0))
    return pl.pallas_call(
        functools.partial(_s5_core_kernel, n_seq=n_seq),
        grid=(g // 2,),
        in_specs=[col_spec(n_cc), col_spec(n_cx),
                  vec_spec, vec_spec, mat_spec, mat_spec, mat_spec, mat_spec,
                  pl.BlockSpec((2, 2, r, r), lambda i: (0, i, 0, 0))],
        out_specs=[col_spec(n_cc), col_spec(n_cx)],
        out_shape=[jax.ShapeDtypeStruct((g, r, n_cc), F32), jax.ShapeDtypeStruct((g, r, n_cx), F32)],
        scratch_shapes=[pltpu.VMEM((n_cc + n_cx, 2 * p), F32)] * 4,
        compiler_params=_cparams(1),
        name="s5_core",
    )(vc, vx, pair_vec(a16r), pair_vec(a16i), pair_mat(winr), pair_mat(wini),
      pair_mat(woutr), pair_mat(wouti), tz)


def _s5_cols_kernel(x_ref, g_ref, sc_ref, sh_ref, v_ref, row_scr):
    nb, r, d = x_ref.shape
    g = v_ref.shape[0]
    cg = d // g
    tc = v_ref.shape[1] // cg
    nk = r // tc
    x = x_ref[...]
    y = x * lax.rsqrt(jnp.mean(x * x, axis=-1, keepdims=True) + RMS_EPS)
    h = ((y * g_ref[...]) * (1.0 + sc_ref[...]) + sh_ref[...]).reshape(nb * r, d)
    n_lane_blocks = row_scr.shape[0]
    gl = V7X_LANES // cg
    for c in range(n_lane_blocks):
        row_scr[c] = h[:, c * V7X_LANES:(c + 1) * V7X_LANES]
    for t in range(tc):
        for c in range(n_lane_blocks):
            z = jnp.concatenate([row_scr[c, pl.ds(t + tc * k, nb, stride=r), :] for k in range(nk)],
                                axis=0)
            v_ref[c * gl:(c + 1) * gl, t * cg:(t + 1) * cg, :] = (
                z.T.reshape(gl, cg, nk * nb).astype(v_ref.dtype))


def _s5_block_rows(n_seq, seq_len):
    return min(seq_len, S5_CHUNK * max(1, V7X_LANES // n_seq))


def _s5_cols(x, mods, n_seq, seq_len, norm_g, n_groups):
    d = x.shape[1]
    r = _s5_block_rows(n_seq, seq_len)
    cols = (r // S5_CHUNK) * n_seq
    rows = S5_CHUNK * (d // n_groups)
    n_mod = mods.shape[0]
    mod_spec = lambda j: pl.BlockSpec((n_mod, 1, d), lambda i: (0, 0, j))
    return pl.pallas_call(
        _s5_cols_kernel,
        grid=(seq_len // r,),
        in_specs=[pl.BlockSpec((n_seq, r, d), lambda i: (0, i, 0)), _full_spec((1, d)),
                  mod_spec(1), mod_spec(0)],
        out_specs=pl.BlockSpec((n_groups, rows, cols), lambda i: (0, 0, i)),
        out_shape=jax.ShapeDtypeStruct((n_groups, rows, (seq_len // S5_CHUNK) * n_seq), BF16),
        scratch_shapes=[pltpu.VMEM((d // V7X_LANES, n_seq * r, V7X_LANES), F32)],
        compiler_params=_cparams(1),
        name="s5_cols",
    )(x.reshape(n_seq, seq_len, d), norm_g.reshape(1, d), mods, mods)


def _s5_rows_kernel(y_ref, o_ref, row_scr):
    nb, r, d = o_ref.shape
    g = y_ref.shape[0]
    cg = d // g
    tc = y_ref.shape[1] // cg
    nk = r // tc
    n_lane_blocks = row_scr.shape[0]
    gl = V7X_LANES // cg
    for t in range(tc):
        for c in range(n_lane_blocks):
            z = y_ref[c * gl:(c + 1) * gl, t * cg:(t + 1) * cg, :].reshape(V7X_LANES, nk * nb).T
            for k in range(nk):
                row_scr[c, pl.ds(t + tc * k, nb, stride=r), :] = z[k * nb:(k + 1) * nb, :]
    o_ref[...] = jnp.concatenate([row_scr[c] for c in range(n_lane_blocks)],
                                 axis=-1).reshape(nb, r, d)


def _s5_rows(y, n_seq, seq_len):
    n_groups, rows, _ = y.shape
    d = n_groups * (rows // S5_CHUNK)
    r = _s5_block_rows(n_seq, seq_len)
    cols = (r // S5_CHUNK) * n_seq
    out = pl.pallas_call(
        _s5_rows_kernel,
        grid=(seq_len // r,),
        in_specs=[pl.BlockSpec((n_groups, rows, cols), lambda i: (0, 0, i))],
        out_specs=pl.BlockSpec((n_seq, r, d), lambda i: (0, i, 0)),
        out_shape=jax.ShapeDtypeStruct((n_seq, seq_len, d), F32),
        scratch_shapes=[pltpu.VMEM((d // V7X_LANES, n_seq * r, V7X_LANES), F32)],
        compiler_params=_cparams(1),
        name="s5_rows",
    )(y)
    return out.reshape(n_seq * seq_len, d)


def _s5_head_kernel(x_ref, y_ref, g_ref, sc_ref, sh_ref, d_ref, w_ref, g1_ref, o_ref):
    x = x_ref[...]
    h = _modnorm(x, g_ref[...], sc_ref[0], sh_ref[0])
    a = jax.nn.gelu(y_ref[...] + d_ref[...] * h, approximate=True)
    z = jnp.dot(a.astype(BF16), w_ref[...], preferred_element_type=F32)
    dm = z.shape[1] // 2
    o_ref[...] = x + g1_ref[0] * (z[:, :dm] * jax.nn.sigmoid(z[:, dm:]))


def _s5_head(x, y, mods, seq_len, norm_g, d_skip, w_glu):
    t, d = x.shape
    tm = _row_tile(seq_len, 512)
    return pl.pallas_call(
        _s5_head_kernel,
        grid=(t // tm,),
        in_specs=[_row_spec(tm, d), _row_spec(tm, d), _full_spec((1, d)),
                  _mod_spec(mods.shape[0], tm, seq_len, d, 1),
                  _mod_spec(mods.shape[0], tm, seq_len, d, 0),
                  _full_spec((1, d)), _full_spec(w_glu.shape),
                  _mod_spec(mods.shape[0], tm, seq_len, d, 2)],
        out_specs=_row_spec(tm, d),
        out_shape=jax.ShapeDtypeStruct((t, d), F32),
        compiler_params=_cparams(1),
        name="s5_head",
    )(x, y, norm_g.reshape(1, d), mods, mods, d_skip.reshape(1, d), w_glu, mods)


def _s5_mixer(x, ctx, mods_x, mods_c, n_seq, norm_g, a_re, a_im, log_dt, b_re, b_im, c_re, c_im,
              d_skip, w_glu):
    s_len, c_len = x.shape[0] // n_seq, ctx.shape[0] // n_seq
    g = a_re.shape[1]
    ops = [_s5_prep(a_re[k], a_im[k], log_dt[k], b_re[k], b_im[k], c_re[k], c_im[k], bool(k))
           for k in range(2)]
    vc = _s5_cols(ctx, mods_c, n_seq, c_len, norm_g, g)
    vx = _s5_cols(x, mods_x, n_seq, s_len, norm_g, g)
    y_c, y_x = _s5_core(vc, vx, ops[0], ops[1], n_seq)
    x = _s5_head(x, _s5_rows(y_x, n_seq, s_len), mods_x, s_len, norm_g, d_skip, w_glu)
    ctx = _s5_head(ctx, _s5_rows(y_c, n_seq, c_len), mods_c, c_len, norm_g, d_skip, w_glu)
    return x, ctx


def _pack_pairs(lo, hi):
    lo_bits = lax.bitcast_convert_type(lo.astype(BF16).astype(F32), U32)
    hi_bits = lax.bitcast_convert_type(hi.astype(BF16).astype(F32), U32)
    return lax.shift_right_logical(lo_bits, jnp.uint32(16)) | (hi_bits & jnp.uint32(0xFFFF0000))


def _unpack_pairs(words):
    lo = lax.bitcast_convert_type(lax.shift_left(words, jnp.uint32(16)), F32)
    hi = lax.bitcast_convert_type(words & jnp.uint32(0xFFFF0000), F32)
    return lo, hi


def _pack_row_chunks(rows, pp):
    lanes = V7X_LANES
    return [_pack_pairs(rows[:, c * lanes:(c + 1) * lanes], rows[:, (c + pp) * lanes:(c + pp + 1) * lanes])
            for c in range(pp)]


def _unpack_row_chunks(chunks):
    pairs = [_unpack_pairs(w) for w in chunks]
    return jnp.concatenate([lo for lo, _ in pairs] + [hi for _, hi in pairs], axis=-1)


def _router_kernel(x_ref, g_ref, sc_ref, sh_ref, wr_ref, br_ref, h_ref, r_ref, c_ref, *, n_groups,
                   epg):
    h = _modnorm(x_ref[...], g_ref[...], sc_ref[0], sh_ref[0])
    tm = x_ref.shape[0]
    pp = h_ref.shape[0] // tm
    for c, words in enumerate(_pack_row_chunks(h, pp)):
        h_ref[pl.ds(c, tm, stride=pp), :] = words
    logits = lax.dot_general(wr_ref[...], h, (((1,), (1,)), ((), ())), preferred_element_type=F32,
                             precision=HIGHEST) + br_ref[...]
    row = lax.broadcasted_iota(I32, logits.shape, 0)
    far = jnp.int32(1 << 20)

    def first_max(vals):
        m = jnp.max(vals, axis=0, keepdims=True)
        return m, jnp.min(jnp.where(vals == m, row, far), axis=0, keepdims=True)

    is_group = row < n_groups
    gl = jnp.where(is_group, logits, NEG_BIG)
    gmax, gidx = first_max(gl)
    gsum = jnp.sum(jnp.where(is_group, jnp.exp(gl - gmax), 0.0), axis=0, keepdims=True)
    g_w = 1.0 / gsum
    lo = n_groups + gidx * epg
    le = jnp.where((row >= lo) & (row < lo + epg), logits, NEG_BIG)
    m1, i1 = first_max(le)
    m2, i2 = first_max(jnp.where(row == i1, NEG_BIG, le))
    ratio = jnp.exp(m2 - m1)
    w1 = g_w / (1.0 + ratio)
    w2 = g_w * ratio / (1.0 + ratio)
    e1 = (i1 - n_groups).astype(F32)
    e2 = (i2 - n_groups).astype(F32)
    out_row = lax.broadcasted_iota(I32, r_ref.shape, 0)
    r_ref[...] = jnp.where(out_row == 0, e1, jnp.where(out_row == 1, e2, jnp.where(
        out_row == 2, w1, jnp.where(out_row == 3, w2, 0.0))))
    chosen = ((row == i1) | (row == i2)).astype(F32)
    c_ref[0] = jnp.broadcast_to(jnp.sum(chosen, axis=1, keepdims=True), c_ref.shape[1:])


def _router(x, mods, seq_len, norm_g, wr, br, n_groups, epg):
    t, d = x.shape
    tm = _row_tile(seq_len, 1024)
    pp = d // (2 * V7X_LANES)
    n_logit = wr.shape[0]
    return pl.pallas_call(
        functools.partial(_router_kernel, n_groups=n_groups, epg=epg),
        grid=(t // tm,),
        in_specs=[_row_spec(tm, d), _full_spec((1, d)),
                  _mod_spec(mods.shape[0], tm, seq_len, d, 4),
                  _mod_spec(mods.shape[0], tm, seq_len, d, 3),
                  _full_spec(wr.shape), _full_spec(br.shape)],
        out_specs=[pl.BlockSpec((tm * pp, V7X_LANES), lambda i: (i, 0)),
                   pl.BlockSpec((V7X_SUBLANES, tm), lambda i: (0, i)),
                   pl.BlockSpec((1, n_logit, V7X_LANES), lambda i: (i, 0, 0))],
        out_shape=[jax.ShapeDtypeStruct((t * pp, V7X_LANES), U32),
                   jax.ShapeDtypeStruct((V7X_SUBLANES, t), F32),
                   jax.ShapeDtypeStruct((t // tm, n_logit, V7X_LANES), F32)],
        compiler_params=_cparams(1),
        name="moe_router",
    )(x, norm_g.reshape(1, d), mods, mods, wr, br)


def _dispatch_lists(route, counts, n_block, cap, pp):
    t = route.shape[1]
    n_sb = t // n_block
    n_assign = n_block * TOP_K
    ids = route[0:TOP_K].T.astype(I32).reshape(n_sb, n_assign)
    asg = jnp.broadcast_to(jnp.arange(n_assign, dtype=I32)[None, :], (n_sb, n_assign))
    _, asg_s = lax.sort((ids, asg), dimension=1, is_stable=True, num_keys=1)
    offs = jnp.cumsum(counts, axis=1) - counts
    asg_s = jnp.pad(asg_s, ((0, 0), (0, cap - n_assign)), constant_values=n_assign)
    src = jnp.minimum(asg_s // TOP_K, n_block - 1) * pp
    return (counts.reshape(-1), offs.reshape(-1),
            src.reshape(n_sb, 1, cap), (asg_s * pp).reshape(n_sb, 1, cap))


def _moe_kernel(cnt_ref, off_ref, src_ref, dst_ref, h_ref, w13_ref, w2_ref, o_ref,
                lhs_a, lhs_b, ys_a, ys_b, *, n_exp, tile):
    sb = pl.program_id(0)
    eb = pl.program_id(1)
    n_local = w13_ref.shape[0]
    pp = w13_ref.shape[1] // (2 * V7X_LANES)
    de = w2_ref.shape[1]
    scratch = ((lhs_a, ys_a), (lhs_b, ys_b))

    def slab(row):
        return pl.ds(pl.multiple_of(row, pp), pp)

    @pl.when(eb == 0)
    def _():
        n_live = h_ref.shape[0] * TOP_K
        spare = o_ref.shape[1] - n_live
        o_ref[0, pl.ds(n_live, spare), :] = jnp.zeros((spare, V7X_LANES), U32)

    def run_tiles(q, base, sizes):
        starts = [base + sum(sizes[:k]) for k in range(len(sizes))]
        for (lhs_scr, _), start, size in zip(scratch, starts, sizes):
            stride = size + 1
            for mi in range(size):
                lhs_scr[pl.ds(mi, pp, stride=stride), :] = h_ref[slab(src_ref[0, 0, start + mi]), :]
        for (lhs_scr, ys_scr), size in zip(scratch, sizes):
            stride = size + 1
            lhs = _unpack_row_chunks([lhs_scr[pl.ds(c * stride, size), :] for c in range(pp)])
            hid = jnp.dot(lhs.astype(BF16), w13_ref[q], preferred_element_type=F32)
            act = _silu(hid[:, :de]) * hid[:, de:]
            ys = jnp.dot(act.astype(BF16), w2_ref[q], preferred_element_type=F32)
            for c, words in enumerate(_pack_row_chunks(ys, pp)):
                ys_scr[pl.ds(c * stride, size), :] = words
        for (_, ys_scr), start, size in zip(scratch, starts, sizes):
            stride = size + 1
            for mi in range(size):
                o_ref[0, slab(dst_ref[0, 0, start + mi]), :] = ys_scr[pl.ds(mi, pp, stride=stride), :]

    half = tile // 2

    def one_expert(q, carry):
        e = eb * n_local + q
        count = cnt_ref[sb * n_exp + e]
        seg = off_ref[sb * n_exp + e]
        n_pairs = lax.div(count, 2 * tile)

        def pair(j, c):
            run_tiles(q, seg + j * 2 * tile, (tile, tile))
            return c
        lax.fori_loop(0, n_pairs, pair, 0)

        rem = count - n_pairs * 2 * tile
        rem_base = seg + n_pairs * 2 * tile
        for hi, sizes in ((half, (half,)), (tile, (tile,)), (tile + half, (tile, half)),
                          (2 * tile, (tile, tile))):
            @pl.when((rem > hi - half) & (rem <= hi))
            def _():
                run_tiles(q, rem_base, sizes)
        return carry

    lax.fori_loop(0, n_local, one_expert, 0)


def _moe_tile(n_block, n_exp):
    return max(4 * V7X_SUBLANES, n_block * TOP_K // n_exp)


def _moe_experts(h, route, counts, w13, w2, n_block):
    n_exp, d, de2 = w13.shape
    de = de2 // 2
    pp = d // (2 * V7X_LANES)
    n_sb = h.shape[0] // (n_block * pp)
    n_assign = n_block * TOP_K
    tile = _moe_tile(n_block, n_exp)
    cap = -(-(n_assign + 2 * tile) // V7X_LANES) * V7X_LANES
    cnt, off, src_list, dst_list = _dispatch_lists(route, counts, n_block, cap, pp)
    out_slabs = n_assign + V7X_SUBLANES
    eb = MOE_EXPERTS_PER_STEP
    list_spec = pl.BlockSpec((1, 1, cap), lambda s, e, *_: (s, 0, 0), memory_space=pltpu.SMEM)
    grid_spec = pltpu.PrefetchScalarGridSpec(
        num_scalar_prefetch=2,
        grid=(n_sb, n_exp // eb),
        in_specs=[list_spec, list_spec,
                  pl.BlockSpec((n_block * pp, V7X_LANES), lambda s, e, *_: (s, 0),
                               pipeline_mode=pl.Buffered(1)),
                  pl.BlockSpec((eb, d, de2), lambda s, e, *_: (e, 0, 0)),
                  pl.BlockSpec((eb, de, d), lambda s, e, *_: (e, 0, 0))],
        out_specs=pl.BlockSpec((1, out_slabs * pp, V7X_LANES), lambda s, e, *_: (s, 0, 0)),
        scratch_shapes=[pltpu.VMEM((pp * (tile + 1), V7X_LANES), U32)] * 4)
    return pl.pallas_call(
        functools.partial(_moe_kernel, n_exp=n_exp, tile=tile),
        grid_spec=grid_spec,
        out_shape=jax.ShapeDtypeStruct((n_sb, out_slabs * pp, V7X_LANES), U32),
        compiler_params=_cparams(2),
        name="moe_experts",
    )(cnt, off, src_list, dst_list, h, w13, w2)


def _residual_kernel(x_ref, y_ref, w_ref, g2_ref, fg_ref, o_ref, *, final_norm):
    tm = x_ref.shape[0]
    wts = w_ref[...]
    pp = y_ref.shape[1] // (tm * TOP_K)
    y = sum(wts[:, k:k + 1]
            * _unpack_row_chunks([y_ref[0, pl.ds(k * pp + c, tm, stride=TOP_K * pp), :]
                                  for c in range(pp)])
            for k in range(TOP_K))
    x = x_ref[...] + g2_ref[0] * y
    if final_norm:
        x = x * lax.rsqrt(jnp.mean(x * x, axis=-1, keepdims=True) + RMS_EPS) * fg_ref[...]
    o_ref[...] = x


def _residual(x, y, wts, mods, seq_len, n_block, final_g, final_norm):
    t, d = x.shape
    tm = _row_tile(min(seq_len, n_block), 1024)
    per_sb = n_block // tm
    pp = d // (2 * V7X_LANES)
    return pl.pallas_call(
        functools.partial(_residual_kernel, final_norm=final_norm),
        grid=(t // tm,),
        in_specs=[_row_spec(tm, d),
                  pl.BlockSpec((1, tm * TOP_K * pp, V7X_LANES), lambda i: (i // per_sb, i % per_sb, 0)),
                  _row_spec(tm, TOP_K),
                  _mod_spec(mods.shape[0], tm, seq_len, d, 5), _full_spec((1, d))],
        out_specs=_row_spec(tm, d),
        out_shape=jax.ShapeDtypeStruct((t, d), F32),
        compiler_params=_cparams(1),
        name="moe_residual",
    )(x, y, wts, mods, final_g.reshape(1, d))


def _moe_block(t):
    n = min(t, 4096)
    while t % n:
        n //= 2
    return n


def _moe(x, mods, seq_len, norm_g, wr, br, w13, w2, n_groups, final_g, final_norm):
    n_exp = w13.shape[0]
    n_block = _moe_block(x.shape[0])
    h, route, tile_counts = _router(x, mods, seq_len, norm_g, wr, br, n_groups, n_exp // n_groups)
    counts = tile_counts[:, n_groups:n_groups + n_exp, 0].reshape(x.shape[0] // n_block, -1, n_exp)
    counts = jnp.sum(counts, axis=1).astype(I32)
    y = _moe_experts(h, route, counts, w13, w2, n_block)
    return _residual(x, y, route[TOP_K:2 * TOP_K].T, mods, seq_len, n_block, final_g, final_norm)


def kernel(x, c, ctx, c_ctx, ada_w, ada_b, norm1_g, norm2_g, conf_w_in, conf_dw, conf_dw_b, conf_ln_g, conf_ln_b, conf_w_out, sc_w_in, sc_conv, sc_w_out, s5_a_re, s5_a_im, s5_log_dt, s5_b_re, s5_b_im, s5_c_re, s5_c_im, s5_d, s5_w_glu, moe_wg, moe_bg, moe_we, moe_be, moe_w13, moe_w2, final_g):
    b, s, d = x.shape
    lc = ctx.shape[1]
    depth = ada_w.shape[0]
    n_groups = moe_wg.shape[-1]
    n_exp = moe_we.shape[-1]
    assert s % GRID_W == 0

    rows = (b + 1 + V7X_SUBLANES - 1) // V7X_SUBLANES * V7X_SUBLANES
    cin = jnp.zeros((rows, d), F32).at[:b].set(c).at[b].set(c_ctx)
    table = _ada_table(cin, ada_w, ada_b)

    xs = x.reshape(b * s, d)
    cs = ctx.reshape(b * lc, d)
    for i in range(depth):
        kind, j = i % N_MIXERS, i // N_MIXERS
        update_ctx = i < depth - 1
        mods_x = table[i, :b].reshape(b, 1, 6 * d)
        mods_c = table[i, b].reshape(1, 1, 6 * d)
        if kind == 0:
            args = (norm1_g[i], conf_w_in[j].astype(BF16), conf_dw[j], conf_dw_b[j],
                    conf_ln_g[j], conf_ln_b[j], conf_w_out[j].astype(BF16))
            xs = _conformer(xs, mods_x, s, GRID_W, *args)
            if update_ctx:
                cs = _conformer(cs, mods_c, lc, 1, *args)
        elif kind == 1:
            args = (norm1_g[i], sc_w_in[j].astype(BF16), sc_conv[j], sc_w_out[j].astype(BF16))
            xs = _short_conv(xs, mods_x, s, GRID_W, *args)
            if update_ctx:
                cs = _short_conv(cs, mods_c, lc, lc, *args)
        else:
            xs, cs_new = _s5_mixer(xs, cs, mods_x, mods_c, b, norm1_g[i], s5_a_re[j], s5_a_im[j],
                                   s5_log_dt[j], s5_b_re[j], s5_b_im[j], s5_c_re[j], s5_c_im[j],
                                   s5_d[j], s5_w_glu[j].astype(BF16))
            if update_ctx:
                cs = cs_new

        n_logit = -(-(n_groups + n_exp) // V7X_SUBLANES) * V7X_SUBLANES
        wr = jnp.zeros((n_logit, d), F32).at[:n_groups].set(moe_wg[i].T)
        wr = wr.at[n_groups:n_groups + n_exp].set(moe_we[i].T)
        br = jnp.zeros((n_logit, 1), F32).at[:n_groups, 0].set(moe_bg[i])
        br = br.at[n_groups:n_groups + n_exp, 0].set(moe_be[i])
        moe_args = (norm2_g[i], wr, br, moe_w13[i].astype(BF16), moe_w2[i].astype(BF16), n_groups,
                    final_g)
        xs = _moe(xs, mods_x, s, *moe_args, final_norm=(i == depth - 1))
        if update_ctx:
            cs = _moe(cs, mods_c, lc, *moe_args, final_norm=False)
    return xs.reshape(b, s, d)
```

```python
import functools

import jax
import jax.numpy as jnp
from jax import lax
from jax.experimental import pallas as pl
from jax.experimental.pallas import tpu as pltpu

F32 = jnp.float32
BF16 = jnp.bfloat16
I32 = jnp.int32
U32 = jnp.uint32
HIGHEST = lax.Precision.HIGHEST

GRID_W = 64
N_MIXERS = 3
TOP_K = 2
RMS_EPS = 1e-6
LN_EPS = 1e-5
S5_DT_FLOOR = -1e-4

V7X_VMEM_BYTES = 64 * 1024 * 1024
V7X_LANES = 128
V7X_SUBLANES = 8
VMEM_LIMIT_BYTES = V7X_VMEM_BYTES - 6 * 1024 * 1024

S5_CHUNK = 16
MOE_EXPERTS_PER_STEP = 2
MOE_REMAINDER_STEPS = 4
NEG_BIG = -1e30


def _cparams(n_axes):
    return pltpu.CompilerParams(dimension_semantics=("arbitrary",) * n_axes,
                                vmem_limit_bytes=VMEM_LIMIT_BYTES)


def _row_tile(seq_len, want):
    t = min(seq_len, want)
    while seq_len % t or t % V7X_SUBLANES:
        t -= 1
    return t


def _modnorm(x, g, sc, sh):
    y = x * lax.rsqrt(jnp.mean(x * x, axis=-1, keepdims=True) + RMS_EPS)
    return (y * g) * (1.0 + sc) + sh


def _silu(v):
    return v * jax.nn.sigmoid(v)


def _mod_spec(n_mod, tm, seq_len, d, j):
    if n_mod == 1:
        return pl.BlockSpec((1, 1, d), lambda t: (0, 0, j))
    return pl.BlockSpec((1, 1, d), lambda t: ((t * tm) // seq_len, 0, j))


def _row_spec(tm, d):
    return pl.BlockSpec((tm, d), lambda t: (t, 0))


def _full_spec(shape):
    nd = len(shape)
    return pl.BlockSpec(shape, lambda *_: (0,) * nd)


def _ada_kernel(c_ref, w_ref, b_ref, o_ref):
    o_ref[0] = jnp.dot(_silu(c_ref[...]), w_ref[0], preferred_element_type=F32,
                       precision=HIGHEST) + b_ref[0]


def _ada_table(cin, ada_w, ada_b):
    depth, d, d6 = ada_w.shape
    r = cin.shape[0]
    tn = d6 // 6
    return pl.pallas_call(
        _ada_kernel,
        grid=(depth, d6 // tn),
        in_specs=[pl.BlockSpec((r, d), lambda i, j: (0, 0)),
                  pl.BlockSpec((1, d, tn), lambda i, j: (i, 0, j)),
                  pl.BlockSpec((1, 1, tn), lambda i, j: (i, 0, j))],
        out_specs=pl.BlockSpec((1, r, tn), lambda i, j: (i, 0, j)),
        out_shape=jax.ShapeDtypeStruct((depth, r, d6), F32),
        compiler_params=_cparams(2),
        name="ada_table",
    )(cin, ada_w, ada_b.reshape(depth, 1, d6))


def _conf_in_kernel(x_ref, g_ref, sc_ref, sh_ref, w_ref, z_ref):
    h = _modnorm(x_ref[...], g_ref[...], sc_ref[0], sh_ref[0])
    y = jnp.dot(h.astype(BF16), w_ref[...], preferred_element_type=F32)
    ci = y.shape[1] // 2
    z_ref[...] = (y[:, :ci] * jax.nn.sigmoid(y[:, ci:])).astype(z_ref.dtype)


def _conf_in(x, mods, seq_len, norm_g, w_in):
    t, d = x.shape
    tm = _row_tile(seq_len, 512)
    ci = w_in.shape[1] // 2
    return pl.pallas_call(
        _conf_in_kernel,
        grid=(t // tm,),
        in_specs=[_row_spec(tm, d), _full_spec((1, d)),
                  _mod_spec(mods.shape[0], tm, seq_len, d, 1),
                  _mod_spec(mods.shape[0], tm, seq_len, d, 0),
                  _full_spec(w_in.shape)],
        out_specs=_row_spec(tm, ci),
        out_shape=jax.ShapeDtypeStruct((t, ci), BF16),
        compiler_params=_cparams(1),
        name="conf_in",
    )(x, norm_g.reshape(1, d), mods, mods, w_in)


def _dwconv_kernel(z_ref, w_ref, b_ref, o_ref, src_scr, *, stride, chunk):
    seq_len, cb = z_ref.shape
    taps = w_ref.shape[0]
    half = taps // 2
    aligned = stride % chunk == 0
    pad = 0 if aligned else half * stride
    if pad:
        src_scr[pl.ds(0, pad), :] = jnp.zeros((pad, cb), F32)
        src_scr[pl.ds(pad + seq_len, pad), :] = jnp.zeros((pad, cb), F32)
    src_scr[pl.ds(pad, seq_len), :] = z_ref[...].astype(F32)
    w = w_ref[...]
    bias = b_ref[...]
    for r0 in range(0, seq_len, chunk):
        acc = jnp.broadcast_to(bias, (chunk, cb))
        for k in range(taps):
            lo = r0 + (k - half) * stride
            if aligned and (lo < 0 or lo + chunk > seq_len):
                continue
            acc = acc + w[k:k + 1, :] * src_scr[pl.ds(lo + pad, chunk), :]
        o_ref[pl.ds(r0, chunk), :] = acc.astype(o_ref.dtype)


def _dwconv(z, seq_len, stride, w, b):
    t, c = z.shape
    taps = w.shape[0]
    cb = min(c, 2 * V7X_LANES)
    chunk = _row_tile(seq_len, 64)
    pad = 0 if stride % chunk == 0 else (taps // 2) * stride
    return pl.pallas_call(
        functools.partial(_dwconv_kernel, stride=stride, chunk=chunk),
        grid=(t // seq_len, c // cb),
        in_specs=[pl.BlockSpec((seq_len, cb), lambda s, j: (s, j)),
                  pl.BlockSpec((taps, cb), lambda s, j: (0, j)),
                  pl.BlockSpec((1, cb), lambda s, j: (0, j))],
        out_specs=pl.BlockSpec((seq_len, cb), lambda s, j: (s, j)),
        out_shape=jax.ShapeDtypeStruct((t, c), BF16),
        scratch_shapes=[pltpu.VMEM((seq_len + 2 * pad, cb), F32)],
        compiler_params=_cparams(2),
        name="dwconv",
    )(z, w, b.reshape(1, c))


def _conf_out_kernel(z_ref, lg_ref, lb_ref, w_ref, x_ref, g1_ref, o_ref):
    z = z_ref[...].astype(F32)
    mu = jnp.mean(z, axis=-1, keepdims=True)
    zc = z - mu
    var = jnp.mean(zc * zc, axis=-1, keepdims=True)
    y = zc * lax.rsqrt(var + LN_EPS) * lg_ref[...] + lb_ref[...]
    m = jnp.dot(_silu(y).astype(BF16), w_ref[...], preferred_element_type=F32)
    o_ref[...] = x_ref[...] + g1_ref[0] * m


def _conf_out(z, x, mods, seq_len, ln_g, ln_b, w_out):
    t, d = x.shape
    ci = z.shape[1]
    tm = _row_tile(seq_len, 1024)
    return pl.pallas_call(
        _conf_out_kernel,
        grid=(t // tm,),
        in_specs=[_row_spec(tm, ci), _full_spec((1, ci)), _full_spec((1, ci)),
                  _full_spec(w_out.shape), _row_spec(tm, d),
                  _mod_spec(mods.shape[0], tm, seq_len, d, 2)],
        out_specs=_row_spec(tm, d),
        out_shape=jax.ShapeDtypeStruct((t, d), F32),
        compiler_params=_cparams(1),
        name="conf_out",
    )(z, ln_g.reshape(1, ci), ln_b.reshape(1, ci), w_out, x, mods)


def _conformer(x, mods, seq_len, stride, norm_g, w_in, dw, dw_b, ln_g, ln_b, w_out):
    z = _conf_in(x, mods, seq_len, norm_g, w_in)
    z = _dwconv(z, seq_len, stride, dw, dw_b)
    return _conf_out(z, x, mods, seq_len, ln_g, ln_b, w_out)


def _sc_kernel(x_ref, g_ref, sc_ref, sh_ref, win_ref, cw_ref, wout_ref, g1_ref, o_ref, *, period):
    x = x_ref[...]
    tm, d = x.shape
    h = _modnorm(x, g_ref[...], sc_ref[0], sh_ref[0])
    y = jnp.dot(h.astype(BF16), win_ref[...], preferred_element_type=F32)
    gb, gc, v = y[:, :d], y[:, d:2 * d], y[:, 2 * d:]
    u = gc * v
    pos = lax.broadcasted_iota(I32, (tm, 1), 0) % period
    u_prev = jnp.where(pos == 0, 0.0, pltpu.roll(u, 1, 0))
    u_next = jnp.where(pos == period - 1, 0.0, pltpu.roll(u, tm - 1, 0))
    cw = cw_ref[...]
    conv = cw[0:1, :] * u_prev + cw[1:2, :] * u + cw[2:3, :] * u_next
    m = jnp.dot((gb * conv).astype(BF16), wout_ref[...], preferred_element_type=F32)
    o_ref[...] = x + g1_ref[0] * m


def _short_conv(x, mods, seq_len, period, norm_g, w_in, conv_w, w_out):
    t, d = x.shape
    tm = _row_tile(seq_len, 512)
    assert tm % period == 0 and conv_w.shape[0] == 3
    return pl.pallas_call(
        functools.partial(_sc_kernel, period=period),
        grid=(t // tm,),
        in_specs=[_row_spec(tm, d), _full_spec((1, d)),
                  _mod_spec(mods.shape[0], tm, seq_len, d, 1),
                  _mod_spec(mods.shape[0], tm, seq_len, d, 0),
                  _full_spec(w_in.shape), _full_spec(conv_w.shape), _full_spec(w_out.shape),
                  _mod_spec(mods.shape[0], tm, seq_len, d, 2)],
        out_specs=_row_spec(tm, d),
        out_shape=jax.ShapeDtypeStruct((t, d), F32),
        compiler_params=_cparams(1),
        name="short_conv",
    )(x, norm_g.reshape(1, d), mods, mods, w_in, conv_w, w_out, mods)


def _s5_prep_kernel(are_ref, aim_ref, ldt_ref, bre_ref, bim_ref, cre_ref, cim_ref,
                    a16r_ref, a16i_ref, winr_ref, wini_ref, woutr_ref, wouti_ref, k_ref, *, reverse):
    a_re = jnp.minimum(are_ref[0], S5_DT_FLOOR)
    a_im = aim_ref[0]
    dt = jnp.exp(ldt_ref[0])
    b_re, b_im = bre_ref[0], bim_ref[0]
    c_re, c_im = cre_ref[0], cim_ref[0]
    cg = b_re.shape[1]
    tc = S5_CHUNK

    def power(n):
        mag = jnp.exp((n * dt) * a_re)
        ang = (n * dt) * a_im
        return mag * jnp.cos(ang), mag * jnp.sin(ang)

    abar_re, abar_im = power(1)
    den = a_re * a_re + a_im * a_im
    n_re = abar_re - 1.0
    n_im = abar_im
    k_re = (n_re * a_re + n_im * a_im) / den
    k_im = (n_im * a_re - n_re * a_im) / den
    bb_re = k_re * b_re - k_im * b_im
    bb_im = k_re * b_im + k_im * b_re

    e16r, e16i = power(tc)
    a16r_ref[0] = e16r
    a16i_ref[0] = e16i
    for t in range(tc):
        er, ei = power(t if reverse else tc - 1 - t)
        winr_ref[0, :, t * cg:(t + 1) * cg, :] = er * bb_re - ei * bb_im
        wini_ref[0, :, t * cg:(t + 1) * cg, :] = er * bb_im + ei * bb_re
        er, ei = power(tc - t if reverse else t + 1)
        woutr_ref[0, :, t * cg:(t + 1) * cg, :] = c_re * er - c_im * ei
        wouti_ref[0, :, t * cg:(t + 1) * cg, :] = -(c_re * ei + c_im * er)
        er, ei = power(t)
        m_re = c_re * er - c_im * ei
        m_im = c_re * ei + c_im * er
        k_ref[0, :, t * cg:(t + 1) * cg, :] = (
            jnp.einsum('gap,gbp->gab', m_re, bb_re, preferred_element_type=F32, precision=HIGHEST)
            - jnp.einsum('gap,gbp->gab', m_im, bb_im, preferred_element_type=F32, precision=HIGHEST))


def _s5_prep(a_re, a_im, log_dt, b_re, b_im, c_re, c_im, reverse):
    g, p = a_re.shape
    cg = b_re.shape[2]
    gb = min(g, 8)
    rows = S5_CHUNK * cg
    a4 = lambda a: a.reshape(1, g, 1, p)
    bt = lambda b: jnp.swapaxes(b, 1, 2).reshape(1, g, cg, p)
    spec_a = pl.BlockSpec((1, gb, 1, p), lambda i: (0, i, 0, 0))
    spec_b = pl.BlockSpec((1, gb, cg, p), lambda i: (0, i, 0, 0))
    spec_w = pl.BlockSpec((1, gb, rows, p), lambda i: (0, i, 0, 0))
    outs = pl.pallas_call(
        functools.partial(_s5_prep_kernel, reverse=reverse),
        grid=(g // gb,),
        in_specs=[spec_a, spec_a, pl.BlockSpec((1, gb, 1, 1), lambda i: (0, i, 0, 0)),
                  spec_b, spec_b, spec_b, spec_b],
        out_specs=[spec_a, spec_a, spec_w, spec_w, spec_w, spec_w,
                   pl.BlockSpec((1, gb, rows, cg), lambda i: (0, i, 0, 0))],
        out_shape=[jax.ShapeDtypeStruct((1, g, 1, p), F32)] * 2
        + [jax.ShapeDtypeStruct((1, g, rows, p), F32)] * 4
        + [jax.ShapeDtypeStruct((1, g, rows, cg), F32)],
        compiler_params=_cparams(1),
        name="s5_prep",
    )(a4(a_re), a4(a_im), log_dt.reshape(1, g, 1, 1), bt(b_re), bt(b_im),
      c_re.reshape(1, g, cg, p), c_im.reshape(1, g, cg, p))
    a16r, a16i, winr, wini, woutr, wouti, kk = [o[0] for o in outs]
    return a16r, a16i, winr, wini, woutr, wouti, kk


def _pair_blockdiag(w):
    g, r, p = w.shape
    w = w.reshape(g // 2, 2, r, p)
    z = jnp.zeros_like(w[:, 0])
    top = jnp.concatenate([w[:, 0], z], axis=-1)
    bot = jnp.concatenate([z, w[:, 1]], axis=-1)
    return jnp.concatenate([top, bot], axis=1)


def _toeplitz(kk, cg, reverse):
    g = kk.shape[0]
    tc = S5_CHUNK
    k5 = kk.reshape(g, tc, cg, cg)
    t_in = jnp.arange(tc)[None, :]
    t_out = jnp.arange(tc)[:, None]
    lag = (t_in - t_out) if reverse else (t_out - t_in)
    blocks = jnp.where((lag >= 0)[None, :, :, None, None], k5[:, jnp.clip(lag, 0, tc - 1)], 0.0)
    return blocks.transpose(0, 1, 3, 2, 4).reshape(g, tc * cg, tc * cg)


def _s5_core_kernel(vc_ref, vx_ref, a16r_ref, a16i_ref, winr_ref, wini_ref, woutr_ref, wouti_ref,
                    tz_ref, yc_ref, yx_ref, sre, sim, hre, him, *, n_seq):
    r = vc_ref.shape[1]
    n_cc = vc_ref.shape[2]
    n_col = n_cc + vx_ref.shape[2]
    n_chunks = n_col // n_seq
    n_ctx_chunks = n_cc // n_seq
    v = jnp.concatenate([jnp.concatenate([vc_ref[half], vx_ref[half]], axis=-1)
                         for half in range(2)], axis=0)
    for direction in range(2):
        sre[...] = lax.dot_general(v, winr_ref[direction, 0], (((0,), (0,)), ((), ())),
                                   preferred_element_type=F32)
        sim[...] = lax.dot_general(v, wini_ref[direction, 0], (((0,), (0,)), ((), ())),
                                   preferred_element_type=F32)
        ar = a16r_ref[direction, 0]
        ai = a16i_ref[direction, 0]

        def step(k, carry):
            h_r, h_i = carry
            rows = pl.ds(pl.multiple_of(k * n_seq, n_seq), n_seq)
            hre[rows, :] = h_r
            him[rows, :] = h_i
            return (ar * h_r - ai * h_i + sre[rows, :], ar * h_i + ai * h_r + sim[rows, :])

        zero = jnp.zeros((n_seq, sre.shape[1]), F32)
        if direction == 0:
            lax.fori_loop(0, n_chunks, step, (zero, zero))
        else:
            mid = lax.fori_loop(0, n_ctx_chunks,
                                lambda i, c: step(n_ctx_chunks - 1 - i, c), (zero, zero))
            lax.fori_loop(0, n_chunks - n_ctx_chunks,
                          lambda i, c: step(n_chunks - 1 - i, c), mid)

        h_r = hre[...].astype(BF16)
        h_i = him[...].astype(BF16)
        for half in range(2):
            rows = pl.ds(half * r, r)
            part = (lax.dot_general(woutr_ref[direction, 0, rows, :], h_r,
                                    (((1,), (1,)), ((), ())), preferred_element_type=F32)
                    + lax.dot_general(wouti_ref[direction, 0, rows, :], h_i,
                                      (((1,), (1,)), ((), ())), preferred_element_type=F32)
                    + jnp.dot(tz_ref[direction, half], v[half * r:(half + 1) * r, :],
                              preferred_element_type=F32))
            if direction == 0:
                yc_ref[half] = part[:, :n_cc]
                yx_ref[half] = part[:, n_cc:]
            else:
                yc_ref[half] = yc_ref[half] + part[:, :n_cc]
                yx_ref[half] = yx_ref[half] + part[:, n_cc:]


def _s5_core(vc, vx, ops_fwd, ops_bwd, n_seq):
    g, r, n_cc = vc.shape
    n_cx = vx.shape[2]
    cg = r // S5_CHUNK
    stacked = []
    for idx in range(6):
        stacked.append(jnp.stack([ops_fwd[idx], ops_bwd[idx]]))
    a16r, a16i, winr, wini, woutr, wouti = stacked
    p = a16r.shape[-1]
    pair_vec = lambda a: a.reshape(2, g // 2, 1, 2 * p)
    pair_mat = lambda w: jnp.stack([_pair_blockdiag(w[0]), _pair_blockdiag(w[1])]).astype(BF16)
    tz = jnp.stack([_toeplitz(ops_fwd[6], cg, False), _toeplitz(ops_bwd[6], cg, True)]).astype(BF16)
    vec_spec = pl.BlockSpec((2, 1, 1, 2 * p), lambda i: (0, i, 0, 0))
    mat_spec = pl.BlockSpec((2, 1, 2 * r, 2 * p), lambda i: (0, i, 0, 0))
    col_spec = lambda n: pl.BlockSpec((2, r, n), lambda i: (i, 0, 0))
    return pl.pallas_call(
        functools.partial(_s5_core_kernel, n_seq=n_seq),
        grid=(g // 2,),
        in_specs=[col_spec(n_cc), col_spec(n_cx),
                  vec_spec, vec_spec, mat_spec, mat_spec, mat_spec, mat_spec,
                  pl.BlockSpec((2, 2, r, r), lambda i: (0, i, 0, 0))],
        out_specs=[col_spec(n_cc), col_spec(n_cx)],
        out_shape=[jax.ShapeDtypeStruct((g, r, n_cc), F32), jax.ShapeDtypeStruct((g, r, n_cx), F32)],
        scratch_shapes=[pltpu.VMEM((n_cc + n_cx, 2 * p), F32)] * 4,
        compiler_params=_cparams(1),
        name="s5_core",
    )(vc, vx, pair_vec(a16r), pair_vec(a16i), pair_mat(winr), pair_mat(wini),
      pair_mat(woutr), pair_mat(wouti), tz)


def _s5_cols_kernel(x_ref, g_ref, sc_ref, sh_ref, v_ref, row_scr):
    nb, r, d = x_ref.shape
    g = v_ref.shape[0]
    cg = d // g
    tc = v_ref.shape[1] // cg
    nk = r // tc
    x = x_ref[...]
    y = x * lax.rsqrt(jnp.mean(x * x, axis=-1, keepdims=True) + RMS_EPS)
    h = ((y * g_ref[...]) * (1.0 + sc_ref[...]) + sh_ref[...]).reshape(nb * r, d)
    n_lane_blocks = row_scr.shape[0]
    gl = V7X_LANES // cg
    for c in range(n_lane_blocks):
        row_scr[c] = h[:, c * V7X_LANES:(c + 1) * V7X_LANES]
    for t in range(tc):
        for c in range(n_lane_blocks):
            z = jnp.concatenate([row_scr[c, pl.ds(t + tc * k, nb, stride=r), :] for k in range(nk)],
                                axis=0)
            v_ref[c * gl:(c + 1) * gl, t * cg:(t + 1) * cg, :] = (
                z.T.reshape(gl, cg, nk * nb).astype(v_ref.dtype))


def _s5_block_rows(n_seq, seq_len):
    return min(seq_len, S5_CHUNK * max(1, V7X_LANES // n_seq))


def _s5_cols(x, mods, n_seq, seq_len, norm_g, n_groups):
    d = x.shape[1]
    r = _s5_block_rows(n_seq, seq_len)
    cols = (r // S5_CHUNK) * n_seq
    rows = S5_CHUNK * (d // n_groups)
    n_mod = mods.shape[0]
    mod_spec = lambda j: pl.BlockSpec((n_mod, 1, d), lambda i: (0, 0, j))
    return pl.pallas_call(
        _s5_cols_kernel,
        grid=(seq_len // r,),
        in_specs=[pl.BlockSpec((n_seq, r, d), lambda i: (0, i, 0)), _full_spec((1, d)),
                  mod_spec(1), mod_spec(0)],
        out_specs=pl.BlockSpec((n_groups, rows, cols), lambda i: (0, 0, i)),
        out_shape=jax.ShapeDtypeStruct((n_groups, rows, (seq_len // S5_CHUNK) * n_seq), BF16),
        scratch_shapes=[pltpu.VMEM((d // V7X_LANES, n_seq * r, V7X_LANES), F32)],
        compiler_params=_cparams(1),
        name="s5_cols",
    )(x.reshape(n_seq, seq_len, d), norm_g.reshape(1, d), mods, mods)


def _s5_rows_kernel(y_ref, o_ref, row_scr):
    nb, r, d = o_ref.shape
    g = y_ref.shape[0]
    cg = d // g
    tc = y_ref.shape[1] // cg
    nk = r // tc
    n_lane_blocks = row_scr.shape[0]
    gl = V7X_LANES // cg
    for t in range(tc):
        for c in range(n_lane_blocks):
            z = y_ref[c * gl:(c + 1) * gl, t * cg:(t + 1) * cg, :].reshape(V7X_LANES, nk * nb).T
            for k in range(nk):
                row_scr[c, pl.ds(t + tc * k, nb, stride=r), :] = z[k * nb:(k + 1) * nb, :]
    o_ref[...] = jnp.concatenate([row_scr[c] for c in range(n_lane_blocks)],
                                 axis=-1).reshape(nb, r, d)


def _s5_rows(y, n_seq, seq_len):
    n_groups, rows, _ = y.shape
    d = n_groups * (rows // S5_CHUNK)
    r = _s5_block_rows(n_seq, seq_len)
    cols = (r // S5_CHUNK) * n_seq
    out = pl.pallas_call(
        _s5_rows_kernel,
        grid=(seq_len // r,),
        in_specs=[pl.BlockSpec((n_groups, rows, cols), lambda i: (0, 0, i))],
        out_specs=pl.BlockSpec((n_seq, r, d), lambda i: (0, i, 0)),
        out_shape=jax.ShapeDtypeStruct((n_seq, seq_len, d), F32),
        scratch_shapes=[pltpu.VMEM((d // V7X_LANES, n_seq * r, V7X_LANES), F32)],
        compiler_params=_cparams(1),
        name="s5_rows",
    )(y)
    return out.reshape(n_seq * seq_len, d)


def _s5_head_kernel(x_ref, y_ref, g_ref, sc_ref, sh_ref, d_ref, w_ref, g1_ref, o_ref):
    x = x_ref[...]
    h = _modnorm(x, g_ref[...], sc_ref[0], sh_ref[0])
    a = jax.nn.gelu(y_ref[...] + d_ref[...] * h, approximate=True)
    z = jnp.dot(a.astype(BF16), w_ref[...], preferred_element_type=F32)
    dm = z.shape[1] // 2
    o_ref[...] = x + g1_ref[0] * (z[:, :dm] * jax.nn.sigmoid(z[:, dm:]))


def _s5_head(x, y, mods, seq_len, norm_g, d_skip, w_glu):
    t, d = x.shape
    tm = _row_tile(seq_len, 512)
    return pl.pallas_call(
        _s5_head_kernel,
        grid=(t // tm,),
        in_specs=[_row_spec(tm, d), _row_spec(tm, d), _full_spec((1, d)),
                  _mod_spec(mods.shape[0], tm, seq_len, d, 1),
                  _mod_spec(mods.shape[0], tm, seq_len, d, 0),
                  _full_spec((1, d)), _full_spec(w_glu.shape),
                  _mod_spec(mods.shape[0], tm, seq_len, d, 2)],
        out_specs=_row_spec(tm, d),
        out_shape=jax.ShapeDtypeStruct((t, d), F32),
        compiler_params=_cparams(1),
        name="s5_head",
    )(x, y, norm_g.reshape(1, d), mods, mods, d_skip.reshape(1, d), w_glu, mods)


def _s5_mixer(x, ctx, mods_x, mods_c, n_seq, norm_g, a_re, a_im, log_dt, b_re, b_im, c_re, c_im,
              d_skip, w_glu):
    s_len, c_len = x.shape[0] // n_seq, ctx.shape[0] // n_seq
    g = a_re.shape[1]
    ops = [_s5_prep(a_re[k], a_im[k], log_dt[k], b_re[k], b_im[k], c_re[k], c_im[k], bool(k))
           for k in range(2)]
    vc = _s5_cols(ctx, mods_c, n_seq, c_len, norm_g, g)
    vx = _s5_cols(x, mods_x, n_seq, s_len, norm_g, g)
    y_c, y_x = _s5_core(vc, vx, ops[0], ops[1], n_seq)
    x = _s5_head(x, _s5_rows(y_x, n_seq, s_len), mods_x, s_len, norm_g, d_skip, w_glu)
    ctx = _s5_head(ctx, _s5_rows(y_c, n_seq, c_len), mods_c, c_len, norm_g, d_skip, w_glu)
    return x, ctx


def _pack_pairs(lo, hi):
    lo_bits = lax.bitcast_convert_type(lo.astype(BF16).astype(F32), U32)
    hi_bits = lax.bitcast_convert_type(hi.astype(BF16).astype(F32), U32)
    return lax.shift_right_logical(lo_bits, jnp.uint32(16)) | (hi_bits & jnp.uint32(0xFFFF0000))


def _unpack_pairs(words):
    lo = lax.bitcast_convert_type(lax.shift_left(words, jnp.uint32(16)), F32)
    hi = lax.bitcast_convert_type(words & jnp.uint32(0xFFFF0000), F32)
    return lo, hi


def _pack_row_chunks(rows, pp):
    lanes = V7X_LANES
    return [_pack_pairs(rows[:, c * lanes:(c + 1) * lanes], rows[:, (c + pp) * lanes:(c + pp + 1) * lanes])
            for c in range(pp)]


def _unpack_row_chunks(chunks):
    pairs = [_unpack_pairs(w) for w in chunks]
    return jnp.concatenate([lo for lo, _ in pairs] + [hi for _, hi in pairs], axis=-1)


def _router_kernel(x_ref, g_ref, sc_ref, sh_ref, wr_ref, br_ref, h_ref, r_ref, c_ref, *, n_groups,
                   epg):
    h = _modnorm(x_ref[...], g_ref[...], sc_ref[0], sh_ref[0])
    tm = x_ref.shape[0]
    pp = h_ref.shape[0] // tm
    for c, words in enumerate(_pack_row_chunks(h, pp)):
        h_ref[pl.ds(c, tm, stride=pp), :] = words
    logits = lax.dot_general(wr_ref[...], h, (((1,), (1,)), ((), ())), preferred_element_type=F32,
                             precision=HIGHEST) + br_ref[...]
    row = lax.broadcasted_iota(I32, logits.shape, 0)
    far = jnp.int32(1 << 20)

    def first_max(vals):
        m = jnp.max(vals, axis=0, keepdims=True)
        return m, jnp.min(jnp.where(vals == m, row, far), axis=0, keepdims=True)

    is_group = row < n_groups
    gl = jnp.where(is_group, logits, NEG_BIG)
    gmax, gidx = first_max(gl)
    gsum = jnp.sum(jnp.where(is_group, jnp.exp(gl - gmax), 0.0), axis=0, keepdims=True)
    g_w = 1.0 / gsum
    lo = n_groups + gidx * epg
    le = jnp.where((row >= lo) & (row < lo + epg), logits, NEG_BIG)
    m1, i1 = first_max(le)
    m2, i2 = first_max(jnp.where(row == i1, NEG_BIG, le))
    ratio = jnp.exp(m2 - m1)
    w1 = g_w / (1.0 + ratio)
    w2 = g_w * ratio / (1.0 + ratio)
    e1 = (i1 - n_groups).astype(F32)
    e2 = (i2 - n_groups).astype(F32)
    out_row = lax.broadcasted_iota(I32, r_ref.shape, 0)
    r_ref[...] = jnp.where(out_row == 0, e1, jnp.where(out_row == 1, e2, jnp.where(
        out_row == 2, w1, jnp.where(out_row == 3, w2, 0.0))))
    chosen = ((row == i1) | (row == i2)).astype(F32)
    c_ref[0] = jnp.broadcast_to(jnp.sum(chosen, axis=1, keepdims=True), c_ref.shape[1:])


def _router(x, mods, seq_len, norm_g, wr, br, n_groups, epg):
    t, d = x.shape
    tm = _row_tile(seq_len, 1024)
    pp = d // (2 * V7X_LANES)
    n_logit = wr.shape[0]
    return pl.pallas_call(
        functools.partial(_router_kernel, n_groups=n_groups, epg=epg),
        grid=(t // tm,),
        in_specs=[_row_spec(tm, d), _full_spec((1, d)),
                  _mod_spec(mods.shape[0], tm, seq_len, d, 4),
                  _mod_spec(mods.shape[0], tm, seq_len, d, 3),
                  _full_spec(wr.shape), _full_spec(br.shape)],
        out_specs=[pl.BlockSpec((tm * pp, V7X_LANES), lambda i: (i, 0)),
                   pl.BlockSpec((V7X_SUBLANES, tm), lambda i: (0, i)),
                   pl.BlockSpec((1, n_logit, V7X_LANES), lambda i: (i, 0, 0))],
        out_shape=[jax.ShapeDtypeStruct((t * pp, V7X_LANES), U32),
                   jax.ShapeDtypeStruct((V7X_SUBLANES, t), F32),
                   jax.ShapeDtypeStruct((t // tm, n_logit, V7X_LANES), F32)],
        compiler_params=_cparams(1),
        name="moe_router",
    )(x, norm_g.reshape(1, d), mods, mods, wr, br)


def _dispatch_lists(route, counts, n_block, cap, pp):
    t = route.shape[1]
    n_sb = t // n_block
    n_assign = n_block * TOP_K
    ids = route[0:TOP_K].T.astype(I32).reshape(n_sb, n_assign)
    asg = jnp.broadcast_to(jnp.arange(n_assign, dtype=I32)[None, :], (n_sb, n_assign))
    _, asg_s = lax.sort((ids, asg), dimension=1, is_stable=True, num_keys=1)
    offs = jnp.cumsum(counts, axis=1) - counts
    asg_s = jnp.pad(asg_s, ((0, 0), (0, cap - n_assign)), constant_values=n_assign)
    src = jnp.minimum(asg_s // TOP_K, n_block - 1) * pp
    return (counts.reshape(-1), offs.reshape(-1),
            src.reshape(n_sb, 1, cap), (asg_s * pp).reshape(n_sb, 1, cap))


def _moe_kernel(cnt_ref, off_ref, src_ref, dst_ref, h_ref, w13_ref, w2_ref, o_ref,
                lhs_a, lhs_b, ys_a, ys_b, *, n_exp, tile):
    sb = pl.program_id(0)
    eb = pl.program_id(1)
    n_local = w13_ref.shape[0]
    pp = w13_ref.shape[1] // (2 * V7X_LANES)
    de = w2_ref.shape[1]
    scratch = ((lhs_a, ys_a), (lhs_b, ys_b))

    def slab(row):
        return pl.ds(pl.multiple_of(row, pp), pp)

    @pl.when(eb == 0)
    def _():
        n_live = h_ref.shape[0] * TOP_K
        spare = o_ref.shape[1] - n_live
        o_ref[0, pl.ds(n_live, spare), :] = jnp.zeros((spare, V7X_LANES), U32)

    def run_tiles(q, base, sizes):
        starts = [base + sum(sizes[:k]) for k in range(len(sizes))]
        for (lhs_scr, _), start, size in zip(scratch, starts, sizes):
            stride = size + 1
            for mi in range(size):
                lhs_scr[pl.ds(mi, pp, stride=stride), :] = h_ref[slab(src_ref[0, 0, start + mi]), :]
        for (lhs_scr, ys_scr), size in zip(scratch, sizes):
            stride = size + 1
            lhs = _unpack_row_chunks([lhs_scr[pl.ds(c * stride, size), :] for c in range(pp)])
            hid = jnp.dot(lhs.astype(BF16), w13_ref[q], preferred_element_type=F32)
            act = _silu(hid[:, :de]) * hid[:, de:]
            ys = jnp.dot(act.astype(BF16), w2_ref[q], preferred_element_type=F32)
            for c, words in enumerate(_pack_row_chunks(ys, pp)):
                ys_scr[pl.ds(c * stride, size), :] = words
        for (_, ys_scr), start, size in zip(scratch, starts, sizes):
            stride = size + 1
            for mi in range(size):
                o_ref[0, slab(dst_ref[0, 0, start + mi]), :] = ys_scr[pl.ds(mi, pp, stride=stride), :]

    step = tile // MOE_REMAINDER_STEPS

    def remainder_runs(rows):
        sizes, size = [], tile
        while rows:
            while rows >= size:
                sizes.append(size)
                rows -= size
            size //= 2
        return [tuple(sizes[k:k + 2]) for k in range(0, len(sizes), 2)]

    def one_expert(q, carry):
        e = eb * n_local + q
        count = cnt_ref[sb * n_exp + e]
        seg = off_ref[sb * n_exp + e]
        n_pairs = lax.div(count, 2 * tile)

        def pair(j, c):
            run_tiles(q, seg + j * 2 * tile, (tile, tile))
            return c
        lax.fori_loop(0, n_pairs, pair, 0)

        rem = count - n_pairs * 2 * tile
        rem_base = seg + n_pairs * 2 * tile
        for hi in range(step, 2 * tile + 1, step):
            @pl.when((rem > hi - step) & (rem <= hi))
            def _():
                done = 0
                for sizes in remainder_runs(hi):
                    run_tiles(q, rem_base + done, sizes)
                    done += sum(sizes)
        return carry

    lax.fori_loop(0, n_local, one_expert, 0)


def _moe_tile(n_block, n_exp):
    return max(4 * V7X_SUBLANES, n_block * TOP_K // n_exp)


def _moe_experts(h, route, counts, w13, w2, n_block):
    n_exp, d, de2 = w13.shape
    de = de2 // 2
    pp = d // (2 * V7X_LANES)
    n_sb = h.shape[0] // (n_block * pp)
    n_assign = n_block * TOP_K
    tile = _moe_tile(n_block, n_exp)
    cap = -(-(n_assign + 2 * tile) // V7X_LANES) * V7X_LANES
    cnt, off, src_list, dst_list = _dispatch_lists(route, counts, n_block, cap, pp)
    out_slabs = n_assign + V7X_SUBLANES
    eb = MOE_EXPERTS_PER_STEP
    list_spec = pl.BlockSpec((1, 1, cap), lambda s, e, *_: (s, 0, 0), memory_space=pltpu.SMEM)
    grid_spec = pltpu.PrefetchScalarGridSpec(
        num_scalar_prefetch=2,
        grid=(n_sb, n_exp // eb),
        in_specs=[list_spec, list_spec,
                  pl.BlockSpec((n_block * pp, V7X_LANES), lambda s, e, *_: (s, 0),
                               pipeline_mode=pl.Buffered(1)),
                  pl.BlockSpec((eb, d, de2), lambda s, e, *_: (e, 0, 0)),
                  pl.BlockSpec((eb, de, d), lambda s, e, *_: (e, 0, 0))],
        out_specs=pl.BlockSpec((1, out_slabs * pp, V7X_LANES), lambda s, e, *_: (s, 0, 0)),
        scratch_shapes=[pltpu.VMEM((pp * (tile + 1), V7X_LANES), U32)] * 4)
    return pl.pallas_call(
        functools.partial(_moe_kernel, n_exp=n_exp, tile=tile),
        grid_spec=grid_spec,
        out_shape=jax.ShapeDtypeStruct((n_sb, out_slabs * pp, V7X_LANES), U32),
        compiler_params=_cparams(2),
        name="moe_experts",
    )(cnt, off, src_list, dst_list, h, w13, w2)


def _residual_kernel(x_ref, y_ref, w_ref, g2_ref, fg_ref, o_ref, *, final_norm):
    tm = x_ref.shape[0]
    wts = w_ref[...]
    pp = y_ref.shape[1] // (tm * TOP_K)
    y = sum(wts[:, k:k + 1]
            * _unpack_row_chunks([y_ref[0, pl.ds(k * pp + c, tm, stride=TOP_K * pp), :]
                                  for c in range(pp)])
            for k in range(TOP_K))
    x = x_ref[...] + g2_ref[0] * y
    if final_norm:
        x = x * lax.rsqrt(jnp.mean(x * x, axis=-1, keepdims=True) + RMS_EPS) * fg_ref[...]
    o_ref[...] = x


def _residual(x, y, wts, mods, seq_len, n_block, final_g, final_norm):
    t, d = x.shape
    tm = _row_tile(min(seq_len, n_block), 1024)
    per_sb = n_block // tm
    pp = d // (2 * V7X_LANES)
    return pl.pallas_call(
        functools.partial(_residual_kernel, final_norm=final_norm),
        grid=(t // tm,),
        in_specs=[_row_spec(tm, d),
                  pl.BlockSpec((1, tm * TOP_K * pp, V7X_LANES), lambda i: (i // per_sb, i % per_sb, 0)),
                  _row_spec(tm, TOP_K),
                  _mod_spec(mods.shape[0], tm, seq_len, d, 5), _full_spec((1, d))],
        out_specs=_row_spec(tm, d),
        out_shape=jax.ShapeDtypeStruct((t, d), F32),
        compiler_params=_cparams(1),
        name="moe_residual",
    )(x, y, wts, mods, final_g.reshape(1, d))


def _moe_block(t):
    n = min(t, 4096)
    while t % n:
        n //= 2
    return n


def _moe(x, mods, seq_len, norm_g, wr, br, w13, w2, n_groups, final_g, final_norm):
    n_exp = w13.shape[0]
    n_block = _moe_block(x.shape[0])
    h, route, tile_counts = _router(x, mods, seq_len, norm_g, wr, br, n_groups, n_exp // n_groups)
    counts = tile_counts[:, n_groups:n_groups + n_exp, 0].reshape(x.shape[0] // n_block, -1, n_exp)
    counts = jnp.sum(counts, axis=1).astype(I32)
    y = _moe_experts(h, route, counts, w13, w2, n_block)
    return _residual(x, y, route[TOP_K:2 * TOP_K].T, mods, seq_len, n_block, final_g, final_norm)


def kernel(x, c, ctx, c_ctx, ada_w, ada_b, norm1_g, norm2_g, conf_w_in, conf_dw, conf_dw_b, conf_ln_g, conf_ln_b, conf_w_out, sc_w_in, sc_conv, sc_w_out, s5_a_re, s5_a_im, s5_log_dt, s5_b_re, s5_b_im, s5_c_re, s5_c_im, s5_d, s5_w_glu, moe_wg, moe_bg, moe_we, moe_be, moe_w13, moe_w2, final_g):
    b, s, d = x.shape
    lc = ctx.shape[1]
    depth = ada_w.shape[0]
    n_groups = moe_wg.shape[-1]
    n_exp = moe_we.shape[-1]
    assert s % GRID_W == 0

    rows = (b + 1 + V7X_SUBLANES - 1) // V7X_SUBLANES * V7X_SUBLANES
    cin = jnp.zeros((rows, d), F32).at[:b].set(c).at[b].set(c_ctx)
    table = _ada_table(cin, ada_w, ada_b)

    xs = x.reshape(b * s, d)
    cs = ctx.reshape(b * lc, d)
    for i in range(depth):
        kind, j = i % N_MIXERS, i // N_MIXERS
        update_ctx = i < depth - 1
        mods_x = table[i, :b].reshape(b, 1, 6 * d)
        mods_c = table[i, b].reshape(1, 1, 6 * d)
        if kind == 0:
            args = (norm1_g[i], conf_w_in[j].astype(BF16), conf_dw[j], conf_dw_b[j],
                    conf_ln_g[j], conf_ln_b[j], conf_w_out[j].astype(BF16))
            xs = _conformer(xs, mods_x, s, GRID_W, *args)
            if update_ctx:
                cs = _conformer(cs, mods_c, lc, 1, *args)
        elif kind == 1:
            args = (norm1_g[i], sc_w_in[j].astype(BF16), sc_conv[j], sc_w_out[j].astype(BF16))
            xs = _short_conv(xs, mods_x, s, GRID_W, *args)
            if update_ctx:
                cs = _short_conv(cs, mods_c, lc, lc, *args)
        else:
            xs, cs_new = _s5_mixer(xs, cs, mods_x, mods_c, b, norm1_g[i], s5_a_re[j], s5_a_im[j],
                                   s5_log_dt[j], s5_b_re[j], s5_b_im[j], s5_c_re[j], s5_c_im[j],
                                   s5_d[j], s5_w_glu[j].astype(BF16))
            if update_ctx:
                cs = cs_new

        n_logit = -(-(n_groups + n_exp) // V7X_SUBLANES) * V7X_SUBLANES
        wr = jnp.zeros((n_logit, d), F32).at[:n_groups].set(moe_wg[i].T)
        wr = wr.at[n_groups:n_groups + n_exp].set(moe_we[i].T)
        br = jnp.zeros((n_logit, 1), F32).at[:n_groups, 0].set(moe_bg[i])
        br = br.at[n_groups:n_groups + n_exp, 0].set(moe_be[i])
        moe_args = (norm2_g[i], wr, br, moe_w13[i].astype(BF16), moe_w2[i].astype(BF16), n_groups,
                    final_g)
        xs = _moe(xs, mods_x, s, *moe_args, final_norm=(i == depth - 1))
        if update_ctx:
            cs = _moe(cs, mods_c, lc, *moe_args, final_norm=False)
    return xs.reshape(b, s, d)
```

```python
import functools

import jax
import jax.numpy as jnp
from jax import lax
from jax.experimental import pallas as pl
from jax.experimental.pallas import tpu as pltpu

F32 = jnp.float32
BF16 = jnp.bfloat16
I32 = jnp.int32
U32 = jnp.uint32
HIGHEST = lax.Precision.HIGHEST

GRID_W = 64
N_MIXERS = 3
TOP_K = 2
RMS_EPS = 1e-6
LN_EPS = 1e-5
S5_DT_FLOOR = -1e-4

V7X_VMEM_BYTES = 64 * 1024 * 1024
V7X_LANES = 128
V7X_SUBLANES = 8
VMEM_LIMIT_BYTES = V7X_VMEM_BYTES - 6 * 1024 * 1024

ROW_TILE = 1024
DWCONV_CHUNK = 64
MOE_BLOCK_TOKENS = 4096
S5_CHUNK = 16
MOE_EXPERTS_PER_STEP = 2
NEG_BIG = -1e30


def _cparams(n_axes):
    return pltpu.CompilerParams(dimension_semantics=("arbitrary",) * n_axes,
                                vmem_limit_bytes=VMEM_LIMIT_BYTES)


def _row_tile(seq_len, want):
    t = min(seq_len, want)
    while seq_len % t or t % V7X_SUBLANES:
        t -= 1
    return t


def _modnorm(x, g, sc, sh):
    y = x * lax.rsqrt(jnp.mean(x * x, axis=-1, keepdims=True) + RMS_EPS)
    return (y * g) * (1.0 + sc) + sh


def _silu(v):
    return v * jax.nn.sigmoid(v)


def _mod_spec(n_mod, tm, seq_len, d, j):
    if n_mod == 1:
        return pl.BlockSpec((1, 1, d), lambda t: (0, 0, j))
    return pl.BlockSpec((1, 1, d), lambda t: ((t * tm) // seq_len, 0, j))


def _row_spec(tm, d):
    return pl.BlockSpec((tm, d), lambda t: (t, 0))


def _full_spec(shape):
    nd = len(shape)
    return pl.BlockSpec(shape, lambda *_: (0,) * nd)


def _ada_kernel(c_ref, w_ref, b_ref, o_ref):
    o_ref[0] = jnp.dot(_silu(c_ref[...]), w_ref[0], preferred_element_type=F32,
                       precision=HIGHEST) + b_ref[0]


def _ada_table(cin, ada_w, ada_b):
    depth, d, d6 = ada_w.shape
    r = cin.shape[0]
    tn = d6 // 6
    return pl.pallas_call(
        _ada_kernel,
        grid=(depth, d6 // tn),
        in_specs=[pl.BlockSpec((r, d), lambda i, j: (0, 0)),
                  pl.BlockSpec((1, d, tn), lambda i, j: (i, 0, j)),
                  pl.BlockSpec((1, 1, tn), lambda i, j: (i, 0, j))],
        out_specs=pl.BlockSpec((1, r, tn), lambda i, j: (i, 0, j)),
        out_shape=jax.ShapeDtypeStruct((depth, r, d6), F32),
        compiler_params=_cparams(2),
        name="ada_table",
    )(cin, ada_w, ada_b.reshape(depth, 1, d6))


def _conf_in_kernel(x_ref, g_ref, sc_ref, sh_ref, w_ref, z_ref):
    h = _modnorm(x_ref[...], g_ref[...], sc_ref[0], sh_ref[0])
    y = jnp.dot(h.astype(BF16), w_ref[...], preferred_element_type=F32)
    ci = y.shape[1] // 2
    z_ref[...] = (y[:, :ci] * jax.nn.sigmoid(y[:, ci:])).astype(z_ref.dtype)


def _conf_in(x, mods, seq_len, norm_g, w_in):
    t, d = x.shape
    tm = _row_tile(seq_len, ROW_TILE)
    ci = w_in.shape[1] // 2
    return pl.pallas_call(
        _conf_in_kernel,
        grid=(t // tm,),
        in_specs=[_row_spec(tm, d), _full_spec((1, d)),
                  _mod_spec(mods.shape[0], tm, seq_len, d, 1),
                  _mod_spec(mods.shape[0], tm, seq_len, d, 0),
                  _full_spec(w_in.shape)],
        out_specs=_row_spec(tm, ci),
        out_shape=jax.ShapeDtypeStruct((t, ci), BF16),
        compiler_params=_cparams(1),
        name="conf_in",
    )(x, norm_g.reshape(1, d), mods, mods, w_in)


def _dwconv_kernel(z_ref, w_ref, b_ref, o_ref, src_scr, *, stride, chunk):
    seq_len, cb = z_ref.shape
    taps = w_ref.shape[0]
    half = taps // 2
    aligned = stride % chunk == 0
    pad = 0 if aligned else half * stride
    if pad:
        src_scr[pl.ds(0, pad), :] = jnp.zeros((pad, cb), F32)
        src_scr[pl.ds(pad + seq_len, pad), :] = jnp.zeros((pad, cb), F32)
    src_scr[pl.ds(pad, seq_len), :] = z_ref[...].astype(F32)
    w = w_ref[...]
    bias = b_ref[...]
    for r0 in range(0, seq_len, chunk):
        acc = jnp.broadcast_to(bias, (chunk, cb))
        for k in range(taps):
            lo = r0 + (k - half) * stride
            if aligned and (lo < 0 or lo + chunk > seq_len):
                continue
            acc = acc + w[k:k + 1, :] * src_scr[pl.ds(lo + pad, chunk), :]
        o_ref[pl.ds(r0, chunk), :] = acc.astype(o_ref.dtype)


def _dwconv(z, seq_len, stride, w, b):
    t, c = z.shape
    taps = w.shape[0]
    cb = min(c, 2 * V7X_LANES)
    chunk = _row_tile(seq_len, DWCONV_CHUNK)
    pad = 0 if stride % chunk == 0 else (taps // 2) * stride
    return pl.pallas_call(
        functools.partial(_dwconv_kernel, stride=stride, chunk=chunk),
        grid=(t // seq_len, c // cb),
        in_specs=[pl.BlockSpec((seq_len, cb), lambda s, j: (s, j)),
                  pl.BlockSpec((taps, cb), lambda s, j: (0, j)),
                  pl.BlockSpec((1, cb), lambda s, j: (0, j))],
        out_specs=pl.BlockSpec((seq_len, cb), lambda s, j: (s, j)),
        out_shape=jax.ShapeDtypeStruct((t, c), BF16),
        scratch_shapes=[pltpu.VMEM((seq_len + 2 * pad, cb), F32)],
        compiler_params=_cparams(2),
        name="dwconv",
    )(z, w, b.reshape(1, c))


def _conf_out_kernel(z_ref, lg_ref, lb_ref, w_ref, x_ref, g1_ref, o_ref):
    z = z_ref[...].astype(F32)
    mu = jnp.mean(z, axis=-1, keepdims=True)
    zc = z - mu
    var = jnp.mean(zc * zc, axis=-1, keepdims=True)
    y = zc * lax.rsqrt(var + LN_EPS) * lg_ref[...] + lb_ref[...]
    m = jnp.dot(_silu(y).astype(BF16), w_ref[...], preferred_element_type=F32)
    o_ref[...] = x_ref[...] + g1_ref[0] * m


def _conf_out(z, x, mods, seq_len, ln_g, ln_b, w_out):
    t, d = x.shape
    ci = z.shape[1]
    tm = _row_tile(seq_len, ROW_TILE)
    return pl.pallas_call(
        _conf_out_kernel,
        grid=(t // tm,),
        in_specs=[_row_spec(tm, ci), _full_spec((1, ci)), _full_spec((1, ci)),
                  _full_spec(w_out.shape), _row_spec(tm, d),
                  _mod_spec(mods.shape[0], tm, seq_len, d, 2)],
        out_specs=_row_spec(tm, d),
        out_shape=jax.ShapeDtypeStruct((t, d), F32),
        compiler_params=_cparams(1),
        name="conf_out",
    )(z, ln_g.reshape(1, ci), ln_b.reshape(1, ci), w_out, x, mods)


def _conformer(x, mods, seq_len, stride, norm_g, w_in, dw, dw_b, ln_g, ln_b, w_out):
    z = _conf_in(x, mods, seq_len, norm_g, w_in)
    z = _dwconv(z, seq_len, stride, dw, dw_b)
    return _conf_out(z, x, mods, seq_len, ln_g, ln_b, w_out)


def _sc_kernel(x_ref, g_ref, sc_ref, sh_ref, win_ref, cw_ref, wout_ref, g1_ref, o_ref, *, period):
    x = x_ref[...]
    tm, d = x.shape
    h = _modnorm(x, g_ref[...], sc_ref[0], sh_ref[0])
    y = jnp.dot(h.astype(BF16), win_ref[...], preferred_element_type=F32)
    gb, gc, v = y[:, :d], y[:, d:2 * d], y[:, 2 * d:]
    u = gc * v
    pos = lax.broadcasted_iota(I32, (tm, 1), 0) % period
    u_prev = jnp.where(pos == 0, 0.0, pltpu.roll(u, 1, 0))
    u_next = jnp.where(pos == period - 1, 0.0, pltpu.roll(u, tm - 1, 0))
    cw = cw_ref[...]
    conv = cw[0:1, :] * u_prev + cw[1:2, :] * u + cw[2:3, :] * u_next
    m = jnp.dot((gb * conv).astype(BF16), wout_ref[...], preferred_element_type=F32)
    o_ref[...] = x + g1_ref[0] * m


def _short_conv(x, mods, seq_len, period, norm_g, w_in, conv_w, w_out):
    t, d = x.shape
    tm = _row_tile(seq_len, ROW_TILE)
    assert tm % period == 0 and conv_w.shape[0] == 3
    return pl.pallas_call(
        functools.partial(_sc_kernel, period=period),
        grid=(t // tm,),
        in_specs=[_row_spec(tm, d), _full_spec((1, d)),
                  _mod_spec(mods.shape[0], tm, seq_len, d, 1),
                  _mod_spec(mods.shape[0], tm, seq_len, d, 0),
                  _full_spec(w_in.shape), _full_spec(conv_w.shape), _full_spec(w_out.shape),
                  _mod_spec(mods.shape[0], tm, seq_len, d, 2)],
        out_specs=_row_spec(tm, d),
        out_shape=jax.ShapeDtypeStruct((t, d), F32),
        compiler_params=_cparams(1),
        name="short_conv",
    )(x, norm_g.reshape(1, d), mods, mods, w_in, conv_w, w_out, mods)


def _s5_prep_kernel(are_ref, aim_ref, ldt_ref, bre_ref, bim_ref, cre_ref, cim_ref,
                    a16r_ref, a16i_ref, winr_ref, wini_ref, woutr_ref, wouti_ref, k_ref, *, reverse):
    a_re = jnp.minimum(are_ref[0], S5_DT_FLOOR)
    a_im = aim_ref[0]
    dt = jnp.exp(ldt_ref[0])
    b_re, b_im = bre_ref[0], bim_ref[0]
    c_re, c_im = cre_ref[0], cim_ref[0]
    cg = b_re.shape[1]
    tc = S5_CHUNK

    def power(n):
        mag = jnp.exp((n * dt) * a_re)
        ang = (n * dt) * a_im
        return mag * jnp.cos(ang), mag * jnp.sin(ang)

    abar_re, abar_im = power(1)
    den = a_re * a_re + a_im * a_im
    n_re = abar_re - 1.0
    n_im = abar_im
    k_re = (n_re * a_re + n_im * a_im) / den
    k_im = (n_im * a_re - n_re * a_im) / den
    bb_re = k_re * b_re - k_im * b_im
    bb_im = k_re * b_im + k_im * b_re

    e16r, e16i = power(tc)
    a16r_ref[0] = e16r
    a16i_ref[0] = e16i
    for t in range(tc):
        er, ei = power(t if reverse else tc - 1 - t)
        winr_ref[0, :, t * cg:(t + 1) * cg, :] = er * bb_re - ei * bb_im
        wini_ref[0, :, t * cg:(t + 1) * cg, :] = er * bb_im + ei * bb_re
        er, ei = power(tc - t if reverse else t + 1)
        woutr_ref[0, :, t * cg:(t + 1) * cg, :] = c_re * er - c_im * ei
        wouti_ref[0, :, t * cg:(t + 1) * cg, :] = -(c_re * ei + c_im * er)
        er, ei = power(t)
        m_re = c_re * er - c_im * ei
        m_im = c_re * ei + c_im * er
        k_ref[0, :, t * cg:(t + 1) * cg, :] = (
            jnp.einsum('gap,gbp->gab', m_re, bb_re, preferred_element_type=F32, precision=HIGHEST)
            - jnp.einsum('gap,gbp->gab', m_im, bb_im, preferred_element_type=F32, precision=HIGHEST))


def _s5_prep(a_re, a_im, log_dt, b_re, b_im, c_re, c_im, reverse):
    g, p = a_re.shape
    cg = b_re.shape[2]
    gb = min(g, V7X_SUBLANES)
    rows = S5_CHUNK * cg
    a4 = lambda a: a.reshape(1, g, 1, p)
    bt = lambda b: jnp.swapaxes(b, 1, 2).reshape(1, g, cg, p)
    spec_a = pl.BlockSpec((1, gb, 1, p), lambda i: (0, i, 0, 0))
    spec_b = pl.BlockSpec((1, gb, cg, p), lambda i: (0, i, 0, 0))
    spec_w = pl.BlockSpec((1, gb, rows, p), lambda i: (0, i, 0, 0))
    outs = pl.pallas_call(
        functools.partial(_s5_prep_kernel, reverse=reverse),
        grid=(g // gb,),
        in_specs=[spec_a, spec_a, pl.BlockSpec((1, gb, 1, 1), lambda i: (0, i, 0, 0)),
                  spec_b, spec_b, spec_b, spec_b],
        out_specs=[spec_a, spec_a, spec_w, spec_w, spec_w, spec_w,
                   pl.BlockSpec((1, gb, rows, cg), lambda i: (0, i, 0, 0))],
        out_shape=[jax.ShapeDtypeStruct((1, g, 1, p), F32)] * 2
        + [jax.ShapeDtypeStruct((1, g, rows, p), F32)] * 4
        + [jax.ShapeDtypeStruct((1, g, rows, cg), F32)],
        compiler_params=_cparams(1),
        name="s5_prep",
    )(a4(a_re), a4(a_im), log_dt.reshape(1, g, 1, 1), bt(b_re), bt(b_im),
      c_re.reshape(1, g, cg, p), c_im.reshape(1, g, cg, p))
    a16r, a16i, winr, wini, woutr, wouti, kk = [o[0] for o in outs]
    return a16r, a16i, winr, wini, woutr, wouti, kk


def _pair_blockdiag(w):
    g, r, p = w.shape
    w = w.reshape(g // 2, 2, r, p)
    z = jnp.zeros_like(w[:, 0])
    top = jnp.concatenate([w[:, 0], z], axis=-1)
    bot = jnp.concatenate([z, w[:, 1]], axis=-1)
    return jnp.concatenate([top, bot], axis=1)


def _toeplitz(kk, cg, reverse):
    g = kk.shape[0]
    tc = S5_CHUNK
    k5 = kk.reshape(g, tc, cg, cg)
    t_in = jnp.arange(tc)[None, :]
    t_out = jnp.arange(tc)[:, None]
    lag = (t_in - t_out) if reverse else (t_out - t_in)
    blocks = jnp.where((lag >= 0)[None, :, :, None, None], k5[:, jnp.clip(lag, 0, tc - 1)], 0.0)
    return blocks.transpose(0, 1, 3, 2, 4).reshape(g, tc * cg, tc * cg)


def _s5_core_kernel(vc_ref, vx_ref, a16r_ref, a16i_ref, winr_ref, wini_ref, woutr_ref, wouti_ref,
                    tz_ref, yc_ref, yx_ref, sre, sim, hre, him, *, n_seq):
    r = vc_ref.shape[1]
    n_cc = vc_ref.shape[2]
    n_col = n_cc + vx_ref.shape[2]
    n_chunks = n_col // n_seq
    n_ctx_chunks = n_cc // n_seq
    v = jnp.concatenate([jnp.concatenate([vc_ref[half], vx_ref[half]], axis=-1)
                         for half in range(2)], axis=0)
    for direction in range(2):
        sre[...] = lax.dot_general(v, winr_ref[direction, 0], (((0,), (0,)), ((), ())),
                                   preferred_element_type=F32)
        sim[...] = lax.dot_general(v, wini_ref[direction, 0], (((0,), (0,)), ((), ())),
                                   preferred_element_type=F32)
        ar = a16r_ref[direction, 0]
        ai = a16i_ref[direction, 0]

        def step(k, carry):
            h_r, h_i = carry
            rows = pl.ds(pl.multiple_of(k * n_seq, n_seq), n_seq)
            hre[rows, :] = h_r
            him[rows, :] = h_i
            return (ar * h_r - ai * h_i + sre[rows, :], ar * h_i + ai * h_r + sim[rows, :])

        zero = jnp.zeros((n_seq, sre.shape[1]), F32)
        if direction == 0:
            lax.fori_loop(0, n_chunks, step, (zero, zero))
        else:
            mid = lax.fori_loop(0, n_ctx_chunks,
                                lambda i, c: step(n_ctx_chunks - 1 - i, c), (zero, zero))
            lax.fori_loop(0, n_chunks - n_ctx_chunks,
                          lambda i, c: step(n_chunks - 1 - i, c), mid)

        h_r = hre[...].astype(BF16)
        h_i = him[...].astype(BF16)
        for half in range(2):
            rows = pl.ds(half * r, r)
            part = (lax.dot_general(woutr_ref[direction, 0, rows, :], h_r,
                                    (((1,), (1,)), ((), ())), preferred_element_type=F32)
                    + lax.dot_general(wouti_ref[direction, 0, rows, :], h_i,
                                      (((1,), (1,)), ((), ())), preferred_element_type=F32)
                    + jnp.dot(tz_ref[direction, half], v[half * r:(half + 1) * r, :],
                              preferred_element_type=F32))
            if direction == 0:
                yc_ref[half] = part[:, :n_cc]
                yx_ref[half] = part[:, n_cc:]
            else:
                yc_ref[half] = yc_ref[half] + part[:, :n_cc]
                yx_ref[half] = yx_ref[half] + part[:, n_cc:]


def _s5_core(vc, vx, ops_fwd, ops_bwd, n_seq):
    g, r, n_cc = vc.shape
    n_cx = vx.shape[2]
    cg = r // S5_CHUNK
    stacked = []
    for idx in range(6):
        stacked.append(jnp.stack([ops_fwd[idx], ops_bwd[idx]]))
    a16r, a16i, winr, wini, woutr, wouti = stacked
    p = a16r.shape[-1]
    pair_vec = lambda a: a.reshape(2, g // 2, 1, 2 * p)
    pair_mat = lambda w: jnp.stack([_pair_blockdiag(w[0]), _pair_blockdiag(w[1])]).astype(BF16)
    tz = jnp.stack([_toeplitz(ops_fwd[6], cg, False), _toeplitz(ops_bwd[6], cg, True)]).astype(BF16)
    vec_spec = pl.BlockSpec((2, 1, 1, 2 * p), lambda i: (0, i, 0, 0))
    mat_spec = pl.BlockSpec((2, 1, 2 * r, 2 * p), lambda i: (0, i, 0, 0))
    col_spec = lambda n: pl.BlockSpec((2, r, n), lambda i: (i, 0, 0))
    return pl.pallas_call(
        functools.partial(_s5_core_kernel, n_seq=n_seq),
        grid=(g // 2,),
        in_specs=[col_spec(n_cc), col_spec(n_cx),
                  vec_spec, vec_spec, mat_spec, mat_spec, mat_spec, mat_spec,
                  pl.BlockSpec((2, 2, r, r), lambda i: (0, i, 0, 0))],
        out_specs=[col_spec(n_cc), col_spec(n_cx)],
        out_shape=[jax.ShapeDtypeStruct((g, r, n_cc), F32), jax.ShapeDtypeStruct((g, r, n_cx), F32)],
        scratch_shapes=[pltpu.VMEM((n_cc + n_cx, 2 * p), F32)] * 4,
        compiler_params=_cparams(1),
        name="s5_core",
    )(vc, vx, pair_vec(a16r), pair_vec(a16i), pair_mat(winr), pair_mat(wini),
      pair_mat(woutr), pair_mat(wouti), tz)


def _s5_cols_kernel(x_ref, g_ref, sc_ref, sh_ref, v_ref, row_scr):
    nb, r, d = x_ref.shape
    g = v_ref.shape[0]
    cg = d // g
    tc = v_ref.shape[1] // cg
    nk = r // tc
    x = x_ref[...]
    y = x * lax.rsqrt(jnp.mean(x * x, axis=-1, keepdims=True) + RMS_EPS)
    h = ((y * g_ref[...]) * (1.0 + sc_ref[...]) + sh_ref[...]).reshape(nb * r, d)
    n_lane_blocks = row_scr.shape[0]
    gl = V7X_LANES // cg
    for c in range(n_lane_blocks):
        row_scr[c] = h[:, c * V7X_LANES:(c + 1) * V7X_LANES]
    for t in range(tc):
        for c in range(n_lane_blocks):
            z = jnp.concatenate([row_scr[c, pl.ds(t + tc * k, nb, stride=r), :] for k in range(nk)],
                                axis=0)
            v_ref[c * gl:(c + 1) * gl, t * cg:(t + 1) * cg, :] = (
                z.T.reshape(gl, cg, nk * nb).astype(v_ref.dtype))


def _s5_block_rows(n_seq, seq_len):
    return min(seq_len, S5_CHUNK * max(1, V7X_LANES // n_seq))


def _s5_cols(x, mods, n_seq, seq_len, norm_g, n_groups):
    d = x.shape[1]
    r = _s5_block_rows(n_seq, seq_len)
    cols = (r // S5_CHUNK) * n_seq
    rows = S5_CHUNK * (d // n_groups)
    n_mod = mods.shape[0]
    mod_spec = lambda j: pl.BlockSpec((n_mod, 1, d), lambda i: (0, 0, j))
    return pl.pallas_call(
        _s5_cols_kernel,
        grid=(seq_len // r,),
        in_specs=[pl.BlockSpec((n_seq, r, d), lambda i: (0, i, 0)), _full_spec((1, d)),
                  mod_spec(1), mod_spec(0)],
        out_specs=pl.BlockSpec((n_groups, rows, cols), lambda i: (0, 0, i)),
        out_shape=jax.ShapeDtypeStruct((n_groups, rows, (seq_len // S5_CHUNK) * n_seq), BF16),
        scratch_shapes=[pltpu.VMEM((d // V7X_LANES, n_seq * r, V7X_LANES), F32)],
        compiler_params=_cparams(1),
        name="s5_cols",
    )(x.reshape(n_seq, seq_len, d), norm_g.reshape(1, d), mods, mods)


def _s5_rows_kernel(y_ref, o_ref, row_scr):
    nb, r, d = o_ref.shape
    g = y_ref.shape[0]
    cg = d // g
    tc = y_ref.shape[1] // cg
    nk = r // tc
    n_lane_blocks = row_scr.shape[0]
    gl = V7X_LANES // cg
    for t in range(tc):
        for c in range(n_lane_blocks):
            z = y_ref[c * gl:(c + 1) * gl, t * cg:(t + 1) * cg, :].reshape(V7X_LANES, nk * nb).T
            for k in range(nk):
                row_scr[c, pl.ds(t + tc * k, nb, stride=r), :] = z[k * nb:(k + 1) * nb, :]
    o_ref[...] = jnp.concatenate([row_scr[c] for c in range(n_lane_blocks)],
                                 axis=-1).reshape(nb, r, d)


def _s5_rows(y, n_seq, seq_len):
    n_groups, rows, _ = y.shape
    d = n_groups * (rows // S5_CHUNK)
    r = _s5_block_rows(n_seq, seq_len)
    cols = (r // S5_CHUNK) * n_seq
    out = pl.pallas_call(
        _s5_rows_kernel,
        grid=(seq_len // r,),
        in_specs=[pl.BlockSpec((n_groups, rows, cols), lambda i: (0, 0, i))],
        out_specs=pl.BlockSpec((n_seq, r, d), lambda i: (0, i, 0)),
        out_shape=jax.ShapeDtypeStruct((n_seq, seq_len, d), F32),
        scratch_shapes=[pltpu.VMEM((d // V7X_LANES, n_seq * r, V7X_LANES), F32)],
        compiler_params=_cparams(1),
        name="s5_rows",
    )(y)
    return out.reshape(n_seq * seq_len, d)


def _s5_head_kernel(x_ref, y_ref, g_ref, sc_ref, sh_ref, d_ref, w_ref, g1_ref, o_ref):
    x = x_ref[...]
    h = _modnorm(x, g_ref[...], sc_ref[0], sh_ref[0])
    a = jax.nn.gelu(y_ref[...] + d_ref[...] * h, approximate=True)
    z = jnp.dot(a.astype(BF16), w_ref[...], preferred_element_type=F32)
    dm = z.shape[1] // 2
    o_ref[...] = x + g1_ref[0] * (z[:, :dm] * jax.nn.sigmoid(z[:, dm:]))


def _s5_head(x, y, mods, seq_len, norm_g, d_skip, w_glu):
    t, d = x.shape
    tm = _row_tile(seq_len, ROW_TILE)
    return pl.pallas_call(
        _s5_head_kernel,
        grid=(t // tm,),
        in_specs=[_row_spec(tm, d), _row_spec(tm, d), _full_spec((1, d)),
                  _mod_spec(mods.shape[0], tm, seq_len, d, 1),
                  _mod_spec(mods.shape[0], tm, seq_len, d, 0),
                  _full_spec((1, d)), _full_spec(w_glu.shape),
                  _mod_spec(mods.shape[0], tm, seq_len, d, 2)],
        out_specs=_row_spec(tm, d),
        out_shape=jax.ShapeDtypeStruct((t, d), F32),
        compiler_params=_cparams(1),
        name="s5_head",
    )(x, y, norm_g.reshape(1, d), mods, mods, d_skip.reshape(1, d), w_glu, mods)


def _s5_mixer(x, ctx, mods_x, mods_c, n_seq, norm_g, a_re, a_im, log_dt, b_re, b_im, c_re, c_im,
              d_skip, w_glu):
    s_len, c_len = x.shape[0] // n_seq, ctx.shape[0] // n_seq
    g = a_re.shape[1]
    ops = [_s5_prep(a_re[k], a_im[k], log_dt[k], b_re[k], b_im[k], c_re[k], c_im[k], bool(k))
           for k in range(2)]
    vc = _s5_cols(ctx, mods_c, n_seq, c_len, norm_g, g)
    vx = _s5_cols(x, mods_x, n_seq, s_len, norm_g, g)
    y_c, y_x = _s5_core(vc, vx, ops[0], ops[1], n_seq)
    x = _s5_head(x, _s5_rows(y_x, n_seq, s_len), mods_x, s_len, norm_g, d_skip, w_glu)
    ctx = _s5_head(ctx, _s5_rows(y_c, n_seq, c_len), mods_c, c_len, norm_g, d_skip, w_glu)
    return x, ctx


def _pack_pairs(lo, hi):
    lo_bits = lax.bitcast_convert_type(lo.astype(BF16).astype(F32), U32)
    hi_bits = lax.bitcast_convert_type(hi.astype(BF16).astype(F32), U32)
    return lax.shift_right_logical(lo_bits, jnp.uint32(16)) | (hi_bits & jnp.uint32(0xFFFF0000))


def _unpack_pairs(words):
    lo = lax.bitcast_convert_type(lax.shift_left(words, jnp.uint32(16)), F32)
    hi = lax.bitcast_convert_type(words & jnp.uint32(0xFFFF0000), F32)
    return lo, hi


def _pack_row_chunks(rows, pp):
    lanes = V7X_LANES
    return [_pack_pairs(rows[:, c * lanes:(c + 1) * lanes], rows[:, (c + pp) * lanes:(c + pp + 1) * lanes])
            for c in range(pp)]


def _unpack_row_chunks(chunks):
    pairs = [_unpack_pairs(w) for w in chunks]
    return jnp.concatenate([lo for lo, _ in pairs] + [hi for _, hi in pairs], axis=-1)


def _router_kernel(x_ref, g_ref, sc_ref, sh_ref, wr_ref, br_ref, h_ref, r_ref, c_ref, *, n_groups,
                   epg):
    h = _modnorm(x_ref[...], g_ref[...], sc_ref[0], sh_ref[0])
    tm = x_ref.shape[0]
    pp = h_ref.shape[0] // tm
    for c, words in enumerate(_pack_row_chunks(h, pp)):
        h_ref[pl.ds(c, tm, stride=pp), :] = words
    logits = lax.dot_general(wr_ref[...], h, (((1,), (1,)), ((), ())), preferred_element_type=F32,
                             precision=HIGHEST) + br_ref[...]
    row = lax.broadcasted_iota(I32, logits.shape, 0)
    far = jnp.int32(1 << 20)

    def first_max(vals):
        m = jnp.max(vals, axis=0, keepdims=True)
        return m, jnp.min(jnp.where(vals == m, row, far), axis=0, keepdims=True)

    is_group = row < n_groups
    gl = jnp.where(is_group, logits, NEG_BIG)
    gmax, gidx = first_max(gl)
    gsum = jnp.sum(jnp.where(is_group, jnp.exp(gl - gmax), 0.0), axis=0, keepdims=True)
    g_w = 1.0 / gsum
    lo = n_groups + gidx * epg
    le = jnp.where((row >= lo) & (row < lo + epg), logits, NEG_BIG)
    m1, i1 = first_max(le)
    m2, i2 = first_max(jnp.where(row == i1, NEG_BIG, le))
    ratio = jnp.exp(m2 - m1)
    w1 = g_w / (1.0 + ratio)
    w2 = g_w * ratio / (1.0 + ratio)
    e1 = (i1 - n_groups).astype(F32)
    e2 = (i2 - n_groups).astype(F32)
    out_row = lax.broadcasted_iota(I32, r_ref.shape, 0)
    r_ref[...] = jnp.where(out_row == 0, e1, jnp.where(out_row == 1, e2, jnp.where(
        out_row == 2, w1, jnp.where(out_row == 3, w2, 0.0))))
    chosen = ((row == i1) | (row == i2)).astype(F32)
    c_ref[0] = jnp.broadcast_to(jnp.sum(chosen, axis=1, keepdims=True), c_ref.shape[1:])


def _router(x, mods, seq_len, norm_g, wr, br, n_groups, epg):
    t, d = x.shape
    tm = _row_tile(seq_len, ROW_TILE)
    pp = d // (2 * V7X_LANES)
    n_logit = wr.shape[0]
    return pl.pallas_call(
        functools.partial(_router_kernel, n_groups=n_groups, epg=epg),
        grid=(t // tm,),
        in_specs=[_row_spec(tm, d), _full_spec((1, d)),
                  _mod_spec(mods.shape[0], tm, seq_len, d, 4),
                  _mod_spec(mods.shape[0], tm, seq_len, d, 3),
                  _full_spec(wr.shape), _full_spec(br.shape)],
        out_specs=[pl.BlockSpec((tm * pp, V7X_LANES), lambda i: (i, 0)),
                   pl.BlockSpec((V7X_SUBLANES, tm), lambda i: (0, i)),
                   pl.BlockSpec((1, n_logit, V7X_LANES), lambda i: (i, 0, 0))],
        out_shape=[jax.ShapeDtypeStruct((t * pp, V7X_LANES), U32),
                   jax.ShapeDtypeStruct((V7X_SUBLANES, t), F32),
                   jax.ShapeDtypeStruct((t // tm, n_logit, V7X_LANES), F32)],
        compiler_params=_cparams(1),
        name="moe_router",
    )(x, norm_g.reshape(1, d), mods, mods, wr, br)


def _dispatch_lists(route, counts, n_block, cap, pp):
    t = route.shape[1]
    n_sb = t // n_block
    n_assign = n_block * TOP_K
    ids = route[0:TOP_K].T.astype(I32).reshape(n_sb, n_assign)
    asg = jnp.broadcast_to(jnp.arange(n_assign, dtype=I32)[None, :], (n_sb, n_assign))
    _, asg_s = lax.sort((ids, asg), dimension=1, is_stable=True, num_keys=1)
    offs = jnp.cumsum(counts, axis=1) - counts
    asg_s = jnp.pad(asg_s, ((0, 0), (0, cap - n_assign)), constant_values=n_assign)
    src = jnp.minimum(asg_s // TOP_K, n_block - 1) * pp
    return (counts.reshape(-1), offs.reshape(-1),
            src.reshape(n_sb, 1, cap), (asg_s * pp).reshape(n_sb, 1, cap))


def _moe_kernel(cnt_ref, off_ref, src_ref, dst_ref, h_ref, w13_ref, w2_ref, o_ref,
                lhs_a, lhs_b, ys_a, ys_b, *, n_exp, tile):
    sb = pl.program_id(0)
    eb = pl.program_id(1)
    n_local = w13_ref.shape[0]
    pp = w13_ref.shape[1] // (2 * V7X_LANES)
    de = w2_ref.shape[1]
    scratch = ((lhs_a, ys_a), (lhs_b, ys_b))

    def slab(row):
        return pl.ds(pl.multiple_of(row, pp), pp)

    @pl.when(eb == 0)
    def _():
        n_live = h_ref.shape[0] * TOP_K
        spare = o_ref.shape[1] - n_live
        o_ref[0, pl.ds(n_live, spare), :] = jnp.zeros((spare, V7X_LANES), U32)

    def run_tiles(q, base, sizes):
        starts = [base + sum(sizes[:k]) for k in range(len(sizes))]
        for (lhs_scr, _), start, size in zip(scratch, starts, sizes):
            stride = size + 1
            for mi in range(size):
                lhs_scr[pl.ds(mi, pp, stride=stride), :] = h_ref[slab(src_ref[0, 0, start + mi]), :]
        for (lhs_scr, ys_scr), size in zip(scratch, sizes):
            stride = size + 1
            lhs = _unpack_row_chunks([lhs_scr[pl.ds(c * stride, size), :] for c in range(pp)])
            hid = jnp.dot(lhs.astype(BF16), w13_ref[q], preferred_element_type=F32)
            act = _silu(hid[:, :de]) * hid[:, de:]
            ys = jnp.dot(act.astype(BF16), w2_ref[q], preferred_element_type=F32)
            for c, words in enumerate(_pack_row_chunks(ys, pp)):
                ys_scr[pl.ds(c * stride, size), :] = words
        for (_, ys_scr), start, size in zip(scratch, starts, sizes):
            stride = size + 1
            for mi in range(size):
                o_ref[0, slab(dst_ref[0, 0, start + mi]), :] = ys_scr[pl.ds(mi, pp, stride=stride), :]

    half = tile // 2

    def one_expert(q, carry):
        e = eb * n_local + q
        count = cnt_ref[sb * n_exp + e]
        seg = off_ref[sb * n_exp + e]
        n_pairs = lax.div(count, 2 * tile)

        def pair(j, c):
            run_tiles(q, seg + j * 2 * tile, (tile, tile))
            return c
        lax.fori_loop(0, n_pairs, pair, 0)

        rem = count - n_pairs * 2 * tile
        rem_base = seg + n_pairs * 2 * tile
        for hi, sizes in ((half, (half,)), (tile, (tile,)), (tile + half, (tile, half)),
                          (2 * tile, (tile, tile))):
            @pl.when((rem > hi - half) & (rem <= hi))
            def _():
                run_tiles(q, rem_base, sizes)
        return carry

    lax.fori_loop(0, n_local, one_expert, 0)


def _moe_tile(n_block, n_exp):
    return max(4 * V7X_SUBLANES, n_block * TOP_K // n_exp)


def _moe_experts(h, route, counts, w13, w2, n_block):
    n_exp, d, de2 = w13.shape
    de = de2 // 2
    pp = d // (2 * V7X_LANES)
    n_sb = h.shape[0] // (n_block * pp)
    n_assign = n_block * TOP_K
    tile = _moe_tile(n_block, n_exp)
    cap = -(-(n_assign + 2 * tile) // V7X_LANES) * V7X_LANES
    cnt, off, src_list, dst_list = _dispatch_lists(route, counts, n_block, cap, pp)
    out_slabs = n_assign + V7X_SUBLANES
    eb = MOE_EXPERTS_PER_STEP
    list_spec = pl.BlockSpec((1, 1, cap), lambda s, e, *_: (s, 0, 0), memory_space=pltpu.SMEM)
    grid_spec = pltpu.PrefetchScalarGridSpec(
        num_scalar_prefetch=2,
        grid=(n_sb, n_exp // eb),
        in_specs=[list_spec, list_spec,
                  pl.BlockSpec((n_block * pp, V7X_LANES), lambda s, e, *_: (s, 0),
                               pipeline_mode=pl.Buffered(1)),
                  pl.BlockSpec((eb, d, de2), lambda s, e, *_: (e, 0, 0)),
                  pl.BlockSpec((eb, de, d), lambda s, e, *_: (e, 0, 0))],
        out_specs=pl.BlockSpec((1, out_slabs * pp, V7X_LANES), lambda s, e, *_: (s, 0, 0)),
        scratch_shapes=[pltpu.VMEM((pp * (tile + 1), V7X_LANES), U32)] * 4)
    return pl.pallas_call(
        functools.partial(_moe_kernel, n_exp=n_exp, tile=tile),
        grid_spec=grid_spec,
        out_shape=jax.ShapeDtypeStruct((n_sb, out_slabs * pp, V7X_LANES), U32),
        compiler_params=_cparams(2),
        name="moe_experts",
    )(cnt, off, src_list, dst_list, h, w13, w2)


def _residual_kernel(x_ref, y_ref, w_ref, g2_ref, fg_ref, o_ref, *, final_norm):
    tm = x_ref.shape[0]
    wts = w_ref[...]
    pp = y_ref.shape[1] // (tm * TOP_K)
    y = sum(wts[:, k:k + 1]
            * _unpack_row_chunks([y_ref[0, pl.ds(k * pp + c, tm, stride=TOP_K * pp), :]
                                  for c in range(pp)])
            for k in range(TOP_K))
    x = x_ref[...] + g2_ref[0] * y
    if final_norm:
        x = x * lax.rsqrt(jnp.mean(x * x, axis=-1, keepdims=True) + RMS_EPS) * fg_ref[...]
    o_ref[...] = x


def _residual(x, y, wts, mods, seq_len, n_block, final_g, final_norm):
    t, d = x.shape
    tm = _row_tile(min(seq_len, n_block), ROW_TILE)
    per_sb = n_block // tm
    pp = d // (2 * V7X_LANES)
    return pl.pallas_call(
        functools.partial(_residual_kernel, final_norm=final_norm),
        grid=(t // tm,),
        in_specs=[_row_spec(tm, d),
                  pl.BlockSpec((1, tm * TOP_K * pp, V7X_LANES), lambda i: (i // per_sb, i % per_sb, 0)),
                  _row_spec(tm, TOP_K),
                  _mod_spec(mods.shape[0], tm, seq_len, d, 5), _full_spec((1, d))],
        out_specs=_row_spec(tm, d),
        out_shape=jax.ShapeDtypeStruct((t, d), F32),
        compiler_params=_cparams(1),
        name="moe_residual",
    )(x, y, wts, mods, final_g.reshape(1, d))


def _moe_block(t):
    n = min(t, MOE_BLOCK_TOKENS)
    while t % n:
        n //= 2
    return n


def _moe(x, mods, seq_len, norm_g, wr, br, w13, w2, n_groups, final_g, final_norm):
    n_exp = w13.shape[0]
    n_block = _moe_block(x.shape[0])
    h, route, tile_counts = _router(x, mods, seq_len, norm_g, wr, br, n_groups, n_exp // n_groups)
    counts = tile_counts[:, n_groups:n_groups + n_exp, 0].reshape(x.shape[0] // n_block, -1, n_exp)
    counts = jnp.sum(counts, axis=1).astype(I32)
    y = _moe_experts(h, route, counts, w13, w2, n_block)
    return _residual(x, y, route[TOP_K:2 * TOP_K].T, mods, seq_len, n_block, final_g, final_norm)


def kernel(x, c, ctx, c_ctx, ada_w, ada_b, norm1_g, norm2_g, conf_w_in, conf_dw, conf_dw_b, conf_ln_g, conf_ln_b, conf_w_out, sc_w_in, sc_conv, sc_w_out, s5_a_re, s5_a_im, s5_log_dt, s5_b_re, s5_b_im, s5_c_re, s5_c_im, s5_d, s5_w_glu, moe_wg, moe_bg, moe_we, moe_be, moe_w13, moe_w2, final_g):
    b, s, d = x.shape
    lc = ctx.shape[1]
    depth = ada_w.shape[0]
    n_groups = moe_wg.shape[-1]
    n_exp = moe_we.shape[-1]
    assert s % GRID_W == 0

    rows = (b + 1 + V7X_SUBLANES - 1) // V7X_SUBLANES * V7X_SUBLANES
    cin = jnp.zeros((rows, d), F32).at[:b].set(c).at[b].set(c_ctx)
    table = _ada_table(cin, ada_w, ada_b)

    xs = x.reshape(b * s, d)
    cs = ctx.reshape(b * lc, d)
    for i in range(depth):
        kind, j = i % N_MIXERS, i // N_MIXERS
        update_ctx = i < depth - 1
        mods_x = table[i, :b].reshape(b, 1, 6 * d)
        mods_c = table[i, b].reshape(1, 1, 6 * d)
        if kind == 0:
            args = (norm1_g[i], conf_w_in[j].astype(BF16), conf_dw[j], conf_dw_b[j],
                    conf_ln_g[j], conf_ln_b[j], conf_w_out[j].astype(BF16))
            xs = _conformer(xs, mods_x, s, GRID_W, *args)
            if update_ctx:
                cs = _conformer(cs, mods_c, lc, 1, *args)
        elif kind == 1:
            args = (norm1_g[i], sc_w_in[j].astype(BF16), sc_conv[j], sc_w_out[j].astype(BF16))
            xs = _short_conv(xs, mods_x, s, GRID_W, *args)
            if update_ctx:
                cs = _short_conv(cs, mods_c, lc, lc, *args)
        else:
            xs, cs_new = _s5_mixer(xs, cs, mods_x, mods_c, b, norm1_g[i], s5_a_re[j], s5_a_im[j],
                                   s5_log_dt[j], s5_b_re[j], s5_b_im[j], s5_c_re[j], s5_c_im[j],
                                   s5_d[j], s5_w_glu[j].astype(BF16))
            if update_ctx:
                cs = cs_new

        n_logit = -(-(n_groups + n_exp) // V7X_SUBLANES) * V7X_SUBLANES
        wr = jnp.zeros((n_logit, d), F32).at[:n_groups].set(moe_wg[i].T)
        wr = wr.at[n_groups:n_groups + n_exp].set(moe_we[i].T)
        br = jnp.zeros((n_logit, 1), F32).at[:n_groups, 0].set(moe_bg[i])
        br = br.at[n_groups:n_groups + n_exp, 0].set(moe_be[i])
        moe_args = (norm2_g[i], wr, br, moe_w13[i].astype(BF16), moe_w2[i].astype(BF16), n_groups,
                    final_g)
        xs = _moe(xs, mods_x, s, *moe_args, final_norm=(i == depth - 1))
        if update_ctx:
            cs = _moe(cs, mods_c, lc, *moe_args, final_norm=False)
    return xs.reshape(b, s, d)
```

```python
import functools

import jax
import jax.numpy as jnp
from jax import lax
from jax.experimental import pallas as pl
from jax.experimental.pallas import tpu as pltpu

F32 = jnp.float32
BF16 = jnp.bfloat16
I32 = jnp.int32
U32 = jnp.uint32
HIGHEST = lax.Precision.HIGHEST

GRID_W = 64
N_MIXERS = 3
TOP_K = 2
RMS_EPS = 1e-6
LN_EPS = 1e-5
S5_DT_FLOOR = -1e-4

V7X_VMEM_BYTES = 64 * 1024 * 1024
V7X_LANES = 128
V7X_SUBLANES = 8
VMEM_LIMIT_BYTES = V7X_VMEM_BYTES - 6 * 1024 * 1024

ROW_TILE = 1024
DWCONV_CHUNK = 64
MOE_BLOCK_TOKENS = 4096
S5_CHUNK = 16
MOE_EXPERTS_PER_STEP = 2
NEG_BIG = -1e30


def _cparams(n_axes):
    return pltpu.CompilerParams(dimension_semantics=("arbitrary",) * n_axes,
                                vmem_limit_bytes=VMEM_LIMIT_BYTES)


def _row_tile(seq_len, want):
    t = min(seq_len, want)
    while seq_len % t or t % V7X_SUBLANES:
        t -= 1
    return t


def _modnorm(x, g, sc, sh):
    y = x * lax.rsqrt(jnp.mean(x * x, axis=-1, keepdims=True) + RMS_EPS)
    return (y * g) * (1.0 + sc) + sh


def _silu(v):
    return v * jax.nn.sigmoid(v)


def _mod_spec(n_mod, tm, seq_len, d, j):
    if n_mod == 1:
        return pl.BlockSpec((1, 1, d), lambda t: (0, 0, j))
    return pl.BlockSpec((1, 1, d), lambda t: ((t * tm) // seq_len, 0, j))


def _row_spec(tm, d):
    return pl.BlockSpec((tm, d), lambda t: (t, 0))


def _full_spec(shape):
    nd = len(shape)
    return pl.BlockSpec(shape, lambda *_: (0,) * nd)


def _ada_kernel(c_ref, w_ref, b_ref, o_ref):
    o_ref[0] = jnp.dot(_silu(c_ref[...]), w_ref[0], preferred_element_type=F32,
                       precision=HIGHEST) + b_ref[0]


def _ada_table(cin, ada_w, ada_b):
    depth, d, d6 = ada_w.shape
    r = cin.shape[0]
    tn = d6 // 6
    return pl.pallas_call(
        _ada_kernel,
        grid=(depth, d6 // tn),
        in_specs=[pl.BlockSpec((r, d), lambda i, j: (0, 0)),
                  pl.BlockSpec((1, d, tn), lambda i, j: (i, 0, j)),
                  pl.BlockSpec((1, 1, tn), lambda i, j: (i, 0, j))],
        out_specs=pl.BlockSpec((1, r, tn), lambda i, j: (i, 0, j)),
        out_shape=jax.ShapeDtypeStruct((depth, r, d6), F32),
        compiler_params=_cparams(2),
        name="ada_table",
    )(cin, ada_w, ada_b.reshape(depth, 1, d6))


def _conf_in_kernel(x_ref, g_ref, sc_ref, sh_ref, w_ref, z_ref):
    h = _modnorm(x_ref[...], g_ref[...], sc_ref[0], sh_ref[0])
    y = jnp.dot(h.astype(BF16), w_ref[...], preferred_element_type=F32)
    ci = y.shape[1] // 2
    z_ref[...] = (y[:, :ci] * jax.nn.sigmoid(y[:, ci:])).astype(z_ref.dtype)


def _conf_in(x, mods, seq_len, norm_g, w_in):
    t, d = x.shape
    tm = _row_tile(seq_len, ROW_TILE)
    ci = w_in.shape[1] // 2
    return pl.pallas_call(
        _conf_in_kernel,
        grid=(t // tm,),
        in_specs=[_row_spec(tm, d), _full_spec((1, d)),
                  _mod_spec(mods.shape[0], tm, seq_len, d, 1),
                  _mod_spec(mods.shape[0], tm, seq_len, d, 0),
                  _full_spec(w_in.shape)],
        out_specs=_row_spec(tm, ci),
        out_shape=jax.ShapeDtypeStruct((t, ci), BF16),
        compiler_params=_cparams(1),
        name="conf_in",
    )(x, norm_g.reshape(1, d), mods, mods, w_in)


def _dwconv_kernel(z_ref, w_ref, b_ref, o_ref, src_scr, *, stride, chunk):
    seq_len, cb = z_ref.shape
    taps = w_ref.shape[0]
    half = taps // 2
    aligned = stride % chunk == 0
    pad = 0 if aligned else half * stride
    if pad:
        src_scr[pl.ds(0, pad), :] = jnp.zeros((pad, cb), F32)
        src_scr[pl.ds(pad + seq_len, pad), :] = jnp.zeros((pad, cb), F32)
    src_scr[pl.ds(pad, seq_len), :] = z_ref[...].astype(F32)
    w = w_ref[...]
    bias = b_ref[...]
    for r0 in range(0, seq_len, chunk):
        acc = jnp.broadcast_to(bias, (chunk, cb))
        for k in range(taps):
            lo = r0 + (k - half) * stride
            if aligned and (lo < 0 or lo + chunk > seq_len):
                continue
            acc = acc + w[k:k + 1, :] * src_scr[pl.ds(lo + pad, chunk), :]
        o_ref[pl.ds(r0, chunk), :] = acc.astype(o_ref.dtype)


def _dwconv(z, seq_len, stride, w, b):
    t, c = z.shape
    taps = w.shape[0]
    cb = min(c, 2 * V7X_LANES)
    chunk = _row_tile(seq_len, DWCONV_CHUNK)
    pad = 0 if stride % chunk == 0 else (taps // 2) * stride
    return pl.pallas_call(
        functools.partial(_dwconv_kernel, stride=stride, chunk=chunk),
        grid=(t // seq_len, c // cb),
        in_specs=[pl.BlockSpec((seq_len, cb), lambda s, j: (s, j)),
                  pl.BlockSpec((taps, cb), lambda s, j: (0, j)),
                  pl.BlockSpec((1, cb), lambda s, j: (0, j))],
        out_specs=pl.BlockSpec((seq_len, cb), lambda s, j: (s, j)),
        out_shape=jax.ShapeDtypeStruct((t, c), BF16),
        scratch_shapes=[pltpu.VMEM((seq_len + 2 * pad, cb), F32)],
        compiler_params=_cparams(2),
        name="dwconv",
    )(z, w, b.reshape(1, c))


def _conf_out_kernel(z_ref, lg_ref, lb_ref, w_ref, x_ref, g1_ref, o_ref):
    z = z_ref[...].astype(F32)
    mu = jnp.mean(z, axis=-1, keepdims=True)
    zc = z - mu
    var = jnp.mean(zc * zc, axis=-1, keepdims=True)
    y = zc * lax.rsqrt(var + LN_EPS) * lg_ref[...] + lb_ref[...]
    m = jnp.dot(_silu(y).astype(BF16), w_ref[...], preferred_element_type=F32)
    o_ref[...] = x_ref[...] + g1_ref[0] * m


def _conf_out(z, x, mods, seq_len, ln_g, ln_b, w_out):
    t, d = x.shape
    ci = z.shape[1]
    tm = _row_tile(seq_len, ROW_TILE)
    return pl.pallas_call(
        _conf_out_kernel,
        grid=(t // tm,),
        in_specs=[_row_spec(tm, ci), _full_spec((1, ci)), _full_spec((1, ci)),
                  _full_spec(w_out.shape), _row_spec(tm, d),
                  _mod_spec(mods.shape[0], tm, seq_len, d, 2)],
        out_specs=_row_spec(tm, d),
        out_shape=jax.ShapeDtypeStruct((t, d), F32),
        compiler_params=_cparams(1),
        name="conf_out",
    )(z, ln_g.reshape(1, ci), ln_b.reshape(1, ci), w_out, x, mods)


def _conformer(x, mods, seq_len, stride, norm_g, w_in, dw, dw_b, ln_g, ln_b, w_out):
    z = _conf_in(x, mods, seq_len, norm_g, w_in)
    z = _dwconv(z, seq_len, stride, dw, dw_b)
    return _conf_out(z, x, mods, seq_len, ln_g, ln_b, w_out)


def _sc_kernel(x_ref, g_ref, sc_ref, sh_ref, win_ref, cw_ref, wout_ref, g1_ref, o_ref, *, period):
    x = x_ref[...]
    tm, d = x.shape
    h = _modnorm(x, g_ref[...], sc_ref[0], sh_ref[0])
    y = jnp.dot(h.astype(BF16), win_ref[...], preferred_element_type=F32)
    gb, gc, v = y[:, :d], y[:, d:2 * d], y[:, 2 * d:]
    u = gc * v
    pos = lax.broadcasted_iota(I32, (tm, 1), 0) % period
    u_prev = jnp.where(pos == 0, 0.0, pltpu.roll(u, 1, 0))
    u_next = jnp.where(pos == period - 1, 0.0, pltpu.roll(u, tm - 1, 0))
    cw = cw_ref[...]
    conv = cw[0:1, :] * u_prev + cw[1:2, :] * u + cw[2:3, :] * u_next
    m = jnp.dot((gb * conv).astype(BF16), wout_ref[...], preferred_element_type=F32)
    o_ref[...] = x + g1_ref[0] * m


def _short_conv(x, mods, seq_len, period, norm_g, w_in, conv_w, w_out):
    t, d = x.shape
    tm = _row_tile(seq_len, ROW_TILE)
    assert tm % period == 0 and conv_w.shape[0] == 3
    return pl.pallas_call(
        functools.partial(_sc_kernel, period=period),
        grid=(t // tm,),
        in_specs=[_row_spec(tm, d), _full_spec((1, d)),
                  _mod_spec(mods.shape[0], tm, seq_len, d, 1),
                  _mod_spec(mods.shape[0], tm, seq_len, d, 0),
                  _full_spec(w_in.shape), _full_spec(conv_w.shape), _full_spec(w_out.shape),
                  _mod_spec(mods.shape[0], tm, seq_len, d, 2)],
        out_specs=_row_spec(tm, d),
        out_shape=jax.ShapeDtypeStruct((t, d), F32),
        compiler_params=_cparams(1),
        name="short_conv",
    )(x, norm_g.reshape(1, d), mods, mods, w_in, conv_w, w_out, mods)


def _s5_prep_kernel(are_ref, aim_ref, ldt_ref, bre_ref, bim_ref, cre_ref, cim_ref,
                    a16r_ref, a16i_ref, winr_ref, wini_ref, woutr_ref, wouti_ref, k_ref, *, reverse):
    a_re = jnp.minimum(are_ref[0], S5_DT_FLOOR)
    a_im = aim_ref[0]
    dt = jnp.exp(ldt_ref[0])
    b_re, b_im = bre_ref[0], bim_ref[0]
    c_re, c_im = cre_ref[0], cim_ref[0]
    cg = b_re.shape[1]
    tc = S5_CHUNK

    def power(n):
        mag = jnp.exp((n * dt) * a_re)
        ang = (n * dt) * a_im
        return mag * jnp.cos(ang), mag * jnp.sin(ang)

    abar_re, abar_im = power(1)
    den = a_re * a_re + a_im * a_im
    n_re = abar_re - 1.0
    n_im = abar_im
    k_re = (n_re * a_re + n_im * a_im) / den
    k_im = (n_im * a_re - n_re * a_im) / den
    bb_re = k_re * b_re - k_im * b_im
    bb_im = k_re * b_im + k_im * b_re

    e16r, e16i = power(tc)
    a16r_ref[0] = e16r
    a16i_ref[0] = e16i
    for t in range(tc):
        er, ei = power(t if reverse else tc - 1 - t)
        winr_ref[0, :, t * cg:(t + 1) * cg, :] = er * bb_re - ei * bb_im
        wini_ref[0, :, t * cg:(t + 1) * cg, :] = er * bb_im + ei * bb_re
        er, ei = power(tc - t if reverse else t + 1)
        woutr_ref[0, :, t * cg:(t + 1) * cg, :] = c_re * er - c_im * ei
        wouti_ref[0, :, t * cg:(t + 1) * cg, :] = -(c_re * ei + c_im * er)
        er, ei = power(t)
        m_re = c_re * er - c_im * ei
        m_im = c_re * ei + c_im * er
        k_ref[0, :, t * cg:(t + 1) * cg, :] = (
            jnp.einsum('gap,gbp->gab', m_re, bb_re, preferred_element_type=F32, precision=HIGHEST)
            - jnp.einsum('gap,gbp->gab', m_im, bb_im, preferred_element_type=F32, precision=HIGHEST))


def _s5_prep(a_re, a_im, log_dt, b_re, b_im, c_re, c_im, reverse):
    g, p = a_re.shape
    cg = b_re.shape[2]
    gb = min(g, V7X_SUBLANES)
    rows = S5_CHUNK * cg
    a4 = lambda a: a.reshape(1, g, 1, p)
    bt = lambda b: jnp.swapaxes(b, 1, 2).reshape(1, g, cg, p)
    spec_a = pl.BlockSpec((1, gb, 1, p), lambda i: (0, i, 0, 0))
    spec_b = pl.BlockSpec((1, gb, cg, p), lambda i: (0, i, 0, 0))
    spec_w = pl.BlockSpec((1, gb, rows, p), lambda i: (0, i, 0, 0))
    outs = pl.pallas_call(
        functools.partial(_s5_prep_kernel, reverse=reverse),
        grid=(g // gb,),
        in_specs=[spec_a, spec_a, pl.BlockSpec((1, gb, 1, 1), lambda i: (0, i, 0, 0)),
                  spec_b, spec_b, spec_b, spec_b],
        out_specs=[spec_a, spec_a, spec_w, spec_w, spec_w, spec_w,
                   pl.BlockSpec((1, gb, rows, cg), lambda i: (0, i, 0, 0))],
        out_shape=[jax.ShapeDtypeStruct((1, g, 1, p), F32)] * 2
        + [jax.ShapeDtypeStruct((1, g, rows, p), F32)] * 4
        + [jax.ShapeDtypeStruct((1, g, rows, cg), F32)],
        compiler_params=_cparams(1),
        name="s5_prep",
    )(a4(a_re), a4(a_im), log_dt.reshape(1, g, 1, 1), bt(b_re), bt(b_im),
      c_re.reshape(1, g, cg, p), c_im.reshape(1, g, cg, p))
    a16r, a16i, winr, wini, woutr, wouti, kk = [o[0] for o in outs]
    return a16r, a16i, winr, wini, woutr, wouti, kk


def _pair_blockdiag(w):
    g, r, p = w.shape
    w = w.reshape(g // 2, 2, r, p)
    z = jnp.zeros_like(w[:, 0])
    top = jnp.concatenate([w[:, 0], z], axis=-1)
    bot = jnp.concatenate([z, w[:, 1]], axis=-1)
    return jnp.concatenate([top, bot], axis=1)


def _toeplitz(kk, cg, reverse):
    g = kk.shape[0]
    tc = S5_CHUNK
    k5 = kk.reshape(g, tc, cg, cg)
    t_in = jnp.arange(tc)[None, :]
    t_out = jnp.arange(tc)[:, None]
    lag = (t_in - t_out) if reverse else (t_out - t_in)
    blocks = jnp.where((lag >= 0)[None, :, :, None, None], k5[:, jnp.clip(lag, 0, tc - 1)], 0.0)
    return blocks.transpose(0, 1, 3, 2, 4).reshape(g, tc * cg, tc * cg)


def _s5_core_kernel(vc_ref, vx_ref, a16r_ref, a16i_ref, winr_ref, wini_ref, woutr_ref, wouti_ref,
                    tz_ref, yc_ref, yx_ref, sre, sim, hre, him, *, n_seq):
    r = vc_ref.shape[1]
    n_cc = vc_ref.shape[2]
    n_col = n_cc + vx_ref.shape[2]
    n_chunks = n_col // n_seq
    n_ctx_chunks = n_cc // n_seq
    v = jnp.concatenate([jnp.concatenate([vc_ref[half], vx_ref[half]], axis=-1)
                         for half in range(2)], axis=0)
    for direction in range(2):
        sre[...] = lax.dot_general(v, winr_ref[direction, 0], (((0,), (0,)), ((), ())),
                                   preferred_element_type=F32)
        sim[...] = lax.dot_general(v, wini_ref[direction, 0], (((0,), (0,)), ((), ())),
                                   preferred_element_type=F32)
        ar = a16r_ref[direction, 0]
        ai = a16i_ref[direction, 0]

        def step(k, carry):
            h_r, h_i = carry
            rows = pl.ds(pl.multiple_of(k * n_seq, n_seq), n_seq)
            hre[rows, :] = h_r
            him[rows, :] = h_i
            return (ar * h_r - ai * h_i + sre[rows, :], ar * h_i + ai * h_r + sim[rows, :])

        zero = jnp.zeros((n_seq, sre.shape[1]), F32)
        if direction == 0:
            lax.fori_loop(0, n_chunks, step, (zero, zero))
        else:
            mid = lax.fori_loop(0, n_ctx_chunks,
                                lambda i, c: step(n_ctx_chunks - 1 - i, c), (zero, zero))
            lax.fori_loop(0, n_chunks - n_ctx_chunks,
                          lambda i, c: step(n_chunks - 1 - i, c), mid)

        h_r = hre[...].astype(BF16)
        h_i = him[...].astype(BF16)
        for half in range(2):
            rows = pl.ds(half * r, r)
            part = (lax.dot_general(woutr_ref[direction, 0, rows, :], h_r,
                                    (((1,), (1,)), ((), ())), preferred_element_type=F32)
                    + lax.dot_general(wouti_ref[direction, 0, rows, :], h_i,
                                      (((1,), (1,)), ((), ())), preferred_element_type=F32)
                    + jnp.dot(tz_ref[direction, half], v[half * r:(half + 1) * r, :],
                              preferred_element_type=F32))
            if direction == 0:
                yc_ref[half] = part[:, :n_cc]
                yx_ref[half] = part[:, n_cc:]
            else:
                yc_ref[half] = yc_ref[half] + part[:, :n_cc]
                yx_ref[half] = yx_ref[half] + part[:, n_cc:]


def _s5_core(vc, vx, ops_fwd, ops_bwd, n_seq):
    g, r, n_cc = vc.shape
    n_cx = vx.shape[2]
    cg = r // S5_CHUNK
    stacked = []
    for idx in range(6):
        stacked.append(jnp.stack([ops_fwd[idx], ops_bwd[idx]]))
    a16r, a16i, winr, wini, woutr, wouti = stacked
    p = a16r.shape[-1]
    pair_vec = lambda a: a.reshape(2, g // 2, 1, 2 * p)
    pair_mat = lambda w: jnp.stack([_pair_blockdiag(w[0]), _pair_blockdiag(w[1])]).astype(BF16)
    tz = jnp.stack([_toeplitz(ops_fwd[6], cg, False), _toeplitz(ops_bwd[6], cg, True)]).astype(BF16)
    vec_spec = pl.BlockSpec((2, 1, 1, 2 * p), lambda i: (0, i, 0, 0))
    mat_spec = pl.BlockSpec((2, 1, 2 * r, 2 * p), lambda i: (0, i, 0, 0))
    col_spec = lambda n: pl.BlockSpec((2, r, n), lambda i: (i, 0, 0))
    return pl.pallas_call(
        functools.partial(_s5_core_kernel, n_seq=n_seq),
        grid=(g // 2,),
        in_specs=[col_spec(n_cc), col_spec(n_cx),
                  vec_spec, vec_spec, mat_spec, mat_spec, mat_spec, mat_spec,
                  pl.BlockSpec((2, 2, r, r), lambda i: (0, i, 0, 0))],
        out_specs=[col_spec(n_cc), col_spec(n_cx)],
        out_shape=[jax.ShapeDtypeStruct((g, r, n_cc), F32), jax.ShapeDtypeStruct((g, r, n_cx), F32)],
        scratch_shapes=[pltpu.VMEM((n_cc + n_cx, 2 * p), F32)] * 4,
        compiler_params=_cparams(1),
        name="s5_core",
    )(vc, vx, pair_vec(a16r), pair_vec(a16i), pair_mat(winr), pair_mat(wini),
      pair_mat(woutr), pair_mat(wouti), tz)


def _s5_cols_kernel(x_ref, g_ref, sc_ref, sh_ref, v_ref, row_scr):
    nb, r, d = x_ref.shape
    g = v_ref.shape[0]
    cg = d // g
    tc = v_ref.shape[1] // cg
    nk = r // tc
    x = x_ref[...]
    y = x * lax.rsqrt(jnp.mean(x * x, axis=-1, keepdims=True) + RMS_EPS)
    h = ((y * g_ref[...]) * (1.0 + sc_ref[...]) + sh_ref[...]).reshape(nb * r, d)
    n_lane_blocks = row_scr.shape[0]
    gl = V7X_LANES // cg
    for c in range(n_lane_blocks):
        row_scr[c] = h[:, c * V7X_LANES:(c + 1) * V7X_LANES]
    for t in range(tc):
        for c in range(n_lane_blocks):
            z = jnp.concatenate([row_scr[c, pl.ds(t + tc * k, nb, stride=r), :] for k in range(nk)],
                                axis=0)
            v_ref[c * gl:(c + 1) * gl, t * cg:(t + 1) * cg, :] = (
                z.T.reshape(gl, cg, nk * nb).astype(v_ref.dtype))


def _s5_block_rows(n_seq, seq_len):
    return min(seq_len, S5_CHUNK * max(1, V7X_LANES // n_seq))


def _s5_cols(x, mods, n_seq, seq_len, norm_g, n_groups):
    d = x.shape[1]
    r = _s5_block_rows(n_seq, seq_len)
    cols = (r // S5_CHUNK) * n_seq
    rows = S5_CHUNK * (d // n_groups)
    n_mod = mods.shape[0]
    mod_spec = lambda j: pl.BlockSpec((n_mod, 1, d), lambda i: (0, 0, j))
    return pl.pallas_call(
        _s5_cols_kernel,
        grid=(seq_len // r,),
        in_specs=[pl.BlockSpec((n_seq, r, d), lambda i: (0, i, 0)), _full_spec((1, d)),
                  mod_spec(1), mod_spec(0)],
        out_specs=pl.BlockSpec((n_groups, rows, cols), lambda i: (0, 0, i)),
        out_shape=jax.ShapeDtypeStruct((n_groups, rows, (seq_len // S5_CHUNK) * n_seq), BF16),
        scratch_shapes=[pltpu.VMEM((d // V7X_LANES, n_seq * r, V7X_LANES), F32)],
        compiler_params=_cparams(1),
        name="s5_cols",
    )(x.reshape(n_seq, seq_len, d), norm_g.reshape(1, d), mods, mods)


def _s5_rows_kernel(y_ref, o_ref, row_scr):
    nb, r, d = o_ref.shape
    g = y_ref.shape[0]
    cg = d // g
    tc = y_ref.shape[1] // cg
    nk = r // tc
    n_lane_blocks = row_scr.shape[0]
    gl = V7X_LANES // cg
    for t in range(tc):
        for c in range(n_lane_blocks):
            z = y_ref[c * gl:(c + 1) * gl, t * cg:(t + 1) * cg, :].reshape(V7X_LANES, nk * nb).T
            for k in range(nk):
                row_scr[c, pl.ds(t + tc * k, nb, stride=r), :] = z[k * nb:(k + 1) * nb, :]
    o_ref[...] = jnp.concatenate([row_scr[c] for c in range(n_lane_blocks)],
                                 axis=-1).reshape(nb, r, d)


def _s5_rows(y, n_seq, seq_len):
    n_groups, rows, _ = y.shape
    d = n_groups * (rows // S5_CHUNK)
    r = _s5_block_rows(n_seq, seq_len)
    cols = (r // S5_CHUNK) * n_seq
    out = pl.pallas_call(
        _s5_rows_kernel,
        grid=(seq_len // r,),
        in_specs=[pl.BlockSpec((n_groups, rows, cols), lambda i: (0, 0, i))],
        out_specs=pl.BlockSpec((n_seq, r, d), lambda i: (0, i, 0)),
        out_shape=jax.ShapeDtypeStruct((n_seq, seq_len, d), F32),
        scratch_shapes=[pltpu.VMEM((d // V7X_LANES, n_seq * r, V7X_LANES), F32)],
        compiler_params=_cparams(1),
        name="s5_rows",
    )(y)
    return out.reshape(n_seq * seq_len, d)


def _s5_head_kernel(x_ref, y_ref, g_ref, sc_ref, sh_ref, d_ref, w_ref, g1_ref, o_ref):
    x = x_ref[...]
    h = _modnorm(x, g_ref[...], sc_ref[0], sh_ref[0])
    a = jax.nn.gelu(y_ref[...] + d_ref[...] * h, approximate=True)
    z = jnp.dot(a.astype(BF16), w_ref[...], preferred_element_type=F32)
    dm = z.shape[1] // 2
    o_ref[...] = x + g1_ref[0] * (z[:, :dm] * jax.nn.sigmoid(z[:, dm:]))


def _s5_head(x, y, mods, seq_len, norm_g, d_skip, w_glu):
    t, d = x.shape
    tm = _row_tile(seq_len, ROW_TILE)
    return pl.pallas_call(
        _s5_head_kernel,
        grid=(t // tm,),
        in_specs=[_row_spec(tm, d), _row_spec(tm, d), _full_spec((1, d)),
                  _mod_spec(mods.shape[0], tm, seq_len, d, 1),
                  _mod_spec(mods.shape[0], tm, seq_len, d, 0),
                  _full_spec((1, d)), _full_spec(w_glu.shape),
                  _mod_spec(mods.shape[0], tm, seq_len, d, 2)],
        out_specs=_row_spec(tm, d),
        out_shape=jax.ShapeDtypeStruct((t, d), F32),
        compiler_params=_cparams(1),
        name="s5_head",
    )(x, y, norm_g.reshape(1, d), mods, mods, d_skip.reshape(1, d), w_glu, mods)


def _s5_mixer(x, ctx, mods_x, mods_c, n_seq, norm_g, a_re, a_im, log_dt, b_re, b_im, c_re, c_im,
              d_skip, w_glu):
    s_len, c_len = x.shape[0] // n_seq, ctx.shape[0] // n_seq
    g = a_re.shape[1]
    ops = [_s5_prep(a_re[k], a_im[k], log_dt[k], b_re[k], b_im[k], c_re[k], c_im[k], bool(k))
           for k in range(2)]
    vc = _s5_cols(ctx, mods_c, n_seq, c_len, norm_g, g)
    vx = _s5_cols(x, mods_x, n_seq, s_len, norm_g, g)
    y_c, y_x = _s5_core(vc, vx, ops[0], ops[1], n_seq)
    x = _s5_head(x, _s5_rows(y_x, n_seq, s_len), mods_x, s_len, norm_g, d_skip, w_glu)
    ctx = _s5_head(ctx, _s5_rows(y_c, n_seq, c_len), mods_c, c_len, norm_g, d_skip, w_glu)
    return x, ctx


def _pack_pairs(lo, hi):
    lo_bits = lax.bitcast_convert_type(lo.astype(BF16).astype(F32), U32)
    hi_bits = lax.bitcast_convert_type(hi.astype(BF16).astype(F32), U32)
    return lax.shift_right_logical(lo_bits, jnp.uint32(16)) | (hi_bits & jnp.uint32(0xFFFF0000))


def _unpack_pairs(words):
    lo = lax.bitcast_convert_type(lax.shift_left(words, jnp.uint32(16)), F32)
    hi = lax.bitcast_convert_type(words & jnp.uint32(0xFFFF0000), F32)
    return lo, hi


def _pack_row_chunks(rows, pp):
    lanes = V7X_LANES
    return [_pack_pairs(rows[:, c * lanes:(c + 1) * lanes], rows[:, (c + pp) * lanes:(c + pp + 1) * lanes])
            for c in range(pp)]


def _unpack_row_chunks(chunks):
    pairs = [_unpack_pairs(w) for w in chunks]
    return jnp.concatenate([lo for lo, _ in pairs] + [hi for _, hi in pairs], axis=-1)


def _router_kernel(x_ref, g_ref, sc_ref, sh_ref, wr_ref, br_ref, h_ref, r_ref, c_ref, *, n_groups,
                   epg):
    h = _modnorm(x_ref[...], g_ref[...], sc_ref[0], sh_ref[0])
    tm = x_ref.shape[0]
    pp = h_ref.shape[0] // tm
    for c, words in enumerate(_pack_row_chunks(h, pp)):
        h_ref[pl.ds(c, tm, stride=pp), :] = words
    h_hi = h.astype(BF16)
    h_lo = (h - h_hi.astype(F32)).astype(BF16)
    wr = wr_ref[...]
    w_hi = wr.astype(BF16)
    w_lo = (wr - w_hi.astype(F32)).astype(BF16)
    nt = (((1,), (1,)), ((), ()))
    logits = (lax.dot_general(w_hi, h_hi, nt, preferred_element_type=F32)
              + lax.dot_general(w_lo, h_hi, nt, preferred_element_type=F32)
              + lax.dot_general(w_hi, h_lo, nt, preferred_element_type=F32)) + br_ref[...]
    row = lax.broadcasted_iota(I32, logits.shape, 0)
    far = jnp.int32(1 << 20)

    def first_max(vals):
        m = jnp.max(vals, axis=0, keepdims=True)
        return m, jnp.min(jnp.where(vals == m, row, far), axis=0, keepdims=True)

    is_group = row < n_groups
    gl = jnp.where(is_group, logits, NEG_BIG)
    gmax, gidx = first_max(gl)
    gsum = jnp.sum(jnp.where(is_group, jnp.exp(gl - gmax), 0.0), axis=0, keepdims=True)
    g_w = 1.0 / gsum
    lo = n_groups + gidx * epg
    le = jnp.where((row >= lo) & (row < lo + epg), logits, NEG_BIG)
    m1, i1 = first_max(le)
    m2, i2 = first_max(jnp.where(row == i1, NEG_BIG, le))
    ratio = jnp.exp(m2 - m1)
    w1 = g_w / (1.0 + ratio)
    w2 = g_w * ratio / (1.0 + ratio)
    e1 = (i1 - n_groups).astype(F32)
    e2 = (i2 - n_groups).astype(F32)
    out_row = lax.broadcasted_iota(I32, r_ref.shape, 0)
    r_ref[...] = jnp.where(out_row == 0, e1, jnp.where(out_row == 1, e2, jnp.where(
        out_row == 2, w1, jnp.where(out_row == 3, w2, 0.0))))
    chosen = ((row == i1) | (row == i2)).astype(F32)
    c_ref[0] = jnp.broadcast_to(jnp.sum(chosen, axis=1, keepdims=True), c_ref.shape[1:])


def _router(x, mods, seq_len, norm_g, wr, br, n_groups, epg):
    t, d = x.shape
    tm = _row_tile(seq_len, ROW_TILE)
    pp = d // (2 * V7X_LANES)
    n_logit = wr.shape[0]
    return pl.pallas_call(
        functools.partial(_router_kernel, n_groups=n_groups, epg=epg),
        grid=(t // tm,),
        in_specs=[_row_spec(tm, d), _full_spec((1, d)),
                  _mod_spec(mods.shape[0], tm, seq_len, d, 4),
                  _mod_spec(mods.shape[0], tm, seq_len, d, 3),
                  _full_spec(wr.shape), _full_spec(br.shape)],
        out_specs=[pl.BlockSpec((tm * pp, V7X_LANES), lambda i: (i, 0)),
                   pl.BlockSpec((V7X_SUBLANES, tm), lambda i: (0, i)),
                   pl.BlockSpec((1, n_logit, V7X_LANES), lambda i: (i, 0, 0))],
        out_shape=[jax.ShapeDtypeStruct((t * pp, V7X_LANES), U32),
                   jax.ShapeDtypeStruct((V7X_SUBLANES, t), F32),
                   jax.ShapeDtypeStruct((t // tm, n_logit, V7X_LANES), F32)],
        compiler_params=_cparams(1),
        name="moe_router",
    )(x, norm_g.reshape(1, d), mods, mods, wr, br)


def _dispatch_lists(route, counts, n_block, cap, pp):
    t = route.shape[1]
    n_sb = t // n_block
    n_assign = n_block * TOP_K
    ids = route[0:TOP_K].T.astype(I32).reshape(n_sb, n_assign)
    asg = jnp.broadcast_to(jnp.arange(n_assign, dtype=I32)[None, :], (n_sb, n_assign))
    _, asg_s = lax.sort((ids, asg), dimension=1, is_stable=True, num_keys=1)
    offs = jnp.cumsum(counts, axis=1) - counts
    asg_s = jnp.pad(asg_s, ((0, 0), (0, cap - n_assign)), constant_values=n_assign)
    src = jnp.minimum(asg_s // TOP_K, n_block - 1) * pp
    return (counts.reshape(-1), offs.reshape(-1),
            src.reshape(n_sb, 1, cap), (asg_s * pp).reshape(n_sb, 1, cap))


def _moe_kernel(cnt_ref, off_ref, src_ref, dst_ref, h_ref, w13_ref, w2_ref, o_ref,
                lhs_a, lhs_b, ys_a, ys_b, *, n_exp, tile):
    sb = pl.program_id(0)
    eb = pl.program_id(1)
    n_local = w13_ref.shape[0]
    pp = w13_ref.shape[1] // (2 * V7X_LANES)
    de = w2_ref.shape[1]
    scratch = ((lhs_a, ys_a), (lhs_b, ys_b))

    def slab(row):
        return pl.ds(pl.multiple_of(row, pp), pp)

    @pl.when(eb == 0)
    def _():
        n_live = h_ref.shape[0] * TOP_K
        spare = o_ref.shape[1] - n_live
        o_ref[0, pl.ds(n_live, spare), :] = jnp.zeros((spare, V7X_LANES), U32)

    def run_tiles(q, base, sizes):
        starts = [base + sum(sizes[:k]) for k in range(len(sizes))]
        for (lhs_scr, _), start, size in zip(scratch, starts, sizes):
            stride = size + 1
            for mi in range(size):
                lhs_scr[pl.ds(mi, pp, stride=stride), :] = h_ref[slab(src_ref[0, 0, start + mi]), :]
        for (lhs_scr, ys_scr), size in zip(scratch, sizes):
            stride = size + 1
            lhs = _unpack_row_chunks([lhs_scr[pl.ds(c * stride, size), :] for c in range(pp)])
            hid = jnp.dot(lhs.astype(BF16), w13_ref[q], preferred_element_type=F32)
            act = _silu(hid[:, :de]) * hid[:, de:]
            ys = jnp.dot(act.astype(BF16), w2_ref[q], preferred_element_type=F32)
            for c, words in enumerate(_pack_row_chunks(ys, pp)):
                ys_scr[pl.ds(c * stride, size), :] = words
        for (_, ys_scr), start, size in zip(scratch, starts, sizes):
            stride = size + 1
            for mi in range(size):
                o_ref[0, slab(dst_ref[0, 0, start + mi]), :] = ys_scr[pl.ds(mi, pp, stride=stride), :]

    half = tile // 2

    def one_expert(q, carry):
        e = eb * n_local + q
        count = cnt_ref[sb * n_exp + e]
        seg = off_ref[sb * n_exp + e]
        n_pairs = lax.div(count, 2 * tile)

        def pair(j, c):
            run_tiles(q, seg + j * 2 * tile, (tile, tile))
            return c
        lax.fori_loop(0, n_pairs, pair, 0)

        rem = count - n_pairs * 2 * tile
        rem_base = seg + n_pairs * 2 * tile
        for hi, sizes in ((half, (half,)), (tile, (tile,)), (tile + half, (tile, half)),
                          (2 * tile, (tile, tile))):
            @pl.when((rem > hi - half) & (rem <= hi))
            def _():
                run_tiles(q, rem_base, sizes)
        return carry

    lax.fori_loop(0, n_local, one_expert, 0)


def _moe_tile(n_block, n_exp):
    return max(4 * V7X_SUBLANES, n_block * TOP_K // n_exp)


def _moe_experts(h, route, counts, w13, w2, n_block):
    n_exp, d, de2 = w13.shape
    de = de2 // 2
    pp = d // (2 * V7X_LANES)
    n_sb = h.shape[0] // (n_block * pp)
    n_assign = n_block * TOP_K
    tile = _moe_tile(n_block, n_exp)
    cap = -(-(n_assign + 2 * tile) // V7X_LANES) * V7X_LANES
    cnt, off, src_list, dst_list = _dispatch_lists(route, counts, n_block, cap, pp)
    out_slabs = n_assign + V7X_SUBLANES
    eb = MOE_EXPERTS_PER_STEP
    list_spec = pl.BlockSpec((1, 1, cap), lambda s, e, *_: (s, 0, 0), memory_space=pltpu.SMEM)
    grid_spec = pltpu.PrefetchScalarGridSpec(
        num_scalar_prefetch=2,
        grid=(n_sb, n_exp // eb),
        in_specs=[list_spec, list_spec,
                  pl.BlockSpec((n_block * pp, V7X_LANES), lambda s, e, *_: (s, 0),
                               pipeline_mode=pl.Buffered(1)),
                  pl.BlockSpec((eb, d, de2), lambda s, e, *_: (e, 0, 0)),
                  pl.BlockSpec((eb, de, d), lambda s, e, *_: (e, 0, 0))],
        out_specs=pl.BlockSpec((1, out_slabs * pp, V7X_LANES), lambda s, e, *_: (s, 0, 0)),
        scratch_shapes=[pltpu.VMEM((pp * (tile + 1), V7X_LANES), U32)] * 4)
    return pl.pallas_call(
        functools.partial(_moe_kernel, n_exp=n_exp, tile=tile),
        grid_spec=grid_spec,
        out_shape=jax.ShapeDtypeStruct((n_sb, out_slabs * pp, V7X_LANES), U32),
        compiler_params=_cparams(2),
        name="moe_experts",
    )(cnt, off, src_list, dst_list, h, w13, w2)


def _residual_kernel(x_ref, y_ref, w_ref, g2_ref, fg_ref, o_ref, *, final_norm):
    tm = x_ref.shape[0]
    wts = w_ref[...]
    pp = y_ref.shape[1] // (tm * TOP_K)
    y = sum(wts[:, k:k + 1]
            * _unpack_row_chunks([y_ref[0, pl.ds(k * pp + c, tm, stride=TOP_K * pp), :]
                                  for c in range(pp)])
            for k in range(TOP_K))
    x = x_ref[...] + g2_ref[0] * y
    if final_norm:
        x = x * lax.rsqrt(jnp.mean(x * x, axis=-1, keepdims=True) + RMS_EPS) * fg_ref[...]
    o_ref[...] = x


def _residual(x, y, wts, mods, seq_len, n_block, final_g, final_norm):
    t, d = x.shape
    tm = _row_tile(min(seq_len, n_block), ROW_TILE)
    per_sb = n_block // tm
    pp = d // (2 * V7X_LANES)
    return pl.pallas_call(
        functools.partial(_residual_kernel, final_norm=final_norm),
        grid=(t // tm,),
        in_specs=[_row_spec(tm, d),
                  pl.BlockSpec((1, tm * TOP_K * pp, V7X_LANES), lambda i: (i // per_sb, i % per_sb, 0)),
                  _row_spec(tm, TOP_K),
                  _mod_spec(mods.shape[0], tm, seq_len, d, 5), _full_spec((1, d))],
        out_specs=_row_spec(tm, d),
        out_shape=jax.ShapeDtypeStruct((t, d), F32),
        compiler_params=_cparams(1),
        name="moe_residual",
    )(x, y, wts, mods, final_g.reshape(1, d))


def _moe_block(t):
    n = min(t, MOE_BLOCK_TOKENS)
    while t % n:
        n //= 2
    return n


def _moe(x, mods, seq_len, norm_g, wr, br, w13, w2, n_groups, final_g, final_norm):
    n_exp = w13.shape[0]
    n_block = _moe_block(x.shape[0])
    h, route, tile_counts = _router(x, mods, seq_len, norm_g, wr, br, n_groups, n_exp // n_groups)
    counts = tile_counts[:, n_groups:n_groups + n_exp, 0].reshape(x.shape[0] // n_block, -1, n_exp)
    counts = jnp.sum(counts, axis=1).astype(I32)
    y = _moe_experts(h, route, counts, w13, w2, n_block)
    return _residual(x, y, route[TOP_K:2 * TOP_K].T, mods, seq_len, n_block, final_g, final_norm)


def kernel(x, c, ctx, c_ctx, ada_w, ada_b, norm1_g, norm2_g, conf_w_in, conf_dw, conf_dw_b, conf_ln_g, conf_ln_b, conf_w_out, sc_w_in, sc_conv, sc_w_out, s5_a_re, s5_a_im, s5_log_dt, s5_b_re, s5_b_im, s5_c_re, s5_c_im, s5_d, s5_w_glu, moe_wg, moe_bg, moe_we, moe_be, moe_w13, moe_w2, final_g):
    b, s, d = x.shape
    lc = ctx.shape[1]
    depth = ada_w.shape[0]
    n_groups = moe_wg.shape[-1]
    n_exp = moe_we.shape[-1]
    assert s % GRID_W == 0

    rows = (b + 1 + V7X_SUBLANES - 1) // V7X_SUBLANES * V7X_SUBLANES
    cin = jnp.zeros((rows, d), F32).at[:b].set(c).at[b].set(c_ctx)
    table = _ada_table(cin, ada_w, ada_b)

    xs = x.reshape(b * s, d)
    cs = ctx.reshape(b * lc, d)
    for i in range(depth):
        kind, j = i % N_MIXERS, i // N_MIXERS
        update_ctx = i < depth - 1
        mods_x = table[i, :b].reshape(b, 1, 6 * d)
        mods_c = table[i, b].reshape(1, 1, 6 * d)
        if kind == 0:
            args = (norm1_g[i], conf_w_in[j].astype(BF16), conf_dw[j], conf_dw_b[j],
                    conf_ln_g[j], conf_ln_b[j], conf_w_out[j].astype(BF16))
            xs = _conformer(xs, mods_x, s, GRID_W, *args)
            if update_ctx:
                cs = _conformer(cs, mods_c, lc, 1, *args)
        elif kind == 1:
            args = (norm1_g[i], sc_w_in[j].astype(BF16), sc_conv[j], sc_w_out[j].astype(BF16))
            xs = _short_conv(xs, mods_x, s, GRID_W, *args)
            if update_ctx:
                cs = _short_conv(cs, mods_c, lc, lc, *args)
        else:
            xs, cs_new = _s5_mixer(xs, cs, mods_x, mods_c, b, norm1_g[i], s5_a_re[j], s5_a_im[j],
                                   s5_log_dt[j], s5_b_re[j], s5_b_im[j], s5_c_re[j], s5_c_im[j],
                                   s5_d[j], s5_w_glu[j].astype(BF16))
            if update_ctx:
                cs = cs_new

        n_logit = -(-(n_groups + n_exp) // V7X_SUBLANES) * V7X_SUBLANES
        wr = jnp.zeros((n_logit, d), F32).at[:n_groups].set(moe_wg[i].T)
        wr = wr.at[n_groups:n_groups + n_exp].set(moe_we[i].T)
        br = jnp.zeros((n_logit, 1), F32).at[:n_groups, 0].set(moe_bg[i])
        br = br.at[n_groups:n_groups + n_exp, 0].set(moe_be[i])
        moe_args = (norm2_g[i], wr, br, moe_w13[i].astype(BF16), moe_w2[i].astype(BF16), n_groups,
                    final_g)
        xs = _moe(xs, mods_x, s, *moe_args, final_norm=(i == depth - 1))
        if update_ctx:
            cs = _moe(cs, mods_c, lc, *moe_args, final_norm=False)
    return xs.reshape(b, s, d)
```
